```python
import math
import jax, jax.numpy as jnp
from jax import lax
import numpy as np

D_MODEL = 4096
BATCH = 4
SEQ = 4096
DEPTH = 1

D_MIX = 2 * D_MODEL
D_SSD = D_MIX // 2
D_CONV = D_MIX - D_SSD
SSD_HEAD_DIM = 64
SSD_HEADS = D_SSD // SSD_HEAD_DIM
SSD_GROUPS = 8
SSD_HEADS_PER_GROUP = SSD_HEADS // SSD_GROUPS
SSD_STATE = 128
SSD_CONV_WIDTH = 4
SSD_CHUNK = 128
D_XBC = D_SSD + 2 * SSD_GROUPS * SSD_STATE
SC_CONV_WIDTH = 3
D_IN_PROJ = D_SSD + D_XBC + SSD_HEADS + 3 * D_CONV
PEER_HEADS = 8
PEER_N_KEYS = 128
PEER_N_EXPERTS = PEER_N_KEYS * PEER_N_KEYS
PEER_TOPK = 16
PEER_QUERY_DIM = 256
PEER_TOKEN_BLOCK = 128
RMS_EPS = 1e-6

kernel_name = "hybrid_ssd_shortconv_peer_block"


def rmsnorm(x, w, groups=1):
    shape = x.shape
    xf = x.astype(jnp.float32).reshape(*shape[:-1], groups, shape[-1] // groups)
    xf = xf * lax.rsqrt(jnp.mean(xf * xf, axis=-1, keepdims=True) + RMS_EPS)
    return (xf.reshape(shape) * w.astype(jnp.float32)).astype(x.dtype)


def causal_dwconv(x, w):
    k_w = w.shape[0]
    s = x.shape[1]
    xp = jnp.pad(x, ((0, 0), (k_w - 1, 0), (0, 0)))
    acc = xp[:, k_w - 1:k_w - 1 + s] * w[k_w - 1]
    for k in range(k_w - 1):
        acc = acc + xp[:, k:k + s] * w[k]
    return acc


def ssd_chunked_scan(xh, dt, a_neg, bm, cm):
    b, s = xh.shape[:2]
    nc = s // SSD_CHUNK

    def to_chunks(t):
        return jnp.moveaxis(t.reshape(b, nc, SSD_CHUNK, *t.shape[2:]), 1, 0)

    causal = jnp.tril(jnp.ones((SSD_CHUNK, SSD_CHUNK), dtype=bool))

    def step(state, inp):
        xc, dtc, bc, cc = inp
        acum = jnp.cumsum(dtc * a_neg, axis=1)
        seg = acum[:, :, None] - acum[:, None, :]
        decay = jnp.exp(jnp.where(causal[None, :, :, None, None], seg, -jnp.inf))
        cb = jnp.einsum('btgn,bsgn->btsg', cc, bc)
        xdt = xc * dtc[..., None]
        y_diag = jnp.einsum('btsg,btsgr,bsgrp->btgrp', cb, decay, xdt)
        y_off = jnp.einsum('btgn,bgrpn->btgrp', cc, state) * jnp.exp(acum)[..., None]
        to_end = jnp.exp(acum[:, -1:] - acum)
        new_state = (state * jnp.exp(acum[:, -1])[..., None, None]
                     + jnp.einsum('bsgn,bsgrp->bgrpn', bc, xdt * to_end[..., None]))
        return new_state, y_diag + y_off

    state0 = jnp.zeros((b, SSD_GROUPS, SSD_HEADS_PER_GROUP, SSD_HEAD_DIM, SSD_STATE), jnp.float32)
    _, ys = lax.scan(step, state0, (to_chunks(xh), to_chunks(dt), to_chunks(bm), to_chunks(cm)))
    return jnp.moveaxis(ys, 0, 1).reshape(xh.shape)


def mixer(xn, w_in, ssd_conv_w, ssd_conv_b, ssd_dt_bias, ssd_a_log, ssd_d, ssd_norm_w,
          sc_conv_w, sc_norm_w, w_out):
    b, s, _ = xn.shape
    proj = xn @ w_in
    o1 = D_SSD
    o2 = o1 + D_XBC
    o3 = o2 + SSD_HEADS
    o4 = o3 + D_CONV
    o5 = o4 + D_CONV
    z, xbc, dt_raw, sc_b, sc_c, sc_h = jnp.split(proj, [o1, o2, o3, o4, o5], axis=-1)

    xbc = jax.nn.silu(causal_dwconv(xbc, ssd_conv_w) + ssd_conv_b)
    xs, bm, cm = jnp.split(xbc, [D_SSD, D_SSD + SSD_GROUPS * SSD_STATE], axis=-1)
    dt = jax.nn.softplus(dt_raw.astype(jnp.float32) + ssd_dt_bias.astype(jnp.float32))
    a_neg = -jnp.exp(ssd_a_log.astype(jnp.float32))
    shp_h = (b, s, SSD_GROUPS, SSD_HEADS_PER_GROUP)
    xh = xs.astype(jnp.float32).reshape(*shp_h, SSD_HEAD_DIM)
    y = ssd_chunked_scan(
        xh, dt.reshape(shp_h), a_neg.reshape(SSD_GROUPS, SSD_HEADS_PER_GROUP),
        bm.astype(jnp.float32).reshape(b, s, SSD_GROUPS, SSD_STATE),
        cm.astype(jnp.float32).reshape(b, s, SSD_GROUPS, SSD_STATE))
    y = y + ssd_d.astype(jnp.float32).reshape(SSD_GROUPS, SSD_HEADS_PER_GROUP, 1) * xh
    y = y.reshape(b, s, D_SSD) * jax.nn.silu(z.astype(jnp.float32))
    y_ssd = rmsnorm(y, ssd_norm_w, groups=SSD_GROUPS).astype(xn.dtype)

    y_sc = sc_b * causal_dwconv(sc_c * sc_h, sc_conv_w)
    y_sc = rmsnorm(y_sc, sc_norm_w)

    return jnp.concatenate([y_ssd, y_sc], axis=-1) @ w_out


def peer_ffn(xn, w_query, sub_keys, u_tab, v_tab):
    b, s, d = xn.shape
    t = b * s
    xt = xn.reshape(t, d)
    q = (xt @ w_query).reshape(t, PEER_HEADS, 2, PEER_QUERY_DIM // 2)
    sc = jnp.einsum('thcd,hckd->thck', q.astype(jnp.float32), sub_keys.astype(jnp.float32))
    top_s, top_i = lax.top_k(sc, PEER_TOPK)
    kk = PEER_TOPK * PEER_TOPK
    cand_s = (top_s[:, :, 0, :, None] + top_s[:, :, 1, None, :]).reshape(t, PEER_HEADS, kk)
    cand_i = (top_i[:, :, 0, :, None] * PEER_N_KEYS + top_i[:, :, 1, None, :]).reshape(t, PEER_HEADS, kk)
    best_s, best_pos = lax.top_k(cand_s, PEER_TOPK)
    expert_idx = jnp.take_along_axis(cand_i, best_pos, axis=-1)
    gates = jax.nn.softmax(best_s, axis=-1).astype(xn.dtype)

    nblk = t // PEER_TOKEN_BLOCK
    hk = PEER_HEADS * PEER_TOPK

    def block(args):
        xb, ib, gb = args
        act = jax.nn.gelu(jnp.einsum('td,tkd->tk', xb, u_tab[ib]), approximate=False) * gb
        return jnp.einsum('tk,tkd->td', act, v_tab[ib])

    out = lax.map(block, (xt.reshape(nblk, PEER_TOKEN_BLOCK, d),
                          expert_idx.reshape(nblk, PEER_TOKEN_BLOCK, hk),
                          gates.reshape(nblk, PEER_TOKEN_BLOCK, hk)))
    return out.reshape(b, s, d)


def setup_inputs(seed: int = 0) -> dict:
    key = jax.random.key(seed)
    ks = jax.random.split(key, 20)
    f32 = jnp.float32
    L = DEPTH

    def nrm(k, shape, scale):
        return jax.random.normal(k, shape, f32) * scale

    def gain(k, shape):
        return 1.0 + 0.05 * jax.random.normal(k, shape, f32)

    dt0 = jnp.exp(jax.random.uniform(ks[4], (L, SSD_HEADS), f32)
                  * (math.log(0.1) - math.log(0.001)) + math.log(0.001))
    dt_bias = dt0 + jnp.log(-jnp.expm1(-dt0))
    a_log = jnp.log(jax.random.uniform(ks[5], (L, SSD_HEADS), f32, 1.0, 16.0))
    return {
        "x": nrm(ks[0], (BATCH, SEQ, D_MODEL), 1.0),
        "w_in": nrm(ks[1], (L, D_MODEL, D_IN_PROJ), D_MODEL ** -0.5),
        "ssd_conv_w": nrm(ks[2], (L, SSD_CONV_WIDTH, D_XBC), SSD_CONV_WIDTH ** -0.5),
        "ssd_conv_b": nrm(ks[3], (L, D_XBC), 0.01),
        "ssd_dt_bias": dt_bias,
        "ssd_a_log": a_log,
        "ssd_d": gain(ks[6], (L, SSD_HEADS)),
        "ssd_norm_w": gain(ks[7], (L, D_SSD)),
        "sc_conv_w": nrm(ks[8], (L, SC_CONV_WIDTH, D_CONV), SC_CONV_WIDTH ** -0.5),
        "sc_norm_w": gain(ks[9], (L, D_CONV)),
        "w_out": nrm(ks[10], (L, D_MIX, D_MODEL), D_MIX ** -0.5),
        "norm_mix_w": gain(ks[11], (L, D_MODEL)),
        "norm_ffn_w": gain(ks[12], (L, D_MODEL)),
        "peer_w_query": nrm(ks[13], (L, D_MODEL, PEER_HEADS * PEER_QUERY_DIM), D_MODEL ** -0.5),
        "peer_sub_keys": nrm(ks[14], (L, PEER_HEADS, 2, PEER_N_KEYS, PEER_QUERY_DIM // 2),
                             (PEER_QUERY_DIM // 2) ** -0.5),
        "peer_u": nrm(ks[15], (L, PEER_N_EXPERTS, D_MODEL), D_MODEL ** -0.5),
        "peer_v": nrm(ks[16], (L, PEER_N_EXPERTS, D_MODEL), PEER_HEADS ** -0.5),
        "norm_final_w": gain(ks[17], (D_MODEL,)),
    }


def reference(x, w_in, ssd_conv_w, ssd_conv_b, ssd_dt_bias, ssd_a_log, ssd_d, ssd_norm_w,
              sc_conv_w, sc_norm_w, w_out, norm_mix_w, norm_ffn_w, peer_w_query,
              peer_sub_keys, peer_u, peer_v, norm_final_w):
    h = x
    for l in range(DEPTH):
        xn = rmsnorm(h, norm_mix_w[l])
        h = h + mixer(xn, w_in[l], ssd_conv_w[l], ssd_conv_b[l], ssd_dt_bias[l], ssd_a_log[l],
                      ssd_d[l], ssd_norm_w[l], sc_conv_w[l], sc_norm_w[l], w_out[l])
        xn = rmsnorm(h, norm_ffn_w[l])
        h = h + peer_ffn(xn, peer_w_query[l], peer_sub_keys[l], peer_u[l], peer_v[l])
    return rmsnorm(h, norm_final_w)
```

```python
import functools
import math

import jax
import jax.numpy as jnp
from jax import lax
from jax.experimental import pallas as pl
from jax.experimental.pallas import tpu as pltpu

F32 = jnp.float32
BF16 = jnp.bfloat16

D_MODEL = 4096
D_SSD = 4096
D_CONV = 4096
HEAD_DIM = 64
N_HEADS = 64
N_GROUPS = 8
HEADS_PER_GROUP = 8
D_STATE = 128
CHUNK = 128
GROUP_W = HEADS_PER_GROUP * HEAD_DIM
PEER_HEADS = 8
N_KEYS = 128
N_EXPERTS = N_KEYS * N_KEYS
TOPK = 16
RMS_EPS = 1e-6

VMEM_LIMIT_BYTES = 58 * 1024 * 1024
CARRY_ROWS = 8


def _cparams(sem):
    return pltpu.CompilerParams(dimension_semantics=sem, vmem_limit_bytes=VMEM_LIMIT_BYTES)


def _rmsnorm_kernel(x_ref, w_ref, o_ref, *, transpose_out):
    x = x_ref[...]
    ms = jnp.mean(x * x, axis=-1, keepdims=True)
    y = x * lax.rsqrt(ms + RMS_EPS) * w_ref[...]
    if transpose_out:
        o_ref[...] = y.T.astype(o_ref.dtype)
    else:
        o_ref[...] = y.astype(o_ref.dtype)


def rmsnorm_rows(x, w, *, transpose_out=False, tm=256):
    t, d = x.shape
    out_shape = (d, t) if transpose_out else (t, d)
    out_spec = pl.BlockSpec((d, tm), lambda i: (0, i)) if transpose_out else pl.BlockSpec((tm, d), lambda i: (i, 0))
    return pl.pallas_call(
        functools.partial(_rmsnorm_kernel, transpose_out=transpose_out),
        grid=(t // tm,),
        in_specs=[pl.BlockSpec((tm, d), lambda i: (i, 0)), pl.BlockSpec((1, d), lambda i: (0, 0))],
        out_specs=out_spec,
        out_shape=jax.ShapeDtypeStruct(out_shape, BF16),
        compiler_params=_cparams(("parallel",)),
        name="rmsnorm_rows",
    )(x, w.reshape(1, d))


def _matmul_kernel(*refs, nk, has_res):
    if has_res:
        a_ref, b_ref, r_ref, o_ref = refs[:4]
    else:
        a_ref, b_ref, o_ref = refs[:3]
        r_ref = None
    if nk == 1:
        acc = jnp.dot(a_ref[...], b_ref[...], preferred_element_type=F32)
        if has_res:
            acc = acc + r_ref[...]
        o_ref[...] = acc.astype(o_ref.dtype)
        return
    acc_ref = refs[-1]
    k = pl.program_id(2)

    @pl.when(k == 0)
    def _():
        acc_ref[...] = jnp.zeros_like(acc_ref)

    acc_ref[...] += jnp.dot(a_ref[...], b_ref[...], preferred_element_type=F32)

    @pl.when(k == nk - 1)
    def _():
        acc = acc_ref[...]
        if has_res:
            acc = acc + r_ref[...]
        o_ref[...] = acc.astype(o_ref.dtype)


def matmul(a, b, *, tm, tn, tk, out_dtype, residual=None, name="matmul"):
    m, kdim = a.shape
    _, n = b.shape
    tm, tn, tk = min(tm, m), min(tn, n), min(tk, kdim)
    nk = kdim // tk
    has_res = residual is not None
    in_specs = [pl.BlockSpec((tm, tk), lambda i, j, k: (i, k)), pl.BlockSpec((tk, tn), lambda i, j, k: (k, j))]
    args = [a, b]
    if has_res:
        in_specs.append(pl.BlockSpec((tm, tn), lambda i, j, k: (i, j)))
        args.append(residual)
    scratch = [pltpu.VMEM((tm, tn), F32)] if nk > 1 else []
    return pl.pallas_call(
        functools.partial(_matmul_kernel, nk=nk, has_res=has_res),
        grid=(m // tm, n // tn, nk),
        in_specs=in_specs,
        out_specs=pl.BlockSpec((tm, tn), lambda i, j, k: (i, j)),
        out_shape=jax.ShapeDtypeStruct((m, n), out_dtype),
        scratch_shapes=scratch,
        compiler_params=_cparams(("parallel", "parallel", "arbitrary")),
        name=name,
    )(*args)


def _silu(x):
    return x * (1.0 / (1.0 + jnp.exp(-x)))


def _softplus(x):
    return jnp.maximum(x, 0.0) + jnp.log(1.0 + jnp.exp(-jnp.abs(x)))


def _ssd_kernel(z_ref, xs_ref, b_ref, c_ref, dt_ref,
                cwx_ref, cwb_ref, cwc_ref, cbx_ref, cbb_ref, cbc_ref,
                dtb_ref, alog_ref, dexp_ref, nw_ref,
                o_ref,
                state_ref, ext_ref):
    g = pl.program_id(1)
    c = pl.program_id(2)
    L = CHUNK
    W = GROUP_W
    N = D_STATE

    @pl.when(c == 0)
    def _():
        state_ref[...] = jnp.zeros_like(state_ref)
        ext_ref[0:CARRY_ROWS, :] = jnp.zeros((CARRY_ROWS, W + 2 * N), F32)

    ext_ref[CARRY_ROWS:CARRY_ROWS + L, 0:W] = xs_ref[...].astype(F32)
    ext_ref[CARRY_ROWS:CARRY_ROWS + L, W:W + N] = b_ref[...].astype(F32)
    ext_ref[CARRY_ROWS:CARRY_ROWS + L, W + N:W + 2 * N] = c_ref[...].astype(F32)

    def conv(lo, hi, w_ref, bias_ref):
        acc = ext_ref[CARRY_ROWS:CARRY_ROWS + L, lo:hi] * w_ref[3:4, :]
        for k in range(3):
            acc = acc + ext_ref[CARRY_ROWS - 3 + k:CARRY_ROWS - 3 + k + L, lo:hi] * w_ref[k:k + 1, :]
        return _silu(acc + bias_ref[...])

    xs = conv(0, W, cwx_ref, cbx_ref)
    bmat = conv(W, W + N, cwb_ref, cbb_ref)
    cmat = conv(W + N, W + 2 * N, cwc_ref, cbc_ref)
    ext_ref[0:CARRY_ROWS, :] = ext_ref[L:L + CARRY_ROWS, :]

    dt_all = _softplus(dt_ref[...] + dtb_ref[...])
    da_all = dt_all * (-jnp.exp(alog_ref[...]))
    shift = (128 - g * HEADS_PER_GROUP) % 128
    dt_g = pltpu.roll(dt_all, shift, 1)
    da_g = pltpu.roll(da_all, shift, 1)

    row = lax.broadcasted_iota(jnp.int32, (L, L), 0)
    col = lax.broadcasted_iota(jnp.int32, (L, L), 1)
    causal = row >= col
    tril = causal.astype(BF16)
    hi = da_g.astype(BF16)
    r1 = da_g - hi.astype(F32)
    mid = r1.astype(BF16)
    lo = (r1 - mid.astype(F32)).astype(BF16)
    acum = (jnp.dot(tril, hi, preferred_element_type=F32)
            + jnp.dot(tril, mid, preferred_element_type=F32)
            + jnp.dot(tril, lo, preferred_element_type=F32))
    acum_t = acum.T
    a_last = acum[L - 1:L, :]

    er = lax.broadcasted_iota(jnp.int32, (128, W), 0)
    ec = lax.broadcasted_iota(jnp.int32, (128, W), 1)
    expand = (er == ec // HEAD_DIM).astype(BF16)

    def expand_heads(v):
        vh = v.astype(BF16)
        vl = (v - vh.astype(F32)).astype(BF16)
        return jnp.dot(vh, expand, preferred_element_type=F32) + jnp.dot(vl, expand, preferred_element_type=F32)

    dt_x = expand_heads(dt_g)
    ea_x = expand_heads(jnp.exp(acum))
    to_end_x = expand_heads(jnp.exp(a_last - acum))

    xdt = xs * dt_x
    xdt_b = xdt.astype(BF16)
    b_b = bmat.astype(BF16)
    c_b = cmat.astype(BF16)

    cb = lax.dot_general(c_b, b_b, (((1,), (1,)), ((), ())), preferred_element_type=F32)
    state = state_ref[...]
    y_off = jnp.dot(c_b, state.astype(BF16), preferred_element_type=F32) * ea_x

    y_parts = []
    for r in range(HEADS_PER_GROUP):
        seg = acum[:, r:r + 1] - acum_t[r:r + 1, :]
        decay = jnp.exp(jnp.where(causal, seg, -jnp.inf))
        m = (cb * decay).astype(BF16)
        y_parts.append(jnp.dot(m, xdt_b[:, r * HEAD_DIM:(r + 1) * HEAD_DIM], preferred_element_type=F32))
    y = jnp.concatenate(y_parts, axis=1) + y_off + dexp_ref[...] * xs

    upd = lax.dot_general(b_b, (xdt * to_end_x).astype(BF16), (((0,), (0,)), ((), ())),
                          preferred_element_type=F32)
    state_ref[...] = state * ea_x[L - 1:L, :] + upd

    y = y * _silu(z_ref[...].astype(F32))
    ms = jnp.mean(y * y, axis=-1, keepdims=True)
    o_ref[...] = (y * lax.rsqrt(ms + RMS_EPS) * nw_ref[...]).astype(o_ref.dtype)


def ssd_mixer(proj, dt_raw, conv_w, conv_b, dt_bias, a_log, d_skip, norm_w, *, batch, seq, col_off):
    t = batch * seq
    nchunk = seq // CHUNK
    W, N, L = GROUP_W, D_STATE, CHUNK
    zb, xb, bb, cb_ = col_off["z"] // W, col_off["xs"] // W, col_off["B"] // N, col_off["C"] // N

    def rows(b, g, c):
        return b * nchunk + c

    pad = 128 - N_HEADS
    dtb = jnp.pad(dt_bias.astype(F32), (0, pad)).reshape(1, 128)
    alog = jnp.pad(a_log.astype(F32), (0, pad)).reshape(1, 128)
    dexp = jnp.repeat(d_skip.astype(F32), HEAD_DIM).reshape(1, D_SSD)
    cw = conv_w.astype(F32)
    cbias = conv_b.astype(F32).reshape(1, -1)
    cwx, cwb, cwc = cw[:, :D_SSD], cw[:, D_SSD:D_SSD + N_GROUPS * N], cw[:, D_SSD + N_GROUPS * N:]
    cbx, cbb, cbc = cbias[:, :D_SSD], cbias[:, D_SSD:D_SSD + N_GROUPS * N], cbias[:, D_SSD + N_GROUPS * N:]

    in_specs = [
        pl.BlockSpec((L, W), lambda b, g, c: (rows(b, g, c), zb + g)),
        pl.BlockSpec((L, W), lambda b, g, c: (rows(b, g, c), xb + g)),
        pl.BlockSpec((L, N), lambda b, g, c: (rows(b, g, c), bb + g)),
        pl.BlockSpec((L, N), lambda b, g, c: (rows(b, g, c), cb_ + g)),
        pl.BlockSpec((L, 128), lambda b, g, c: (rows(b, g, c), 0)),
        pl.BlockSpec((4, W), lambda b, g, c: (0, g)),
        pl.BlockSpec((4, N), lambda b, g, c: (0, g)),
        pl.BlockSpec((4, N), lambda b, g, c: (0, g)),
        pl.BlockSpec((1, W), lambda b, g, c: (0, g)),
        pl.BlockSpec((1, N), lambda b, g, c: (0, g)),
        pl.BlockSpec((1, N), lambda b, g, c: (0, g)),
        pl.BlockSpec((1, 128), lambda b, g, c: (0, 0)),
        pl.BlockSpec((1, 128), lambda b, g, c: (0, 0)),
        pl.BlockSpec((1, W), lambda b, g, c: (0, g)),
        pl.BlockSpec((1, W), lambda b, g, c: (0, g)),
    ]
    return pl.pallas_call(
        _ssd_kernel,
        grid=(batch, N_GROUPS, nchunk),
        in_specs=in_specs,
        out_specs=pl.BlockSpec((L, W), lambda b, g, c: (rows(b, g, c), g)),
        out_shape=jax.ShapeDtypeStruct((t, D_SSD), BF16),
        scratch_shapes=[pltpu.VMEM((N, W), F32), pltpu.VMEM((L + CARRY_ROWS, W + 2 * N), F32)],
        compiler_params=_cparams(("parallel", "parallel", "arbitrary")),
        name="ssd_mixer",
    )(proj, proj, proj, proj, dt_raw, cwx, cwb, cwc, cbx, cbb, cbc, dtb, alog, dexp, norm_w.astype(F32).reshape(1, -1))


def _shortconv_kernel(b_ref, c_ref, h_ref, w_ref, nw_ref, o_ref, ext_ref, y_ref, *, rows, col_chunk):
    s = pl.program_id(1)
    d = D_CONV

    @pl.when(s == 0)
    def _():
        ext_ref[0:CARRY_ROWS, :] = jnp.zeros((CARRY_ROWS, d), F32)

    ss = jnp.zeros((rows, 1), F32)
    for j in range(0, d, col_chunk):
        sl = slice(j, j + col_chunk)
        ext_ref[CARRY_ROWS:CARRY_ROWS + rows, sl] = c_ref[:, sl].astype(F32) * h_ref[:, sl].astype(F32)
        acc = ext_ref[CARRY_ROWS:CARRY_ROWS + rows, sl] * w_ref[2:3, sl]
        for k in range(2):
            acc = acc + ext_ref[CARRY_ROWS - 2 + k:CARRY_ROWS - 2 + k + rows, sl] * w_ref[k:k + 1, sl]
        y = b_ref[:, sl].astype(F32) * acc
        y_ref[:, sl] = y
        ss = ss + jnp.sum(y * y, axis=-1, keepdims=True)
    ext_ref[0:CARRY_ROWS, :] = ext_ref[rows:rows + CARRY_ROWS, :]
    inv = lax.rsqrt(ss * (1.0 / d) + RMS_EPS)
    for j in range(0, d, col_chunk):
        sl = slice(j, j + col_chunk)
        o_ref[:, sl] = (y_ref[:, sl] * inv * nw_ref[:, sl]).astype(o_ref.dtype)


def shortconv_mixer(proj, conv_w, norm_w, *, batch, seq, col_off, rows=256, col_chunk=512):
    t = batch * seq
    nblk = seq // rows
    d = D_CONV
    bb, cb_, hb = col_off["scb"] // d, col_off["scc"] // d, col_off["sch"] // d
    return pl.pallas_call(
        functools.partial(_shortconv_kernel, rows=rows, col_chunk=col_chunk),
        grid=(batch, nblk),
        in_specs=[
            pl.BlockSpec((rows, d), lambda b, s: (b * nblk + s, bb)),
            pl.BlockSpec((rows, d), lambda b, s: (b * nblk + s, cb_)),
            pl.BlockSpec((rows, d), lambda b, s: (b * nblk + s, hb)),
            pl.BlockSpec((3, d), lambda b, s: (0, 0)),
            pl.BlockSpec((1, d), lambda b, s: (0, 0)),
        ],
        out_specs=pl.BlockSpec((rows, d), lambda b, s: (b * nblk + s, 0)),
        out_shape=jax.ShapeDtypeStruct((t, d), BF16),
        scratch_shapes=[pltpu.VMEM((rows + CARRY_ROWS, d), F32), pltpu.VMEM((rows, d), F32)],
        compiler_params=_cparams(("parallel", "arbitrary")),
        name="shortconv_mixer",
    )(proj, proj, proj, conv_w.astype(F32), norm_w.astype(F32).reshape(1, d))


def _outproj_kernel(a1_ref, a2_ref, b_ref, r_ref, o_ref, acc_ref, *, nk):
    k = pl.program_id(2)
    half = nk // 2

    @pl.when(k == 0)
    def _():
        acc_ref[...] = jnp.zeros_like(acc_ref)

    @pl.when(k < half)
    def _():
        acc_ref[...] += jnp.dot(a1_ref[...], b_ref[...], preferred_element_type=F32)

    @pl.when(k >= half)
    def _():
        acc_ref[...] += jnp.dot(a2_ref[...], b_ref[...], preferred_element_type=F32)

    @pl.when(k == nk - 1)
    def _():
        o_ref[...] = acc_ref[...] + r_ref[...]


def out_projection(y_ssd, y_sc, w_out, x, *, tm=1024, tn=1024, tk=2048):
    m, _ = y_ssd.shape
    kdim, n = w_out.shape
    tm = min(tm, m)
    nk = kdim // tk
    half = nk // 2
    return pl.pallas_call(
        functools.partial(_outproj_kernel, nk=nk),
        grid=(m // tm, n // tn, nk),
        in_specs=[
            pl.BlockSpec((tm, tk), lambda i, j, k: (i, jnp.minimum(k, half - 1))),
            pl.BlockSpec((tm, tk), lambda i, j, k: (i, jnp.maximum(k - half, 0))),
            pl.BlockSpec((tk, tn), lambda i, j, k: (k, j)),
            pl.BlockSpec((tm, tn), lambda i, j, k: (i, j)),
        ],
        out_specs=pl.BlockSpec((tm, tn), lambda i, j, k: (i, j)),
        out_shape=jax.ShapeDtypeStruct((m, n), F32),
        scratch_shapes=[pltpu.VMEM((tm, tn), F32)],
        compiler_params=_cparams(("parallel", "parallel", "arbitrary")),
        name="out_projection",
    )(y_ssd, y_sc, w_out, x)


def _top_values(x, n):
    outs = []
    for r in range(n):
        m = jnp.max(x, axis=0, keepdims=True)
        outs.append(m)
        if r < n - 1:
            x = jnp.where(x >= m, -jnp.inf, x)
    return outs


def _route_kernel(qt_ref, keys_ref, phi_ref, e0_ref, e1_ref):
    def head(h, carry):
        q0 = qt_ref[pl.ds(pl.multiple_of(h * 2 * N_KEYS, N_KEYS), N_KEYS), :]
        q1 = qt_ref[pl.ds(pl.multiple_of(h * 2 * N_KEYS + N_KEYS, N_KEYS), N_KEYS), :]
        s0 = jnp.dot(keys_ref[2 * h], q0, preferred_element_type=F32)
        s1 = jnp.dot(keys_ref[2 * h + 1], q1, preferred_element_type=F32)
        top0 = _top_values(s0, TOPK + 1)
        top1 = _top_values(s1, TOPK + 1)
        t1 = jnp.concatenate(top1[:TOPK], axis=0)
        extra = jnp.concatenate([top0[TOPK] + top1[0], top0[0] + top1[TOPK],
                                 jnp.full((6, s0.shape[1]), -jnp.inf, F32)], axis=0)
        cand = jnp.concatenate([a + t1 for a in top0[:TOPK]] + [extra], axis=0)
        best = _top_values(cand, TOPK + 1)
        m = best[0]
        z = jnp.exp(best[0] - m)
        for r in range(1, TOPK):
            z = z + jnp.exp(best[r] - m)
        inv_z = 1.0 / z
        tau = 0.5 * (best[TOPK - 1] + best[TOPK])
        m0, m1 = top0[0], top1[0]
        e0_ref[h] = jnp.exp(s0 - m0)
        e1_ref[h] = jnp.exp(s1 - m1) * inv_z
        phi_ref[h] = jnp.exp((tau - m1) - s0) * inv_z
        return carry

    lax.fori_loop(0, PEER_HEADS, head, 0)


def peer_routing(qt, keys, *, tb=128):
    qd, t = qt.shape
    shp = jax.ShapeDtypeStruct((PEER_HEADS, N_KEYS, t), F32)
    ospec = pl.BlockSpec((PEER_HEADS, N_KEYS, tb), lambda i: (0, 0, i))
    return pl.pallas_call(
        _route_kernel,
        grid=(t // tb,),
        in_specs=[pl.BlockSpec((qd, tb), lambda i: (0, i)),
                  pl.BlockSpec((2 * PEER_HEADS, N_KEYS, N_KEYS), lambda i: (0, 0, 0))],
        out_specs=[ospec, ospec, ospec],
        out_shape=[shp, shp, shp],
        compiler_params=_cparams(("parallel",)),
        name="peer_routing",
    )(qt, keys)


def _gelu_exact(x):
    return 0.5 * x * (1.0 + lax.erf(x * (1.0 / math.sqrt(2.0))))


def _peer_dense_kernel(xt_ref, u_ref, vt_ref, phi_ref, e0_ref, e1_ref, o_ref, a_ref, *, te):
    et = pl.program_id(1)

    @pl.when(et == 0)
    def _():
        o_ref[...] = jnp.zeros_like(o_ref)

    ht = jnp.dot(u_ref[...], xt_ref[...], preferred_element_type=F32)
    for ii in range(te // N_KEYS):
        i = et * (te // N_KEYS) + ii
        gate = None
        for h in range(PEER_HEADS):
            e1 = e1_ref[h]
            sel = jnp.where(e1 >= phi_ref[h, pl.ds(i, 1), :], e1, 0.0) * e0_ref[h, pl.ds(i, 1), :]
            gate = sel if gate is None else gate + sel
        act = _gelu_exact(ht[ii * N_KEYS:(ii + 1) * N_KEYS, :]) * gate
        a_ref[ii * N_KEYS:(ii + 1) * N_KEYS, :] = act.astype(a_ref.dtype)
    o_ref[...] += jnp.dot(vt_ref[...], a_ref[...], preferred_element_type=F32)


def peer_dense(xt, u, vt, phi, e0, e1, *, tb=512, te=512):
    d, t = xt.shape
    e = u.shape[0]
    tb = min(tb, t)
    rspec = pl.BlockSpec((PEER_HEADS, N_KEYS, tb), lambda i, j: (0, 0, i))
    return pl.pallas_call(
        functools.partial(_peer_dense_kernel, te=te),
        grid=(t // tb, e // te),
        in_specs=[
            pl.BlockSpec((d, tb), lambda i, j: (0, i)),
            pl.BlockSpec((te, d), lambda i, j: (j, 0)),
            pl.BlockSpec((d, te), lambda i, j: (0, j)),
            rspec, rspec, rspec,
        ],
        out_specs=pl.BlockSpec((d, tb), lambda i, j: (0, i)),
        out_shape=jax.ShapeDtypeStruct((d, t), F32),
        scratch_shapes=[pltpu.VMEM((te, tb), BF16)],
        compiler_params=_cparams(("parallel", "arbitrary")),
        name="peer_dense",
    )(xt, u, vt, phi, e0, e1)


def _final_kernel(h_ref, pt_ref, w_ref, o_ref):
    x = h_ref[...] + pt_ref[...].T
    ms = jnp.mean(x * x, axis=-1, keepdims=True)
    o_ref[...] = x * lax.rsqrt(ms + RMS_EPS) * w_ref[...]


def final_norm(h1, peer_t, w, *, tm=256):
    t, d = h1.shape
    return pl.pallas_call(
        _final_kernel,
        grid=(t // tm,),
        in_specs=[pl.BlockSpec((tm, d), lambda i: (i, 0)), pl.BlockSpec((d, tm), lambda i: (0, i)),
                  pl.BlockSpec((1, d), lambda i: (0, 0))],
        out_specs=pl.BlockSpec((tm, d), lambda i: (i, 0)),
        out_shape=jax.ShapeDtypeStruct((t, d), F32),
        compiler_params=_cparams(("parallel",)),
        name="final_norm",
    )(h1, peer_t, w.reshape(1, d))


def kernel(x, w_in, ssd_conv_w, ssd_conv_b, ssd_dt_bias, ssd_a_log, ssd_d, ssd_norm_w, sc_conv_w, sc_norm_w, w_out,
           norm_mix_w, norm_ffn_w, peer_w_query, peer_sub_keys, peer_u, peer_v, norm_final_w):
    batch, seq, d = x.shape
    t = batch * seq
    h = x.reshape(t, d)
    assert w_in.shape[0] == 1, "single-layer block"
    o1 = D_SSD
    o_b = o1 + D_SSD
    o_c = o_b + N_GROUPS * D_STATE
    o2 = o_c + N_GROUPS * D_STATE
    o3 = o2 + N_HEADS
    col_off = {"z": 0, "xs": D_SSD, "scb": 2 * D_SSD, "scc": 2 * D_SSD + D_CONV, "sch": 2 * D_SSD + 2 * D_CONV,
               "B": 2 * D_SSD + 3 * D_CONV, "C": 2 * D_SSD + 3 * D_CONV + N_GROUPS * D_STATE}
    wl = w_in[0]
    w_main = jnp.concatenate([wl[:, :o_b], wl[:, o3:], wl[:, o_b:o2]], axis=1).astype(BF16)
    w_dt = jnp.pad(wl[:, o2:o3], ((0, 0), (0, 128 - N_HEADS))).astype(BF16)
    w_out_b = w_out[0].astype(BF16)
    wq_t = peer_w_query[0].T.astype(BF16)
    keys = peer_sub_keys[0].reshape(2 * PEER_HEADS, N_KEYS, N_KEYS).astype(BF16)
    u_b = peer_u[0].astype(BF16)
    vt_b = peer_v[0].T.astype(BF16)

    xn = rmsnorm_rows(h, norm_mix_w[0])
    proj = matmul(xn, w_main, tm=1024, tn=1024, tk=d, out_dtype=BF16, name="in_projection")
    dt_raw = matmul(xn, w_dt, tm=1024, tn=128, tk=d, out_dtype=F32, name="dt_projection")
    y_ssd = ssd_mixer(proj, dt_raw, ssd_conv_w[0], ssd_conv_b[0], ssd_dt_bias[0], ssd_a_log[0], ssd_d[0],
                      ssd_norm_w[0], batch=batch, seq=seq, col_off=col_off)
    y_sc = shortconv_mixer(proj, sc_conv_w[0], sc_norm_w[0], batch=batch, seq=seq, col_off=col_off)
    h = out_projection(y_ssd, y_sc, w_out_b, h)

    xn_t = rmsnorm_rows(h, norm_ffn_w[0], transpose_out=True)
    q_t = matmul(wq_t, xn_t, tm=1024, tn=1024, tk=d, out_dtype=BF16, name="query_projection")
    phi, e0, e1 = peer_routing(q_t, keys)
    peer_t = peer_dense(xn_t, u_b, vt_b, phi, e0, e1)
    out = final_norm(h, peer_t, norm_final_w)
    return out.reshape(batch, seq, d)
```

```python
import functools
import math

import jax
import jax.numpy as jnp
from jax import lax
from jax.experimental import pallas as pl
from jax.experimental.pallas import tpu as pltpu

F32 = jnp.float32
BF16 = jnp.bfloat16

D_MODEL = 4096
D_SSD = 4096
D_CONV = 4096
HEAD_DIM = 64
N_HEADS = 64
N_GROUPS = 8
HEADS_PER_GROUP = 8
D_STATE = 128
CHUNK = 128
GROUP_W = HEADS_PER_GROUP * HEAD_DIM
PEER_HEADS = 8
N_KEYS = 128
N_EXPERTS = N_KEYS * N_KEYS
TOPK = 16
RMS_EPS = 1e-6

VMEM_LIMIT_BYTES = 58 * 1024 * 1024
CARRY_ROWS = 8


def _cparams(sem):
    return pltpu.CompilerParams(dimension_semantics=sem, vmem_limit_bytes=VMEM_LIMIT_BYTES)


def _rmsnorm_kernel(x_ref, w_ref, o_ref):
    x = x_ref[...]
    ms = jnp.mean(x * x, axis=-1, keepdims=True)
    o_ref[...] = (x * lax.rsqrt(ms + RMS_EPS) * w_ref[...]).astype(o_ref.dtype)


def rmsnorm_rows(x, w, *, tm=256):
    t, d = x.shape
    return pl.pallas_call(
        _rmsnorm_kernel,
        grid=(t // tm,),
        in_specs=[pl.BlockSpec((tm, d), lambda i: (i, 0)), pl.BlockSpec((1, d), lambda i: (0, 0))],
        out_specs=pl.BlockSpec((tm, d), lambda i: (i, 0)),
        out_shape=jax.ShapeDtypeStruct((t, d), BF16),
        compiler_params=_cparams(("parallel",)),
        name="rmsnorm_rows",
    )(x, w.reshape(1, d))


def _matmul_kernel(*refs, nk, has_res):
    if has_res:
        a_ref, b_ref, r_ref, o_ref = refs[:4]
    else:
        a_ref, b_ref, o_ref = refs[:3]
        r_ref = None
    if nk == 1:
        acc = jnp.dot(a_ref[...], b_ref[...], preferred_element_type=F32)
        if has_res:
            acc = acc + r_ref[...]
        o_ref[...] = acc.astype(o_ref.dtype)
        return
    acc_ref = refs[-1]
    k = pl.program_id(2)

    @pl.when(k == 0)
    def _():
        acc_ref[...] = jnp.zeros_like(acc_ref)

    acc_ref[...] += jnp.dot(a_ref[...], b_ref[...], preferred_element_type=F32)

    @pl.when(k == nk - 1)
    def _():
        acc = acc_ref[...]
        if has_res:
            acc = acc + r_ref[...]
        o_ref[...] = acc.astype(o_ref.dtype)


def matmul(a, b, *, tm, tn, tk, out_dtype, residual=None, name="matmul"):
    m, kdim = a.shape
    _, n = b.shape
    tm, tn, tk = min(tm, m), min(tn, n), min(tk, kdim)
    nk = kdim // tk
    has_res = residual is not None
    in_specs = [pl.BlockSpec((tm, tk), lambda i, j, k: (i, k)), pl.BlockSpec((tk, tn), lambda i, j, k: (k, j))]
    args = [a, b]
    if has_res:
        in_specs.append(pl.BlockSpec((tm, tn), lambda i, j, k: (i, j)))
        args.append(residual)
    scratch = [pltpu.VMEM((tm, tn), F32)] if nk > 1 else []
    return pl.pallas_call(
        functools.partial(_matmul_kernel, nk=nk, has_res=has_res),
        grid=(m // tm, n // tn, nk),
        in_specs=in_specs,
        out_specs=pl.BlockSpec((tm, tn), lambda i, j, k: (i, j)),
        out_shape=jax.ShapeDtypeStruct((m, n), out_dtype),
        scratch_shapes=scratch,
        compiler_params=_cparams(("parallel", "parallel", "arbitrary")),
        name=name,
    )(*args)


def _silu(x):
    hx = 0.5 * x
    return hx + hx * jnp.tanh(hx)


def _softplus(x):
    return jnp.maximum(x, 0.0) + jnp.log(1.0 + jnp.exp(-jnp.abs(x)))


_P_CONV, _P_BIAS, _P_DSKIP, _P_NORM = 0, 4, 5, 6


def _ssd_kernel(z_ref, xs_ref, b_ref, c_ref, dt_ref, pg_ref, pdt_ref, o_ref, state_ref, ext_ref, *, chunks):
    g = pl.program_id(1)
    c = pl.program_id(2)
    L = CHUNK
    W = GROUP_W
    N = D_STATE
    XBC = W + 2 * N

    @pl.when(c == 0)
    def _():
        state_ref[...] = jnp.zeros_like(state_ref)
        ext_ref[0:CARRY_ROWS, :] = jnp.zeros((CARRY_ROWS, XBC), F32)

    row = lax.broadcasted_iota(jnp.int32, (L, L), 0)
    col = lax.broadcasted_iota(jnp.int32, (L, L), 1)
    causal = row >= col
    tril = causal.astype(BF16)
    er = lax.broadcasted_iota(jnp.int32, (128, W), 0)
    ec = lax.broadcasted_iota(jnp.int32, (128, W), 1)
    expand = (er == ec // HEAD_DIM).astype(BF16)

    def chunk(ci, carry):
        rs = pl.ds(pl.multiple_of(ci * L, L), L)
        ext_ref[CARRY_ROWS:CARRY_ROWS + L, 0:W] = xs_ref[rs, :].astype(F32)
        ext_ref[CARRY_ROWS:CARRY_ROWS + L, W:W + N] = b_ref[rs, :].astype(F32)
        ext_ref[CARRY_ROWS:CARRY_ROWS + L, W + N:XBC] = c_ref[rs, :].astype(F32)
        acc = ext_ref[CARRY_ROWS:CARRY_ROWS + L, :] * pg_ref[_P_CONV + 3:_P_CONV + 4, :]
        for k in range(3):
            acc = acc + ext_ref[CARRY_ROWS - 3 + k:CARRY_ROWS - 3 + k + L, :] * pg_ref[_P_CONV + k:_P_CONV + k + 1, :]
        xbc = _silu(acc + pg_ref[_P_BIAS:_P_BIAS + 1, :])
        ext_ref[0:CARRY_ROWS, :] = ext_ref[L:L + CARRY_ROWS, :]
        xs = xbc[:, 0:W]
        b_b = xbc[:, W:W + N].astype(BF16)
        c_b = xbc[:, W + N:XBC].astype(BF16)

        dt_all = _softplus(dt_ref[rs, :] + pdt_ref[0:1, :])
        da_all = dt_all * (-jnp.exp(pdt_ref[1:2, :]))
        shift = (128 - g * HEADS_PER_GROUP) % 128
        dt_g = pltpu.roll(dt_all, shift, 1)
        da_g = pltpu.roll(da_all, shift, 1)

        hi = da_g.astype(BF16)
        r1 = da_g - hi.astype(F32)
        mid = r1.astype(BF16)
        lo = (r1 - mid.astype(F32)).astype(BF16)
        acum = (jnp.dot(tril, hi, preferred_element_type=F32)
                + jnp.dot(tril, mid, preferred_element_type=F32)
                + jnp.dot(tril, lo, preferred_element_type=F32))
        acum_t = acum.T
        a_last = acum[L - 1:L, :]

        def expand_heads(v):
            vh = v.astype(BF16)
            vl = (v - vh.astype(F32)).astype(BF16)
            return jnp.dot(vh, expand, preferred_element_type=F32) + jnp.dot(vl, expand, preferred_element_type=F32)

        dt_x = expand_heads(dt_g)
        ea_x = expand_heads(jnp.exp(acum))
        to_end_x = expand_heads(jnp.exp(a_last - acum))

        xdt = xs * dt_x
        xdt_b = xdt.astype(BF16)

        cb = lax.dot_general(c_b, b_b, (((1,), (1,)), ((), ())), preferred_element_type=F32)
        state = state_ref[...]
        y_off = jnp.dot(c_b, state.astype(BF16), preferred_element_type=F32) * ea_x

        y_parts = []
        for r in range(HEADS_PER_GROUP):
            seg = acum[:, r:r + 1] - acum_t[r:r + 1, :]
            decay = jnp.exp(jnp.where(causal, seg, -jnp.inf))
            m = (cb * decay).astype(BF16)
            y_parts.append(jnp.dot(m, xdt_b[:, r * HEAD_DIM:(r + 1) * HEAD_DIM], preferred_element_type=F32))
        y = jnp.concatenate(y_parts, axis=1) + y_off + pg_ref[_P_DSKIP:_P_DSKIP + 1, 0:W] * xs

        upd = lax.dot_general(b_b, (xdt * to_end_x).astype(BF16), (((0,), (0,)), ((), ())),
                              preferred_element_type=F32)
        state_ref[...] = state * ea_x[L - 1:L, :] + upd

        y = y * _silu(z_ref[rs, :].astype(F32))
        ms = jnp.mean(y * y, axis=-1, keepdims=True)
        o_ref[rs, :] = (y * lax.rsqrt(ms + RMS_EPS) * pg_ref[_P_NORM:_P_NORM + 1, 0:W]).astype(o_ref.dtype)
        return carry

    lax.fori_loop(0, chunks, chunk, 0)


def ssd_mixer(proj, dt_raw, conv_w, conv_b, dt_bias, a_log, d_skip, norm_w, *, batch, seq, col_off, chunks=4):
    t = batch * seq
    chunks = min(chunks, seq // CHUNK)
    nchunk = seq // (CHUNK * chunks)
    W, N, L, G = GROUP_W, D_STATE, CHUNK * chunks, N_GROUPS
    zb, xb, bb, cb_ = col_off["z"] // W, col_off["xs"] // W, col_off["B"] // N, col_off["C"] // N

    def rows(b, g, c):
        return b * nchunk + c

    def per_group(v):
        r = v.shape[0]
        x_part = v[:, :D_SSD].reshape(r, G, W)
        b_part = v[:, D_SSD:D_SSD + G * N].reshape(r, G, N)
        c_part = v[:, D_SSD + G * N:].reshape(r, G, N)
        return jnp.concatenate([x_part, b_part, c_part], axis=2).transpose(1, 0, 2)

    def x_only(v):
        return jnp.pad(v.astype(F32).reshape(G, 1, W), ((0, 0), (0, 0), (0, 2 * N)))

    pg = jnp.concatenate([per_group(conv_w.astype(F32)), per_group(conv_b.astype(F32).reshape(1, -1)),
                          x_only(jnp.repeat(d_skip, HEAD_DIM)), x_only(norm_w),
                          jnp.zeros((G, 1, W + 2 * N), F32)], axis=1)
    pad = 128 - N_HEADS
    pdt = jnp.pad(jnp.stack([dt_bias.astype(F32), a_log.astype(F32)]), ((0, 6), (0, pad)))

    in_specs = [
        pl.BlockSpec((L, W), lambda b, g, c: (rows(b, g, c), zb + g)),
        pl.BlockSpec((L, W), lambda b, g, c: (rows(b, g, c), xb + g)),
        pl.BlockSpec((L, N), lambda b, g, c: (rows(b, g, c), bb + g)),
        pl.BlockSpec((L, N), lambda b, g, c: (rows(b, g, c), cb_ + g)),
        pl.BlockSpec((L, 128), lambda b, g, c: (rows(b, g, c), 0)),
        pl.BlockSpec((None, 8, W + 2 * N), lambda b, g, c: (g, 0, 0)),
        pl.BlockSpec((8, 128), lambda b, g, c: (0, 0)),
    ]
    return pl.pallas_call(
        functools.partial(_ssd_kernel, chunks=chunks),
        grid=(batch, N_GROUPS, nchunk),
        in_specs=in_specs,
        out_specs=pl.BlockSpec((L, W), lambda b, g, c: (rows(b, g, c), g)),
        out_shape=jax.ShapeDtypeStruct((t, D_SSD), BF16),
        scratch_shapes=[pltpu.VMEM((N, W), F32), pltpu.VMEM((CHUNK + CARRY_ROWS, W + 2 * N), F32)],
        compiler_params=_cparams(("parallel", "parallel", "arbitrary")),
        name="ssd_mixer",
    )(proj, proj, proj, proj, dt_raw, pg, pdt)


def _shortconv_kernel(b_ref, c_ref, h_ref, w_ref, nw_ref, o_ref, ext_ref, y_ref, *, rows, col_chunk):
    s = pl.program_id(1)
    d = D_CONV

    @pl.when(s == 0)
    def _():
        ext_ref[0:CARRY_ROWS, :] = jnp.zeros((CARRY_ROWS, d), F32)

    ss = jnp.zeros((rows, 1), F32)
    for j in range(0, d, col_chunk):
        sl = slice(j, j + col_chunk)
        ext_ref[CARRY_ROWS:CARRY_ROWS + rows, sl] = c_ref[:, sl].astype(F32) * h_ref[:, sl].astype(F32)
        acc = ext_ref[CARRY_ROWS:CARRY_ROWS + rows, sl] * w_ref[2:3, sl]
        for k in range(2):
            acc = acc + ext_ref[CARRY_ROWS - 2 + k:CARRY_ROWS - 2 + k + rows, sl] * w_ref[k:k + 1, sl]
        y = b_ref[:, sl].astype(F32) * acc
        y_ref[:, sl] = y
        ss = ss + jnp.sum(y * y, axis=-1, keepdims=True)
    ext_ref[0:CARRY_ROWS, :] = ext_ref[rows:rows + CARRY_ROWS, :]
    inv = lax.rsqrt(ss * (1.0 / d) + RMS_EPS)
    for j in range(0, d, col_chunk):
        sl = slice(j, j + col_chunk)
        o_ref[:, sl] = (y_ref[:, sl] * inv * nw_ref[:, sl]).astype(o_ref.dtype)


def shortconv_mixer(proj, conv_w, norm_w, *, batch, seq, col_off, rows=256, col_chunk=512):
    t = batch * seq
    nblk = seq // rows
    d = D_CONV
    bb, cb_, hb = col_off["scb"] // d, col_off["scc"] // d, col_off["sch"] // d
    return pl.pallas_call(
        functools.partial(_shortconv_kernel, rows=rows, col_chunk=col_chunk),
        grid=(batch, nblk),
        in_specs=[
            pl.BlockSpec((rows, d), lambda b, s: (b * nblk + s, bb)),
            pl.BlockSpec((rows, d), lambda b, s: (b * nblk + s, cb_)),
            pl.BlockSpec((rows, d), lambda b, s: (b * nblk + s, hb)),
            pl.BlockSpec((3, d), lambda b, s: (0, 0)),
            pl.BlockSpec((1, d), lambda b, s: (0, 0)),
        ],
        out_specs=pl.BlockSpec((rows, d), lambda b, s: (b * nblk + s, 0)),
        out_shape=jax.ShapeDtypeStruct((t, d), BF16),
        scratch_shapes=[pltpu.VMEM((rows + CARRY_ROWS, d), F32), pltpu.VMEM((rows, d), F32)],
        compiler_params=_cparams(("parallel", "arbitrary")),
        name="shortconv_mixer",
    )(proj, proj, proj, conv_w.astype(F32), norm_w.astype(F32).reshape(1, d))


def _outproj_kernel(a1_ref, a2_ref, b_ref, r_ref, o_ref, acc_ref, *, nk):
    k = pl.program_id(2)
    half = nk // 2

    @pl.when(k == 0)
    def _():
        acc_ref[...] = jnp.zeros_like(acc_ref)

    @pl.when(k < half)
    def _():
        acc_ref[...] += jnp.dot(a1_ref[...], b_ref[...], preferred_element_type=F32)

    @pl.when(k >= half)
    def _():
        acc_ref[...] += jnp.dot(a2_ref[...], b_ref[...], preferred_element_type=F32)

    @pl.when(k == nk - 1)
    def _():
        o_ref[...] = acc_ref[...] + r_ref[...]


def out_projection(y_ssd, y_sc, w_out, x, *, tm=1024, tn=1024, tk=2048):
    m, _ = y_ssd.shape
    kdim, n = w_out.shape
    tm = min(tm, m)
    nk = kdim // tk
    half = nk // 2
    return pl.pallas_call(
        functools.partial(_outproj_kernel, nk=nk),
        grid=(m // tm, n // tn, nk),
        in_specs=[
            pl.BlockSpec((tm, tk), lambda i, j, k: (i, jnp.minimum(k, half - 1))),
            pl.BlockSpec((tm, tk), lambda i, j, k: (i, jnp.maximum(k - half, 0))),
            pl.BlockSpec((tk, tn), lambda i, j, k: (k, j)),
            pl.BlockSpec((tm, tn), lambda i, j, k: (i, j)),
        ],
        out_specs=pl.BlockSpec((tm, tn), lambda i, j, k: (i, j)),
        out_shape=jax.ShapeDtypeStruct((m, n), F32),
        scratch_shapes=[pltpu.VMEM((tm, tn), F32)],
        compiler_params=_cparams(("parallel", "parallel", "arbitrary")),
        name="out_projection",
    )(y_ssd, y_sc, w_out, x)


def _top_values(x, n):
    outs = []
    for r in range(n):
        m = jnp.max(x, axis=0, keepdims=True)
        outs.append(m)
        if r < n - 1:
            x = jnp.where(x >= m, -jnp.inf, x)
    return outs


def _query_kernel(x_ref, w_ref, o_ref):
    acc = jnp.dot(x_ref[...], w_ref[...], preferred_element_type=F32)
    for c in range(o_ref.shape[0]):
        o_ref[c] = acc[:, c * N_KEYS:(c + 1) * N_KEYS].astype(o_ref.dtype)


def query_projection(xn, wq, *, tm=1024, tn=1024):
    t, d = xn.shape
    n = wq.shape[1]
    tm = min(tm, t)
    per = tn // N_KEYS
    return pl.pallas_call(
        _query_kernel,
        grid=(t // tm, n // tn),
        in_specs=[pl.BlockSpec((tm, d), lambda i, j: (i, 0)), pl.BlockSpec((d, tn), lambda i, j: (0, j))],
        out_specs=pl.BlockSpec((per, tm, N_KEYS), lambda i, j: (j, i, 0)),
        out_shape=jax.ShapeDtypeStruct((n // N_KEYS, t, N_KEYS), BF16),
        compiler_params=_cparams(("parallel", "parallel")),
        name="query_projection",
    )(xn, wq)


def _route_kernel(q_ref, keys_ref, phi_ref, e0_ref, e1_ref):
    nt = (((1,), (1,)), ((), ()))

    def head(h, carry):
        s0 = lax.dot_general(keys_ref[2 * h], q_ref[2 * h], nt, preferred_element_type=F32)
        s1 = lax.dot_general(keys_ref[2 * h + 1], q_ref[2 * h + 1], nt, preferred_element_type=F32)
        top0 = _top_values(s0, TOPK + 1)
        top1 = _top_values(s1, TOPK + 1)
        t1 = jnp.concatenate(top1, axis=0)
        pieces = [top0[r] + t1[0:(TOPK + 1) // (r + 1)] for r in range(TOPK + 1)]
        n_cand = sum(p.shape[0] for p in pieces)
        pieces.append(jnp.full((-n_cand % 8, s0.shape[1]), -jnp.inf, F32))
        cand = jnp.concatenate(pieces, axis=0)
        best = _top_values(cand, TOPK + 1)
        m = best[0]
        z = jnp.exp(best[0] - m)
        for r in range(1, TOPK):
            z = z + jnp.exp(best[r] - m)
        inv_z = 1.0 / z
        tau = 0.5 * (best[TOPK - 1] + best[TOPK])
        m0, m1 = top0[0], top1[0]
        e0_ref[h] = jnp.exp(s0 - m0)
        e1_ref[h] = jnp.exp(s1 - m1) * inv_z
        phi_ref[h] = jnp.exp((tau - m1) - s0) * inv_z
        return carry

    lax.fori_loop(0, PEER_HEADS, head, 0)


def peer_routing(q, keys, *, tb=256):
    _, t, _ = q.shape
    shp = jax.ShapeDtypeStruct((PEER_HEADS, N_KEYS, t), F32)
    ospec = pl.BlockSpec((PEER_HEADS, N_KEYS, tb), lambda i: (0, 0, i))
    return pl.pallas_call(
        _route_kernel,
        grid=(t // tb,),
        in_specs=[pl.BlockSpec((2 * PEER_HEADS, tb, N_KEYS), lambda i: (0, i, 0)),
                  pl.BlockSpec((2 * PEER_HEADS, N_KEYS, N_KEYS), lambda i: (0, 0, 0))],
        out_specs=[ospec, ospec, ospec],
        out_shape=[shp, shp, shp],
        compiler_params=_cparams(("parallel",)),
        name="peer_routing",
    )(q, keys)


def _gelu_exact(x):
    return 0.5 * x * (1.0 + lax.erf(x * (1.0 / math.sqrt(2.0))))


def _peer_dense_kernel(x_ref, u_ref, v_ref, phi_ref, e0_ref, e1_ref, o_ref, a_ref, *, te):
    et = pl.program_id(1)
    nblk = te // N_KEYS
    tb = x_ref.shape[0]
    tq = 128
    nt = (((1,), (1,)), ((), ()))

    @pl.when(et == 0)
    def _():
        o_ref[...] = jnp.zeros_like(o_ref)

    ht = lax.dot_general(u_ref[...], x_ref[...], nt, preferred_element_type=F32)
    for j in range(nblk):
        i = et * nblk + j
        rows = slice(j * N_KEYS, (j + 1) * N_KEYS)
        phi_rows = [phi_ref[h, pl.ds(i, 1), :] for h in range(PEER_HEADS)]
        e0_rows = [e0_ref[h, pl.ds(i, 1), :] for h in range(PEER_HEADS)]
        for q in range(tb // tq):
            ts = slice(q * tq, (q + 1) * tq)
            gate = None
            for h in range(PEER_HEADS):
                e1 = e1_ref[h, :, ts]
                sel = jnp.where(e1 >= phi_rows[h][:, ts], e1, 0.0) * e0_rows[h][:, ts]
                gate = sel if gate is None else gate + sel
            act = _gelu_exact(ht[rows, ts]) * gate
            a_ref[ts, rows] = act.T.astype(a_ref.dtype)
    o_ref[...] += jnp.dot(a_ref[...], v_ref[...], preferred_element_type=F32)


def peer_dense(xn, u, v, phi, e0, e1, *, tb=512, te=512):
    t, d = xn.shape
    e = u.shape[0]
    tb = min(tb, t)
    rspec = pl.BlockSpec((PEER_HEADS, N_KEYS, tb), lambda i, j: (0, 0, i))
    return pl.pallas_call(
        functools.partial(_peer_dense_kernel, te=te),
        grid=(t // tb, e // te),
        in_specs=[
            pl.BlockSpec((tb, d), lambda i, j: (i, 0)),
            pl.BlockSpec((te, d), lambda i, j: (j, 0)),
            pl.BlockSpec((te, d), lambda i, j: (j, 0)),
            rspec, rspec, rspec,
        ],
        out_specs=pl.BlockSpec((tb, d), lambda i, j: (i, 0)),
        out_shape=jax.ShapeDtypeStruct((t, d), F32),
        scratch_shapes=[pltpu.VMEM((tb, te), BF16)],
        compiler_params=_cparams(("parallel", "arbitrary")),
        name="peer_dense",
    )(xn, u, v, phi, e0, e1)


def _final_kernel(h_ref, p_ref, w_ref, o_ref):
    x = h_ref[...] + p_ref[...]
    ms = jnp.mean(x * x, axis=-1, keepdims=True)
    o_ref[...] = x * lax.rsqrt(ms + RMS_EPS) * w_ref[...]


def final_norm(h1, peer, w, *, tm=256):
    t, d = h1.shape
    return pl.pallas_call(
        _final_kernel,
        grid=(t // tm,),
        in_specs=[pl.BlockSpec((tm, d), lambda i: (i, 0)), pl.BlockSpec((tm, d), lambda i: (i, 0)),
                  pl.BlockSpec((1, d), lambda i: (0, 0))],
        out_specs=pl.BlockSpec((tm, d), lambda i: (i, 0)),
        out_shape=jax.ShapeDtypeStruct((t, d), F32),
        compiler_params=_cparams(("parallel",)),
        name="final_norm",
    )(h1, peer, w.reshape(1, d))


def kernel(x, w_in, ssd_conv_w, ssd_conv_b, ssd_dt_bias, ssd_a_log, ssd_d, ssd_norm_w, sc_conv_w, sc_norm_w, w_out,
           norm_mix_w, norm_ffn_w, peer_w_query, peer_sub_keys, peer_u, peer_v, norm_final_w):
    batch, seq, d = x.shape
    t = batch * seq
    h = x.reshape(t, d)
    assert w_in.shape[0] == 1, "single-layer block"
    o1 = D_SSD
    o_b = o1 + D_SSD
    o_c = o_b + N_GROUPS * D_STATE
    o2 = o_c + N_GROUPS * D_STATE
    o3 = o2 + N_HEADS
    col_off = {"z": 0, "xs": D_SSD, "scb": 2 * D_SSD, "scc": 2 * D_SSD + D_CONV, "sch": 2 * D_SSD + 2 * D_CONV,
               "B": 2 * D_SSD + 3 * D_CONV, "C": 2 * D_SSD + 3 * D_CONV + N_GROUPS * D_STATE}
    wl = w_in[0]
    w_main = jnp.concatenate([wl[:, :o_b], wl[:, o3:], wl[:, o_b:o2]], axis=1).astype(BF16)
    w_dt = jnp.pad(wl[:, o2:o3], ((0, 0), (0, 128 - N_HEADS))).astype(BF16)
    w_out_b = w_out[0].astype(BF16)
    wq_b = peer_w_query[0].astype(BF16)
    keys = peer_sub_keys[0].reshape(2 * PEER_HEADS, N_KEYS, N_KEYS).astype(BF16)
    u_b = peer_u[0].astype(BF16)
    v_b = peer_v[0].astype(BF16)

    xn = rmsnorm_rows(h, norm_mix_w[0])
    proj = matmul(xn, w_main, tm=1024, tn=1024, tk=d, out_dtype=BF16, name="in_projection")
    dt_raw = matmul(xn, w_dt, tm=1024, tn=128, tk=d, out_dtype=F32, name="dt_projection")
    y_ssd = ssd_mixer(proj, dt_raw, ssd_conv_w[0], ssd_conv_b[0], ssd_dt_bias[0], ssd_a_log[0], ssd_d[0],
                      ssd_norm_w[0], batch=batch, seq=seq, col_off=col_off)
    y_sc = shortconv_mixer(proj, sc_conv_w[0], sc_norm_w[0], batch=batch, seq=seq, col_off=col_off)
    h = out_projection(y_ssd, y_sc, w_out_b, h)

    xn2 = rmsnorm_rows(h, norm_ffn_w[0])
    q = query_projection(xn2, wq_b)
    phi, e0, e1 = peer_routing(q, keys)
    peer = peer_dense(xn2, u_b, v_b, phi, e0, e1)
    out = final_norm(h, peer, norm_final_w)
    return out.reshape(batch, seq, d)
```

```python
import functools
import math

import jax
import jax.numpy as jnp
from jax import lax
from jax.experimental import pallas as pl
from jax.experimental.pallas import tpu as pltpu

F32 = jnp.float32
BF16 = jnp.bfloat16

D_MODEL = 4096
D_SSD = 4096
D_CONV = 4096
HEAD_DIM = 64
N_HEADS = 64
N_GROUPS = 8
HEADS_PER_GROUP = 8
D_STATE = 128
CHUNK = 128
GROUP_W = HEADS_PER_GROUP * HEAD_DIM
PEER_HEADS = 8
N_KEYS = 128
N_EXPERTS = N_KEYS * N_KEYS
TOPK = 16
RMS_EPS = 1e-6

VMEM_LIMIT_BYTES = 58 * 1024 * 1024
CARRY_ROWS = 8


def _cparams(sem):
    return pltpu.CompilerParams(dimension_semantics=sem, vmem_limit_bytes=VMEM_LIMIT_BYTES)


def _rmsnorm_kernel(x_ref, w_ref, o_ref):
    x = x_ref[...]
    ms = jnp.mean(x * x, axis=-1, keepdims=True)
    o_ref[...] = (x * lax.rsqrt(ms + RMS_EPS) * w_ref[...]).astype(o_ref.dtype)


def rmsnorm_rows(x, w, *, tm=256):
    t, d = x.shape
    return pl.pallas_call(
        _rmsnorm_kernel,
        grid=(t // tm,),
        in_specs=[pl.BlockSpec((tm, d), lambda i: (i, 0)), pl.BlockSpec((1, d), lambda i: (0, 0))],
        out_specs=pl.BlockSpec((tm, d), lambda i: (i, 0)),
        out_shape=jax.ShapeDtypeStruct((t, d), BF16),
        compiler_params=_cparams(("parallel",)),
        name="rmsnorm_rows",
    )(x, w.reshape(1, d))


def _matmul_kernel(*refs, nk, has_res, b_transposed):
    if has_res:
        a_ref, b_ref, r_ref, o_ref = refs[:4]
    else:
        a_ref, b_ref, o_ref = refs[:3]
        r_ref = None
    dims = (((1,), (1,)), ((), ())) if b_transposed else (((1,), (0,)), ((), ()))
    if nk == 1:
        acc = lax.dot_general(a_ref[...], b_ref[...], dims, preferred_element_type=F32)
        if has_res:
            acc = acc + r_ref[...]
        o_ref[...] = acc.astype(o_ref.dtype)
        return
    acc_ref = refs[-1]
    k = pl.program_id(2)

    @pl.when(k == 0)
    def _():
        acc_ref[...] = jnp.zeros_like(acc_ref)

    acc_ref[...] += lax.dot_general(a_ref[...], b_ref[...], dims, preferred_element_type=F32)

    @pl.when(k == nk - 1)
    def _():
        acc = acc_ref[...]
        if has_res:
            acc = acc + r_ref[...]
        o_ref[...] = acc.astype(o_ref.dtype)


def matmul(a, b, *, tm, tn, tk, out_dtype, residual=None, b_transposed=False, n=None, n_start=0, name="matmul"):
    m, kdim = a.shape
    if n is None:
        n = b.shape[0] if b_transposed else b.shape[1]
    tm, tn, tk = min(tm, m), min(tn, n), min(tk, kdim)
    nk = kdim // tk
    has_res = residual is not None
    j0 = n_start // tn
    b_spec = (pl.BlockSpec((tn, tk), lambda i, j, k: (j0 + j, k)) if b_transposed
              else pl.BlockSpec((tk, tn), lambda i, j, k: (k, j0 + j)))
    in_specs = [pl.BlockSpec((tm, tk), lambda i, j, k: (i, k)), b_spec]
    args = [a, b]
    if has_res:
        in_specs.append(pl.BlockSpec((tm, tn), lambda i, j, k: (i, j)))
        args.append(residual)
    scratch = [pltpu.VMEM((tm, tn), F32)] if nk > 1 else []
    return pl.pallas_call(
        functools.partial(_matmul_kernel, nk=nk, has_res=has_res, b_transposed=b_transposed),
        grid=(m // tm, n // tn, nk),
        in_specs=in_specs,
        out_specs=pl.BlockSpec((tm, tn), lambda i, j, k: (i, j)),
        out_shape=jax.ShapeDtypeStruct((m, n), out_dtype),
        scratch_shapes=scratch,
        compiler_params=_cparams(("parallel", "parallel", "arbitrary")),
        name=name,
    )(*args)


def _silu(x):
    hx = 0.5 * x
    return hx + hx * jnp.tanh(hx)


def _softplus(x):
    return jnp.maximum(x, 0.0) + jnp.log(1.0 + jnp.exp(-jnp.abs(x)))


_P_CONV, _P_BIAS, _P_DSKIP, _P_NORM = 0, 4, 5, 6


def _ssd_kernel(z_ref, xs_ref, b_ref, c_ref, dt_ref, pg_ref, pdt_ref, o_ref, state_ref, ext_ref, *, chunks):
    g = pl.program_id(1)
    c = pl.program_id(2)
    L = CHUNK
    W = GROUP_W
    N = D_STATE
    XBC = W + 2 * N

    @pl.when(c == 0)
    def _():
        state_ref[...] = jnp.zeros_like(state_ref)
        ext_ref[0:CARRY_ROWS, :] = jnp.zeros((CARRY_ROWS, XBC), F32)

    row = lax.broadcasted_iota(jnp.int32, (L, L), 0)
    col = lax.broadcasted_iota(jnp.int32, (L, L), 1)
    causal = row >= col
    tril = causal.astype(BF16)
    er = lax.broadcasted_iota(jnp.int32, (128, W), 0)
    ec = lax.broadcasted_iota(jnp.int32, (128, W), 1)
    expand = (er == ec // HEAD_DIM).astype(BF16)

    def chunk(ci, carry):
        rs = pl.ds(pl.multiple_of(ci * L, L), L)
        ext_ref[CARRY_ROWS:CARRY_ROWS + L, 0:W] = xs_ref[rs, :].astype(F32)
        ext_ref[CARRY_ROWS:CARRY_ROWS + L, W:W + N] = b_ref[rs, :].astype(F32)
        ext_ref[CARRY_ROWS:CARRY_ROWS + L, W + N:XBC] = c_ref[rs, :].astype(F32)
        acc = ext_ref[CARRY_ROWS:CARRY_ROWS + L, :] * pg_ref[_P_CONV + 3:_P_CONV + 4, :]
        for k in range(3):
            acc = acc + ext_ref[CARRY_ROWS - 3 + k:CARRY_ROWS - 3 + k + L, :] * pg_ref[_P_CONV + k:_P_CONV + k + 1, :]
        xbc = _silu(acc + pg_ref[_P_BIAS:_P_BIAS + 1, :])
        ext_ref[0:CARRY_ROWS, :] = ext_ref[L:L + CARRY_ROWS, :]
        xs = xbc[:, 0:W]
        b_b = xbc[:, W:W + N].astype(BF16)
        c_b = xbc[:, W + N:XBC].astype(BF16)

        dt_all = _softplus(dt_ref[rs, :] + pdt_ref[0:1, :])
        da_all = dt_all * (-jnp.exp(pdt_ref[1:2, :]))
        shift = (128 - g * HEADS_PER_GROUP) % 128
        dt_g = pltpu.roll(dt_all, shift, 1)
        da_g = pltpu.roll(da_all, shift, 1)

        hi = da_g.astype(BF16)
        r1 = da_g - hi.astype(F32)
        mid = r1.astype(BF16)
        lo = (r1 - mid.astype(F32)).astype(BF16)
        acum = (jnp.dot(tril, hi, preferred_element_type=F32)
                + jnp.dot(tril, mid, preferred_element_type=F32)
                + jnp.dot(tril, lo, preferred_element_type=F32))
        acum_t = acum.T
        a_last = acum[L - 1:L, :]

        def expand_heads(v):
            vh = v.astype(BF16)
            vl = (v - vh.astype(F32)).astype(BF16)
            return jnp.dot(vh, expand, preferred_element_type=F32) + jnp.dot(vl, expand, preferred_element_type=F32)

        dt_x = expand_heads(dt_g)
        ea_x = expand_heads(jnp.exp(acum))
        to_end_x = expand_heads(jnp.exp(a_last - acum))

        xdt = xs * dt_x
        xdt_b = xdt.astype(BF16)

        cb = lax.dot_general(c_b, b_b, (((1,), (1,)), ((), ())), preferred_element_type=F32)
        state = state_ref[...]
        y_off = jnp.dot(c_b, state.astype(BF16), preferred_element_type=F32) * ea_x

        y_parts = []
        for r in range(HEADS_PER_GROUP):
            seg = acum[:, r:r + 1] - acum_t[r:r + 1, :]
            decay = jnp.exp(jnp.where(causal, seg, -jnp.inf))
            m = (cb * decay).astype(BF16)
            y_parts.append(jnp.dot(m, xdt_b[:, r * HEAD_DIM:(r + 1) * HEAD_DIM], preferred_element_type=F32))
        y = jnp.concatenate(y_parts, axis=1) + y_off + pg_ref[_P_DSKIP:_P_DSKIP + 1, 0:W] * xs

        upd = lax.dot_general(b_b, (xdt * to_end_x).astype(BF16), (((0,), (0,)), ((), ())),
                              preferred_element_type=F32)
        state_ref[...] = state * ea_x[L - 1:L, :] + upd

        y = y * _silu(z_ref[rs, :].astype(F32))
        ms = jnp.mean(y * y, axis=-1, keepdims=True)
        o_ref[rs, :] = (y * lax.rsqrt(ms + RMS_EPS) * pg_ref[_P_NORM:_P_NORM + 1, 0:W]).astype(o_ref.dtype)
        return carry

    lax.fori_loop(0, chunks, chunk, 0)


def ssd_mixer(proj, dt_raw, conv_w, conv_b, dt_bias, a_log, d_skip, norm_w, *, batch, seq, col_off, chunks=4):
    t = batch * seq
    chunks = min(chunks, seq // CHUNK)
    nchunk = seq // (CHUNK * chunks)
    W, N, L, G = GROUP_W, D_STATE, CHUNK * chunks, N_GROUPS
    zb, xb, bb, cb_ = col_off["z"] // W, col_off["xs"] // W, col_off["B"] // N, col_off["C"] // N

    def rows(b, g, c):
        return b * nchunk + c

    def per_group(v):
        r = v.shape[0]
        x_part = v[:, :D_SSD].reshape(r, G, W)
        b_part = v[:, D_SSD:D_SSD + G * N].reshape(r, G, N)
        c_part = v[:, D_SSD + G * N:].reshape(r, G, N)
        return jnp.concatenate([x_part, b_part, c_part], axis=2).transpose(1, 0, 2)

    def x_only(v):
        return jnp.pad(v.astype(F32).reshape(G, 1, W), ((0, 0), (0, 0), (0, 2 * N)))

    pg = jnp.concatenate([per_group(conv_w.astype(F32)), per_group(conv_b.astype(F32).reshape(1, -1)),
                          x_only(jnp.repeat(d_skip, HEAD_DIM)), x_only(norm_w),
                          jnp.zeros((G, 1, W + 2 * N), F32)], axis=1)
    pad = 128 - N_HEADS
    pdt = jnp.pad(jnp.stack([dt_bias.astype(F32), a_log.astype(F32)]), ((0, 6), (0, pad)))

    in_specs = [
        pl.BlockSpec((L, W), lambda b, g, c: (rows(b, g, c), zb + g)),
        pl.BlockSpec((L, W), lambda b, g, c: (rows(b, g, c), xb + g)),
        pl.BlockSpec((L, N), lambda b, g, c: (rows(b, g, c), bb + g)),
        pl.BlockSpec((L, N), lambda b, g, c: (rows(b, g, c), cb_ + g)),
        pl.BlockSpec((L, 128), lambda b, g, c: (rows(b, g, c), 0)),
        pl.BlockSpec((None, 8, W + 2 * N), lambda b, g, c: (g, 0, 0)),
        pl.BlockSpec((8, 128), lambda b, g, c: (0, 0)),
    ]
    return pl.pallas_call(
        functools.partial(_ssd_kernel, chunks=chunks),
        grid=(batch, N_GROUPS, nchunk),
        in_specs=in_specs,
        out_specs=pl.BlockSpec((L, W), lambda b, g, c: (rows(b, g, c), g)),
        out_shape=jax.ShapeDtypeStruct((t, D_SSD), BF16),
        scratch_shapes=[pltpu.VMEM((N, W), F32), pltpu.VMEM((CHUNK + CARRY_ROWS, W + 2 * N), F32)],
        compiler_params=_cparams(("parallel", "parallel", "arbitrary")),
        name="ssd_mixer",
    )(proj, proj, proj, proj, dt_raw, pg, pdt)


def _shortconv_kernel(b_ref, c_ref, h_ref, w_ref, nw_ref, o_ref, ext_ref, y_ref, *, rows, col_chunk):
    s = pl.program_id(1)
    d = D_CONV

    @pl.when(s == 0)
    def _():
        ext_ref[0:CARRY_ROWS, :] = jnp.zeros((CARRY_ROWS, d), F32)

    ss = jnp.zeros((rows, 1), F32)
    for j in range(0, d, col_chunk):
        sl = slice(j, j + col_chunk)
        ext_ref[CARRY_ROWS:CARRY_ROWS + rows, sl] = c_ref[:, sl].astype(F32) * h_ref[:, sl].astype(F32)
        acc = ext_ref[CARRY_ROWS:CARRY_ROWS + rows, sl] * w_ref[2:3, sl]
        for k in range(2):
            acc = acc + ext_ref[CARRY_ROWS - 2 + k:CARRY_ROWS - 2 + k + rows, sl] * w_ref[k:k + 1, sl]
        y = b_ref[:, sl].astype(F32) * acc
        y_ref[:, sl] = y
        ss = ss + jnp.sum(y * y, axis=-1, keepdims=True)
    ext_ref[0:CARRY_ROWS, :] = ext_ref[rows:rows + CARRY_ROWS, :]
    inv = lax.rsqrt(ss * (1.0 / d) + RMS_EPS)
    for j in range(0, d, col_chunk):
        sl = slice(j, j + col_chunk)
        o_ref[:, sl] = (y_ref[:, sl] * inv * nw_ref[:, sl]).astype(o_ref.dtype)


def shortconv_mixer(proj, conv_w, norm_w, *, batch, seq, col_off, rows=256, col_chunk=512):
    t = batch * seq
    nblk = seq // rows
    d = D_CONV
    bb, cb_, hb = col_off["scb"] // d, col_off["scc"] // d, col_off["sch"] // d
    return pl.pallas_call(
        functools.partial(_shortconv_kernel, rows=rows, col_chunk=col_chunk),
        grid=(batch, nblk),
        in_specs=[
            pl.BlockSpec((rows, d), lambda b, s: (b * nblk + s, bb)),
            pl.BlockSpec((rows, d), lambda b, s: (b * nblk + s, cb_)),
            pl.BlockSpec((rows, d), lambda b, s: (b * nblk + s, hb)),
            pl.BlockSpec((3, d), lambda b, s: (0, 0)),
            pl.BlockSpec((1, d), lambda b, s: (0, 0)),
        ],
        out_specs=pl.BlockSpec((rows, d), lambda b, s: (b * nblk + s, 0)),
        out_shape=jax.ShapeDtypeStruct((t, d), BF16),
        scratch_shapes=[pltpu.VMEM((rows + CARRY_ROWS, d), F32), pltpu.VMEM((rows, d), F32)],
        compiler_params=_cparams(("parallel", "arbitrary")),
        name="shortconv_mixer",
    )(proj, proj, proj, conv_w.astype(F32), norm_w.astype(F32).reshape(1, d))


def _outproj_kernel(a1_ref, a2_ref, b1_ref, b2_ref, r_ref, o_ref):
    acc = jnp.dot(a1_ref[...], b1_ref[...], preferred_element_type=F32)
    acc = acc + jnp.dot(a2_ref[...], b2_ref[...], preferred_element_type=F32)
    o_ref[...] = acc + r_ref[...]


def out_projection(y_ssd, y_sc, w_out, x, *, tm=512, tn=512):
    m, kh = y_ssd.shape
    _, n = w_out.shape
    tm = min(tm, m)
    return pl.pallas_call(
        _outproj_kernel,
        grid=(m // tm, n // tn),
        in_specs=[
            pl.BlockSpec((tm, kh), lambda i, j: (i, 0)),
            pl.BlockSpec((tm, kh), lambda i, j: (i, 0)),
            pl.BlockSpec((kh, tn), lambda i, j: (0, j)),
            pl.BlockSpec((kh, tn), lambda i, j: (1, j)),
            pl.BlockSpec((tm, tn), lambda i, j: (i, j)),
        ],
        out_specs=pl.BlockSpec((tm, tn), lambda i, j: (i, j)),
        out_shape=jax.ShapeDtypeStruct((m, n), F32),
        compiler_params=_cparams(("parallel", "parallel")),
        name="out_projection",
    )(y_ssd, y_sc, w_out, w_out, x)


def _top_values(x, n):
    outs = []
    for r in range(n):
        m = jnp.max(x, axis=0, keepdims=True)
        outs.append(m)
        if r < n - 1:
            x = jnp.where(x >= m, -jnp.inf, x)
    return outs


def _query_kernel(x_ref, w_ref, o_ref):
    acc = jnp.dot(x_ref[...], w_ref[...], preferred_element_type=F32)
    for c in range(o_ref.shape[0]):
        o_ref[c] = acc[:, c * N_KEYS:(c + 1) * N_KEYS].astype(o_ref.dtype)


def query_projection(xn, wq, *, tm=1024, tn=1024):
    t, d = xn.shape
    n = wq.shape[1]
    tm = min(tm, t)
    per = tn // N_KEYS
    return pl.pallas_call(
        _query_kernel,
        grid=(t // tm, n // tn),
        in_specs=[pl.BlockSpec((tm, d), lambda i, j: (i, 0)), pl.BlockSpec((d, tn), lambda i, j: (0, j))],
        out_specs=pl.BlockSpec((per, tm, N_KEYS), lambda i, j: (j, i, 0)),
        out_shape=jax.ShapeDtypeStruct((n // N_KEYS, t, N_KEYS), BF16),
        compiler_params=_cparams(("parallel", "parallel")),
        name="query_projection",
    )(xn, wq)


def _route_kernel(q_ref, keys_ref, phi_ref, e0_ref, e1_ref):
    nt = (((1,), (1,)), ((), ()))

    def head(h, carry):
        s0 = lax.dot_general(keys_ref[2 * h], q_ref[2 * h], nt, preferred_element_type=F32)
        s1 = lax.dot_general(keys_ref[2 * h + 1], q_ref[2 * h + 1], nt, preferred_element_type=F32)
        top0 = _top_values(s0, TOPK + 1)
        top1 = _top_values(s1, TOPK + 1)
        t1 = jnp.concatenate(top1, axis=0)
        pieces = [top0[r] + t1[0:(TOPK + 1) // (r + 1)] for r in range(TOPK + 1)]
        n_cand = sum(p.shape[0] for p in pieces)
        pieces.append(jnp.full((-n_cand % 8, s0.shape[1]), -jnp.inf, F32))
        cand = jnp.concatenate(pieces, axis=0)
        best = _top_values(cand, TOPK + 1)
        m = best[0]
        z = jnp.exp(best[0] - m)
        for r in range(1, TOPK):
            z = z + jnp.exp(best[r] - m)
        inv_z = 1.0 / z
        tau = 0.5 * (best[TOPK - 1] + best[TOPK])
        m0, m1 = top0[0], top1[0]
        e0_ref[h] = jnp.exp(s0 - m0)
        e1_ref[h] = jnp.exp(s1 - m1) * inv_z
        phi_ref[h] = jnp.exp((tau - m1) - s0) * inv_z
        return carry

    lax.fori_loop(0, PEER_HEADS, head, 0)


def peer_routing(q, keys, *, tb=256):
    _, t, _ = q.shape
    shp = jax.ShapeDtypeStruct((PEER_HEADS, N_KEYS, t), F32)
    ospec = pl.BlockSpec((PEER_HEADS, N_KEYS, tb), lambda i: (0, 0, i))
    return pl.pallas_call(
        _route_kernel,
        grid=(t // tb,),
        in_specs=[pl.BlockSpec((2 * PEER_HEADS, tb, N_KEYS), lambda i: (0, i, 0)),
                  pl.BlockSpec((2 * PEER_HEADS, N_KEYS, N_KEYS), lambda i: (0, 0, 0))],
        out_specs=[ospec, ospec, ospec],
        out_shape=[shp, shp, shp],
        compiler_params=_cparams(("parallel",)),
        name="peer_routing",
    )(q, keys)


def _gelu_exact(x):
    return 0.5 * x * (1.0 + lax.erf(x * (1.0 / math.sqrt(2.0))))


def _peer_dense_kernel(x_ref, u_ref, v_ref, phi_ref, e0_ref, e1_ref, o_ref, a_ref, *, te):
    et = pl.program_id(1)
    nblk = te // N_KEYS
    tb = x_ref.shape[0]
    tq = 128
    nt = (((1,), (1,)), ((), ()))

    @pl.when(et == 0)
    def _():
        o_ref[...] = jnp.zeros_like(o_ref)

    ht = lax.dot_general(u_ref[...], x_ref[...], nt, preferred_element_type=F32)
    for j in range(nblk):
        i = et * nblk + j
        rows = slice(j * N_KEYS, (j + 1) * N_KEYS)
        phi_rows = [phi_ref[h, pl.ds(i, 1), :] for h in range(PEER_HEADS)]
        e0_rows = [e0_ref[h, pl.ds(i, 1), :] for h in range(PEER_HEADS)]
        for q in range(tb // tq):
            ts = slice(q * tq, (q + 1) * tq)
            gate = None
            for h in range(PEER_HEADS):
                e1 = e1_ref[h, :, ts]
                sel = jnp.where(e1 >= phi_rows[h][:, ts], e1, 0.0) * e0_rows[h][:, ts]
                gate = sel if gate is None else gate + sel
            act = _gelu_exact(ht[rows, ts]) * gate
            a_ref[ts, rows] = act.T.astype(a_ref.dtype)
    o_ref[...] += jnp.dot(a_ref[...], v_ref[...], preferred_element_type=F32)


def peer_dense(xn, u, v, phi, e0, e1, *, tb=512, te=512):
    t, d = xn.shape
    e = u.shape[0]
    tb = min(tb, t)
    rspec = pl.BlockSpec((PEER_HEADS, N_KEYS, tb), lambda i, j: (0, 0, i))
    return pl.pallas_call(
        functools.partial(_peer_dense_kernel, te=te),
        grid=(t // tb, e // te),
        in_specs=[
            pl.BlockSpec((tb, d), lambda i, j: (i, 0)),
            pl.BlockSpec((te, d), lambda i, j: (j, 0)),
            pl.BlockSpec((te, d), lambda i, j: (j, 0)),
            rspec, rspec, rspec,
        ],
        out_specs=pl.BlockSpec((tb, d), lambda i, j: (i, 0)),
        out_shape=jax.ShapeDtypeStruct((t, d), F32),
        scratch_shapes=[pltpu.VMEM((tb, te), BF16)],
        compiler_params=_cparams(("parallel", "arbitrary")),
        name="peer_dense",
    )(xn, u, v, phi, e0, e1)


def _final_kernel(h_ref, p_ref, w_ref, o_ref):
    x = h_ref[...] + p_ref[...]
    ms = jnp.mean(x * x, axis=-1, keepdims=True)
    o_ref[...] = x * lax.rsqrt(ms + RMS_EPS) * w_ref[...]


def final_norm(h1, peer, w, *, tm=256):
    t, d = h1.shape
    return pl.pallas_call(
        _final_kernel,
        grid=(t // tm,),
        in_specs=[pl.BlockSpec((tm, d), lambda i: (i, 0)), pl.BlockSpec((tm, d), lambda i: (i, 0)),
                  pl.BlockSpec((1, d), lambda i: (0, 0))],
        out_specs=pl.BlockSpec((tm, d), lambda i: (i, 0)),
        out_shape=jax.ShapeDtypeStruct((t, d), F32),
        compiler_params=_cparams(("parallel",)),
        name="final_norm",
    )(h1, peer, w.reshape(1, d))


def kernel(x, w_in, ssd_conv_w, ssd_conv_b, ssd_dt_bias, ssd_a_log, ssd_d, ssd_norm_w, sc_conv_w, sc_norm_w, w_out,
           norm_mix_w, norm_ffn_w, peer_w_query, peer_sub_keys, peer_u, peer_v, norm_final_w):
    batch, seq, d = x.shape
    t = batch * seq
    h = x.reshape(t, d)
    assert w_in.shape[0] == 1, "single-layer block"
    o_c = 2 * D_SSD + N_GROUPS * D_STATE
    o2 = o_c + N_GROUPS * D_STATE
    o3 = o2 + N_HEADS
    ssd_off = {"z": 0, "xs": D_SSD, "B": 2 * D_SSD, "C": o_c}
    sc_off = {"scb": 0, "scc": D_CONV, "sch": 2 * D_CONV}
    wt = w_in[0].T.astype(BF16)
    w_sc = wt[o3:]
    w_out_b = w_out[0].astype(BF16)
    wq_b = peer_w_query[0].astype(BF16)
    keys = peer_sub_keys[0].reshape(2 * PEER_HEADS, N_KEYS, N_KEYS).astype(BF16)
    u_b = peer_u[0].astype(BF16)
    v_b = peer_v[0].astype(BF16)

    xn = rmsnorm_rows(h, norm_mix_w[0])
    proj_ssd = matmul(xn, wt, tm=1024, tn=1024, tk=d, out_dtype=BF16, b_transposed=True, n=o2,
                      name="in_projection_ssd")
    proj_sc = matmul(xn, w_sc, tm=1024, tn=1024, tk=d, out_dtype=BF16, b_transposed=True, name="in_projection_sc")
    dt_raw = matmul(xn, wt, tm=1024, tn=128, tk=d, out_dtype=F32, b_transposed=True, n=128, n_start=o2,
                    name="dt_projection")
    y_ssd = ssd_mixer(proj_ssd, dt_raw, ssd_conv_w[0], ssd_conv_b[0], ssd_dt_bias[0], ssd_a_log[0], ssd_d[0],
                      ssd_norm_w[0], batch=batch, seq=seq, col_off=ssd_off)
    y_sc = shortconv_mixer(proj_sc, sc_conv_w[0], sc_norm_w[0], batch=batch, seq=seq, col_off=sc_off)
    h = out_projection(y_ssd, y_sc, w_out_b, h)

    xn2 = rmsnorm_rows(h, norm_ffn_w[0])
    q = query_projection(xn2, wq_b)
    phi, e0, e1 = peer_routing(q, keys)
    peer = peer_dense(xn2, u_b, v_b, phi, e0, e1)
    out = final_norm(h, peer, norm_final_w)
    return out.reshape(batch, seq, d)
```

```python
import functools
import math

import jax
import jax.numpy as jnp
from jax import lax
from jax.experimental import pallas as pl
from jax.experimental.pallas import tpu as pltpu

F32 = jnp.float32
BF16 = jnp.bfloat16

D_MODEL = 4096
D_SSD = 4096
D_CONV = 4096
HEAD_DIM = 64
N_HEADS = 64
N_GROUPS = 8
HEADS_PER_GROUP = 8
D_STATE = 128
CHUNK = 128
GROUP_W = HEADS_PER_GROUP * HEAD_DIM
PEER_HEADS = 8
N_KEYS = 128
N_EXPERTS = N_KEYS * N_KEYS
TOPK = 16
RMS_EPS = 1e-6

VMEM_LIMIT_BYTES = 58 * 1024 * 1024
CARRY_ROWS = 8


def _cparams(sem):
    return pltpu.CompilerParams(dimension_semantics=sem, vmem_limit_bytes=VMEM_LIMIT_BYTES)


def _rmsnorm_kernel(x_ref, w_ref, o_ref):
    x = x_ref[...]
    ms = jnp.mean(x * x, axis=-1, keepdims=True)
    o_ref[...] = (x * lax.rsqrt(ms + RMS_EPS) * w_ref[...]).astype(o_ref.dtype)


def rmsnorm_rows(x, w, *, tm=256):
    t, d = x.shape
    return pl.pallas_call(
        _rmsnorm_kernel,
        grid=(t // tm,),
        in_specs=[pl.BlockSpec((tm, d), lambda i: (i, 0)), pl.BlockSpec((1, d), lambda i: (0, 0))],
        out_specs=pl.BlockSpec((tm, d), lambda i: (i, 0)),
        out_shape=jax.ShapeDtypeStruct((t, d), BF16),
        compiler_params=_cparams(("parallel",)),
        name="rmsnorm_rows",
    )(x, w.reshape(1, d))


def _matmul_kernel(*refs, nk, has_res, b_transposed):
    if has_res:
        a_ref, b_ref, r_ref, o_ref = refs[:4]
    else:
        a_ref, b_ref, o_ref = refs[:3]
        r_ref = None
    dims = (((1,), (1,)), ((), ())) if b_transposed else (((1,), (0,)), ((), ()))
    if nk == 1:
        acc = lax.dot_general(a_ref[...], b_ref[...], dims, preferred_element_type=F32)
        if has_res:
            acc = acc + r_ref[...]
        o_ref[...] = acc.astype(o_ref.dtype)
        return
    acc_ref = refs[-1]
    k = pl.program_id(2)

    @pl.when(k == 0)
    def _():
        acc_ref[...] = jnp.zeros_like(acc_ref)

    acc_ref[...] += lax.dot_general(a_ref[...], b_ref[...], dims, preferred_element_type=F32)

    @pl.when(k == nk - 1)
    def _():
        acc = acc_ref[...]
        if has_res:
            acc = acc + r_ref[...]
        o_ref[...] = acc.astype(o_ref.dtype)


def matmul(a, b, *, tm, tn, tk, out_dtype, residual=None, b_transposed=False, n=None, n_start=0, name="matmul"):
    m, kdim = a.shape
    if n is None:
        n = b.shape[0] if b_transposed else b.shape[1]
    tm, tn, tk = min(tm, m), min(tn, n), min(tk, kdim)
    nk = kdim // tk
    has_res = residual is not None
    j0 = n_start // tn
    b_spec = (pl.BlockSpec((tn, tk), lambda i, j, k: (j0 + j, k)) if b_transposed
              else pl.BlockSpec((tk, tn), lambda i, j, k: (k, j0 + j)))
    in_specs = [pl.BlockSpec((tm, tk), lambda i, j, k: (i, k)), b_spec]
    args = [a, b]
    if has_res:
        in_specs.append(pl.BlockSpec((tm, tn), lambda i, j, k: (i, j)))
        args.append(residual)
    scratch = [pltpu.VMEM((tm, tn), F32)] if nk > 1 else []
    return pl.pallas_call(
        functools.partial(_matmul_kernel, nk=nk, has_res=has_res, b_transposed=b_transposed),
        grid=(m // tm, n // tn, nk),
        in_specs=in_specs,
        out_specs=pl.BlockSpec((tm, tn), lambda i, j, k: (i, j)),
        out_shape=jax.ShapeDtypeStruct((m, n), out_dtype),
        scratch_shapes=scratch,
        compiler_params=_cparams(("parallel", "parallel", "arbitrary")),
        name=name,
    )(*args)


def _silu(x):
    hx = 0.5 * x
    return hx + hx * jnp.tanh(hx)


def _softplus(x):
    return jnp.maximum(x, 0.0) + jnp.log(1.0 + jnp.exp(-jnp.abs(x)))


_P_CONV, _P_BIAS, _P_DSKIP, _P_NORM = 0, 4, 5, 6


def _ssd_kernel(z_ref, xs_ref, b_ref, c_ref, dt_ref, pg_ref, pdt_ref, o_ref, state_ref, ext_ref, *, chunks):
    g = pl.program_id(1)
    c = pl.program_id(2)
    L = CHUNK
    W = GROUP_W
    N = D_STATE
    XBC = W + 2 * N

    @pl.when(c == 0)
    def _():
        state_ref[...] = jnp.zeros_like(state_ref)
        ext_ref[0:CARRY_ROWS, :] = jnp.zeros((CARRY_ROWS, XBC), F32)

    row = lax.broadcasted_iota(jnp.int32, (L, L), 0)
    col = lax.broadcasted_iota(jnp.int32, (L, L), 1)
    causal = row >= col
    tril = causal.astype(BF16)
    er = lax.broadcasted_iota(jnp.int32, (128, W), 0)
    ec = lax.broadcasted_iota(jnp.int32, (128, W), 1)
    expand = (er == ec // HEAD_DIM).astype(BF16)

    def chunk(ci, carry):
        rs = pl.ds(pl.multiple_of(ci * L, L), L)
        ext_ref[CARRY_ROWS:CARRY_ROWS + L, 0:W] = xs_ref[rs, :].astype(F32)
        ext_ref[CARRY_ROWS:CARRY_ROWS + L, W:W + N] = b_ref[rs, :].astype(F32)
        ext_ref[CARRY_ROWS:CARRY_ROWS + L, W + N:XBC] = c_ref[rs, :].astype(F32)
        acc = ext_ref[CARRY_ROWS:CARRY_ROWS + L, :] * pg_ref[_P_CONV + 3:_P_CONV + 4, :]
        for k in range(3):
            acc = acc + ext_ref[CARRY_ROWS - 3 + k:CARRY_ROWS - 3 + k + L, :] * pg_ref[_P_CONV + k:_P_CONV + k + 1, :]
        xbc = _silu(acc + pg_ref[_P_BIAS:_P_BIAS + 1, :])
        ext_ref[0:CARRY_ROWS, :] = ext_ref[L:L + CARRY_ROWS, :]
        xs = xbc[:, 0:W]
        b_b = xbc[:, W:W + N].astype(BF16)
        c_b = xbc[:, W + N:XBC].astype(BF16)

        dt_all = _softplus(dt_ref[rs, :] + pdt_ref[0:1, :])
        da_all = dt_all * (-jnp.exp(pdt_ref[1:2, :]))
        shift = (128 - g * HEADS_PER_GROUP) % 128
        dt_g = pltpu.roll(dt_all, shift, 1)
        da_g = pltpu.roll(da_all, shift, 1)

        hi = da_g.astype(BF16)
        r1 = da_g - hi.astype(F32)
        mid = r1.astype(BF16)
        lo = (r1 - mid.astype(F32)).astype(BF16)
        acum = (jnp.dot(tril, hi, preferred_element_type=F32)
                + jnp.dot(tril, mid, preferred_element_type=F32)
                + jnp.dot(tril, lo, preferred_element_type=F32))
        acum_t = acum.T
        a_last = acum[L - 1:L, :]

        def expand_heads(v):
            vh = v.astype(BF16)
            vl = (v - vh.astype(F32)).astype(BF16)
            return jnp.dot(vh, expand, preferred_element_type=F32) + jnp.dot(vl, expand, preferred_element_type=F32)

        dt_x = expand_heads(dt_g)
        ea_x = expand_heads(jnp.exp(acum))
        to_end_x = expand_heads(jnp.exp(a_last - acum))

        xdt = xs * dt_x
        xdt_b = xdt.astype(BF16)

        cb = lax.dot_general(c_b, b_b, (((1,), (1,)), ((), ())), preferred_element_type=F32)
        state = state_ref[...]
        y_off = jnp.dot(c_b, state.astype(BF16), preferred_element_type=F32) * ea_x

        y_parts = []
        for r in range(HEADS_PER_GROUP):
            seg = acum[:, r:r + 1] - acum_t[r:r + 1, :]
            decay = jnp.exp(jnp.where(causal, seg, -jnp.inf))
            m = (cb * decay).astype(BF16)
            y_parts.append(jnp.dot(m, xdt_b[:, r * HEAD_DIM:(r + 1) * HEAD_DIM], preferred_element_type=F32))
        y = jnp.concatenate(y_parts, axis=1) + y_off + pg_ref[_P_DSKIP:_P_DSKIP + 1, 0:W] * xs

        upd = lax.dot_general(b_b, (xdt * to_end_x).astype(BF16), (((0,), (0,)), ((), ())),
                              preferred_element_type=F32)
        state_ref[...] = state * ea_x[L - 1:L, :] + upd

        y = y * _silu(z_ref[rs, :].astype(F32))
        ms = jnp.mean(y * y, axis=-1, keepdims=True)
        o_ref[rs, :] = (y * lax.rsqrt(ms + RMS_EPS) * pg_ref[_P_NORM:_P_NORM + 1, 0:W]).astype(o_ref.dtype)
        return carry

    lax.fori_loop(0, chunks, chunk, 0)


def ssd_mixer(proj, dt_raw, conv_w, conv_b, dt_bias, a_log, d_skip, norm_w, *, batch, seq, col_off, chunks=8):
    t = batch * seq
    chunks = min(chunks, seq // CHUNK)
    nchunk = seq // (CHUNK * chunks)
    W, N, L, G = GROUP_W, D_STATE, CHUNK * chunks, N_GROUPS
    zb, xb, bb, cb_ = col_off["z"] // W, col_off["xs"] // W, col_off["B"] // N, col_off["C"] // N

    def rows(b, g, c):
        return b * nchunk + c

    def per_group(v):
        r = v.shape[0]
        x_part = v[:, :D_SSD].reshape(r, G, W)
        b_part = v[:, D_SSD:D_SSD + G * N].reshape(r, G, N)
        c_part = v[:, D_SSD + G * N:].reshape(r, G, N)
        return jnp.concatenate([x_part, b_part, c_part], axis=2).transpose(1, 0, 2)

    def x_only(v):
        return jnp.pad(v.astype(F32).reshape(G, 1, W), ((0, 0), (0, 0), (0, 2 * N)))

    pg = jnp.concatenate([per_group(conv_w.astype(F32)), per_group(conv_b.astype(F32).reshape(1, -1)),
                          x_only(jnp.repeat(d_skip, HEAD_DIM)), x_only(norm_w),
                          jnp.zeros((G, 1, W + 2 * N), F32)], axis=1)
    pad = 128 - N_HEADS
    pdt = jnp.pad(jnp.stack([dt_bias.astype(F32), a_log.astype(F32)]), ((0, 6), (0, pad)))

    in_specs = [
        pl.BlockSpec((L, W), lambda b, g, c: (rows(b, g, c), zb + g)),
        pl.BlockSpec((L, W), lambda b, g, c: (rows(b, g, c), xb + g)),
        pl.BlockSpec((L, N), lambda b, g, c: (rows(b, g, c), bb + g)),
        pl.BlockSpec((L, N), lambda b, g, c: (rows(b, g, c), cb_ + g)),
        pl.BlockSpec((L, 128), lambda b, g, c: (rows(b, g, c), 0)),
        pl.BlockSpec((None, 8, W + 2 * N), lambda b, g, c: (g, 0, 0)),
        pl.BlockSpec((8, 128), lambda b, g, c: (0, 0)),
    ]
    return pl.pallas_call(
        functools.partial(_ssd_kernel, chunks=chunks),
        grid=(batch, N_GROUPS, nchunk),
        in_specs=in_specs,
        out_specs=pl.BlockSpec((L, W), lambda b, g, c: (rows(b, g, c), g)),
        out_shape=jax.ShapeDtypeStruct((t, D_SSD), BF16),
        scratch_shapes=[pltpu.VMEM((N, W), F32), pltpu.VMEM((CHUNK + CARRY_ROWS, W + 2 * N), F32)],
        compiler_params=_cparams(("parallel", "parallel", "arbitrary")),
        name="ssd_mixer",
    )(proj, proj, proj, proj, dt_raw, pg, pdt)


def _shortconv_kernel(b_ref, c_ref, h_ref, w_ref, nw_ref, o_ref, ext_ref, y_ref, *, rows, col_chunk):
    s = pl.program_id(1)
    d = D_CONV

    @pl.when(s == 0)
    def _():
        ext_ref[0:CARRY_ROWS, :] = jnp.zeros((CARRY_ROWS, d), F32)

    ss = jnp.zeros((rows, 1), F32)
    for j in range(0, d, col_chunk):
        sl = slice(j, j + col_chunk)
        ext_ref[CARRY_ROWS:CARRY_ROWS + rows, sl] = c_ref[:, sl].astype(F32) * h_ref[:, sl].astype(F32)
        acc = ext_ref[CARRY_ROWS:CARRY_ROWS + rows, sl] * w_ref[2:3, sl]
        for k in range(2):
            acc = acc + ext_ref[CARRY_ROWS - 2 + k:CARRY_ROWS - 2 + k + rows, sl] * w_ref[k:k + 1, sl]
        y = b_ref[:, sl].astype(F32) * acc
        y_ref[:, sl] = y
        ss = ss + jnp.sum(y * y, axis=-1, keepdims=True)
    ext_ref[0:CARRY_ROWS, :] = ext_ref[rows:rows + CARRY_ROWS, :]
    inv = lax.rsqrt(ss * (1.0 / d) + RMS_EPS)
    for j in range(0, d, col_chunk):
        sl = slice(j, j + col_chunk)
        o_ref[:, sl] = (y_ref[:, sl] * inv * nw_ref[:, sl]).astype(o_ref.dtype)


def shortconv_mixer(proj, conv_w, norm_w, *, batch, seq, col_off, rows=256, col_chunk=512):
    t = batch * seq
    nblk = seq // rows
    d = D_CONV
    bb, cb_, hb = col_off["scb"] // d, col_off["scc"] // d, col_off["sch"] // d
    return pl.pallas_call(
        functools.partial(_shortconv_kernel, rows=rows, col_chunk=col_chunk),
        grid=(batch, nblk),
        in_specs=[
            pl.BlockSpec((rows, d), lambda b, s: (b * nblk + s, bb)),
            pl.BlockSpec((rows, d), lambda b, s: (b * nblk + s, cb_)),
            pl.BlockSpec((rows, d), lambda b, s: (b * nblk + s, hb)),
            pl.BlockSpec((3, d), lambda b, s: (0, 0)),
            pl.BlockSpec((1, d), lambda b, s: (0, 0)),
        ],
        out_specs=pl.BlockSpec((rows, d), lambda b, s: (b * nblk + s, 0)),
        out_shape=jax.ShapeDtypeStruct((t, d), BF16),
        scratch_shapes=[pltpu.VMEM((rows + CARRY_ROWS, d), F32), pltpu.VMEM((rows, d), F32)],
        compiler_params=_cparams(("parallel", "arbitrary")),
        name="shortconv_mixer",
    )(proj, proj, proj, conv_w.astype(F32), norm_w.astype(F32).reshape(1, d))


def _outproj_kernel(a1_ref, a2_ref, b1_ref, b2_ref, r_ref, o_ref):
    acc = jnp.dot(a1_ref[...], b1_ref[...], preferred_element_type=F32)
    acc = acc + jnp.dot(a2_ref[...], b2_ref[...], preferred_element_type=F32)
    o_ref[...] = acc + r_ref[...]


def out_projection(y_ssd, y_sc, w_out, x, *, tm=512, tn=512):
    m, kh = y_ssd.shape
    _, n = w_out.shape
    tm = min(tm, m)
    return pl.pallas_call(
        _outproj_kernel,
        grid=(m // tm, n // tn),
        in_specs=[
            pl.BlockSpec((tm, kh), lambda i, j: (i, 0)),
            pl.BlockSpec((tm, kh), lambda i, j: (i, 0)),
            pl.BlockSpec((kh, tn), lambda i, j: (0, j)),
            pl.BlockSpec((kh, tn), lambda i, j: (1, j)),
            pl.BlockSpec((tm, tn), lambda i, j: (i, j)),
        ],
        out_specs=pl.BlockSpec((tm, tn), lambda i, j: (i, j)),
        out_shape=jax.ShapeDtypeStruct((m, n), F32),
        compiler_params=_cparams(("parallel", "parallel")),
        name="out_projection",
    )(y_ssd, y_sc, w_out, w_out, x)


def _top_values(x, n):
    outs = []
    for r in range(n):
        m = jnp.max(x, axis=0, keepdims=True)
        outs.append(m)
        if r < n - 1:
            x = jnp.where(x >= m, -jnp.inf, x)
    return outs


def _query_kernel(x_ref, w_ref, o_ref):
    acc = jnp.dot(x_ref[...], w_ref[...], preferred_element_type=F32)
    for c in range(o_ref.shape[0]):
        o_ref[c] = acc[:, c * N_KEYS:(c + 1) * N_KEYS].astype(o_ref.dtype)


def query_projection(xn, wq, *, tm=1024, tn=1024):
    t, d = xn.shape
    n = wq.shape[1]
    tm = min(tm, t)
    per = tn // N_KEYS
    return pl.pallas_call(
        _query_kernel,
        grid=(t // tm, n // tn),
        in_specs=[pl.BlockSpec((tm, d), lambda i, j: (i, 0)), pl.BlockSpec((d, tn), lambda i, j: (0, j))],
        out_specs=pl.BlockSpec((per, tm, N_KEYS), lambda i, j: (j, i, 0)),
        out_shape=jax.ShapeDtypeStruct((n // N_KEYS, t, N_KEYS), BF16),
        compiler_params=_cparams(("parallel", "parallel")),
        name="query_projection",
    )(xn, wq)


def _route_kernel(q_ref, keys_ref, phi_ref, e0_ref, e1_ref):
    nt = (((1,), (1,)), ((), ()))

    def head(h, carry):
        s0 = lax.dot_general(keys_ref[2 * h], q_ref[2 * h], nt, preferred_element_type=F32)
        s1 = lax.dot_general(keys_ref[2 * h + 1], q_ref[2 * h + 1], nt, preferred_element_type=F32)
        top0 = _top_values(s0, TOPK + 1)
        top1 = _top_values(s1, TOPK + 1)
        t1 = jnp.concatenate(top1, axis=0)
        pieces = [top0[r] + t1[0:(TOPK + 1) // (r + 1)] for r in range(TOPK + 1)]
        n_cand = sum(p.shape[0] for p in pieces)
        pieces.append(jnp.full((-n_cand % 8, s0.shape[1]), -jnp.inf, F32))
        cand = jnp.concatenate(pieces, axis=0)
        best = _top_values(cand, TOPK + 1)
        m = best[0]
        z = jnp.exp(best[0] - m)
        for r in range(1, TOPK):
            z = z + jnp.exp(best[r] - m)
        inv_z = 1.0 / z
        tau = 0.5 * (best[TOPK - 1] + best[TOPK])
        m0, m1 = top0[0], top1[0]
        e0_ref[h] = jnp.exp(s0 - m0)
        e1_ref[h] = jnp.exp(s1 - m1) * inv_z
        phi_ref[h] = jnp.exp((tau - m1) - s0) * inv_z
        return carry

    lax.fori_loop(0, PEER_HEADS, head, 0)


def peer_routing(q, keys, *, tb=512):
    _, t, _ = q.shape
    shp = jax.ShapeDtypeStruct((PEER_HEADS, N_KEYS, t), F32)
    ospec = pl.BlockSpec((PEER_HEADS, N_KEYS, tb), lambda i: (0, 0, i))
    return pl.pallas_call(
        _route_kernel,
        grid=(t // tb,),
        in_specs=[pl.BlockSpec((2 * PEER_HEADS, tb, N_KEYS), lambda i: (0, i, 0)),
                  pl.BlockSpec((2 * PEER_HEADS, N_KEYS, N_KEYS), lambda i: (0, 0, 0))],
        out_specs=[ospec, ospec, ospec],
        out_shape=[shp, shp, shp],
        compiler_params=_cparams(("parallel",)),
        name="peer_routing",
    )(q, keys)


def _gelu_exact(x):
    return 0.5 * x * (1.0 + lax.erf(x * (1.0 / math.sqrt(2.0))))


def _peer_dense_kernel(x_ref, u_ref, v_ref, phi_ref, e0_ref, e1_ref, o_ref, a_ref, acc_ref, *, te):
    et = pl.program_id(1)
    nblk = te // N_KEYS
    tb = x_ref.shape[0]
    tq = 128
    nt = (((1,), (1,)), ((), ()))

    @pl.when(et == 0)
    def _():
        acc_ref[...] = jnp.zeros_like(acc_ref)

    ht = lax.dot_general(u_ref[...], x_ref[...], nt, preferred_element_type=F32)
    for j in range(nblk):
        i = et * nblk + j
        rows = slice(j * N_KEYS, (j + 1) * N_KEYS)
        phi_rows = [phi_ref[h, pl.ds(i, 1), :] for h in range(PEER_HEADS)]
        e0_rows = [e0_ref[h, pl.ds(i, 1), :] for h in range(PEER_HEADS)]
        for q in range(tb // tq):
            ts = slice(q * tq, (q + 1) * tq)
            gate = None
            for h in range(PEER_HEADS):
                e1 = e1_ref[h, :, ts]
                sel = jnp.where(e1 >= phi_rows[h][:, ts], e1, 0.0) * e0_rows[h][:, ts]
                gate = sel if gate is None else gate + sel
            act = _gelu_exact(ht[rows, ts]) * gate
            a_ref[ts, rows] = act.T.astype(a_ref.dtype)
    acc_ref[...] += jnp.dot(a_ref[...], v_ref[...], preferred_element_type=F32)

    @pl.when(et == pl.num_programs(1) - 1)
    def _():
        o_ref[...] = acc_ref[...].astype(o_ref.dtype)


def peer_dense(xn, u, v, phi, e0, e1, *, tb=512, te=512):
    t, d = xn.shape
    e = u.shape[0]
    tb = min(tb, t)
    rspec = pl.BlockSpec((PEER_HEADS, N_KEYS, tb), lambda i, j: (0, 0, i))
    return pl.pallas_call(
        functools.partial(_peer_dense_kernel, te=te),
        grid=(t // tb, e // te),
        in_specs=[
            pl.BlockSpec((tb, d), lambda i, j: (i, 0)),
            pl.BlockSpec((te, d), lambda i, j: (j, 0)),
            pl.BlockSpec((te, d), lambda i, j: (j, 0)),
            rspec, rspec, rspec,
        ],
        out_specs=pl.BlockSpec((tb, d), lambda i, j: (i, 0)),
        out_shape=jax.ShapeDtypeStruct((t, d), BF16),
        scratch_shapes=[pltpu.VMEM((tb, te), BF16), pltpu.VMEM((tb, d), F32)],
        compiler_params=_cparams(("parallel", "arbitrary")),
        name="peer_dense",
    )(xn, u, v, phi, e0, e1)


def _final_kernel(h_ref, p_ref, w_ref, o_ref):
    x = h_ref[...] + p_ref[...]
    ms = jnp.mean(x * x, axis=-1, keepdims=True)
    o_ref[...] = x * lax.rsqrt(ms + RMS_EPS) * w_ref[...]


def final_norm(h1, peer, w, *, tm=256):
    t, d = h1.shape
    return pl.pallas_call(
        _final_kernel,
        grid=(t // tm,),
        in_specs=[pl.BlockSpec((tm, d), lambda i: (i, 0)), pl.BlockSpec((tm, d), lambda i: (i, 0)),
                  pl.BlockSpec((1, d), lambda i: (0, 0))],
        out_specs=pl.BlockSpec((tm, d), lambda i: (i, 0)),
        out_shape=jax.ShapeDtypeStruct((t, d), F32),
        compiler_params=_cparams(("parallel",)),
        name="final_norm",
    )(h1, peer, w.reshape(1, d))


def kernel(x, w_in, ssd_conv_w, ssd_conv_b, ssd_dt_bias, ssd_a_log, ssd_d, ssd_norm_w, sc_conv_w, sc_norm_w, w_out,
           norm_mix_w, norm_ffn_w, peer_w_query, peer_sub_keys, peer_u, peer_v, norm_final_w):
    batch, seq, d = x.shape
    t = batch * seq
    h = x.reshape(t, d)
    assert w_in.shape[0] == 1, "single-layer block"
    o_c = 2 * D_SSD + N_GROUPS * D_STATE
    o2 = o_c + N_GROUPS * D_STATE
    o3 = o2 + N_HEADS
    ssd_off = {"z": 0, "xs": D_SSD, "B": 2 * D_SSD, "C": o_c}
    sc_off = {"scb": 0, "scc": D_CONV, "sch": 2 * D_CONV}
    wt = w_in[0].T.astype(BF16)
    w_sc = wt[o3:]
    w_out_b = w_out[0].astype(BF16)
    wq_b = peer_w_query[0].astype(BF16)
    keys = peer_sub_keys[0].reshape(2 * PEER_HEADS, N_KEYS, N_KEYS).astype(BF16)
    u_b = peer_u[0].astype(BF16)
    v_b = peer_v[0].astype(BF16)

    xn = rmsnorm_rows(h, norm_mix_w[0])
    proj_ssd = matmul(xn, wt, tm=1024, tn=1024, tk=d, out_dtype=BF16, b_transposed=True, n=o2,
                      name="in_projection_ssd")
    proj_sc = matmul(xn, w_sc, tm=1024, tn=1024, tk=d, out_dtype=BF16, b_transposed=True, name="in_projection_sc")
    dt_raw = matmul(xn, wt, tm=1024, tn=128, tk=d, out_dtype=F32, b_transposed=True, n=128, n_start=o2,
                    name="dt_projection")
    y_ssd = ssd_mixer(proj_ssd, dt_raw, ssd_conv_w[0], ssd_conv_b[0], ssd_dt_bias[0], ssd_a_log[0], ssd_d[0],
                      ssd_norm_w[0], batch=batch, seq=seq, col_off=ssd_off)
    y_sc = shortconv_mixer(proj_sc, sc_conv_w[0], sc_norm_w[0], batch=batch, seq=seq, col_off=sc_off)
    h = out_projection(y_ssd, y_sc, w_out_b, h)

    xn2 = rmsnorm_rows(h, norm_ffn_w[0])
    q = query_projection(xn2, wq_b)
    phi, e0, e1 = peer_routing(q, keys)
    peer = peer_dense(xn2, u_b, v_b, phi, e0, e1)
    out = final_norm(h, peer, norm_final_w)
    return out.reshape(batch, seq, d)
```

```python
import functools
import math

import jax
import jax.numpy as jnp
from jax import lax
from jax.experimental import pallas as pl
from jax.experimental.pallas import tpu as pltpu

F32 = jnp.float32
BF16 = jnp.bfloat16

D_MODEL = 4096
D_SSD = 4096
D_CONV = 4096
HEAD_DIM = 64
N_HEADS = 64
N_GROUPS = 8
HEADS_PER_GROUP = 8
D_STATE = 128
CHUNK = 128
GROUP_W = HEADS_PER_GROUP * HEAD_DIM
PEER_HEADS = 8
N_KEYS = 128
N_EXPERTS = N_KEYS * N_KEYS
TOPK = 16
RMS_EPS = 1e-6

VMEM_LIMIT_BYTES = 58 * 1024 * 1024
CARRY_ROWS = 8


def _cparams(sem):
    return pltpu.CompilerParams(dimension_semantics=sem, vmem_limit_bytes=VMEM_LIMIT_BYTES)


def _rmsnorm_kernel(x_ref, w_ref, o_ref):
    x = x_ref[...]
    ms = jnp.mean(x * x, axis=-1, keepdims=True)
    o_ref[...] = (x * lax.rsqrt(ms + RMS_EPS) * w_ref[...]).astype(o_ref.dtype)


def rmsnorm_rows(x, w, *, tm=256):
    t, d = x.shape
    return pl.pallas_call(
        _rmsnorm_kernel,
        grid=(t // tm,),
        in_specs=[pl.BlockSpec((tm, d), lambda i: (i, 0)), pl.BlockSpec((1, d), lambda i: (0, 0))],
        out_specs=pl.BlockSpec((tm, d), lambda i: (i, 0)),
        out_shape=jax.ShapeDtypeStruct((t, d), BF16),
        compiler_params=_cparams(("parallel",)),
        name="rmsnorm_rows",
    )(x, w.reshape(1, d))


def _matmul_kernel(*refs, nk, has_res, b_transposed):
    if has_res:
        a_ref, b_ref, r_ref, o_ref = refs[:4]
    else:
        a_ref, b_ref, o_ref = refs[:3]
        r_ref = None
    dims = (((1,), (1,)), ((), ())) if b_transposed else (((1,), (0,)), ((), ()))
    if nk == 1:
        acc = lax.dot_general(a_ref[...], b_ref[...], dims, preferred_element_type=F32)
        if has_res:
            acc = acc + r_ref[...]
        o_ref[...] = acc.astype(o_ref.dtype)
        return
    acc_ref = refs[-1]
    k = pl.program_id(2)

    @pl.when(k == 0)
    def _():
        acc_ref[...] = jnp.zeros_like(acc_ref)

    acc_ref[...] += lax.dot_general(a_ref[...], b_ref[...], dims, preferred_element_type=F32)

    @pl.when(k == nk - 1)
    def _():
        acc = acc_ref[...]
        if has_res:
            acc = acc + r_ref[...]
        o_ref[...] = acc.astype(o_ref.dtype)


def matmul(a, b, *, tm, tn, tk, out_dtype, residual=None, b_transposed=False, n=None, n_start=0, name="matmul"):
    m, kdim = a.shape
    if n is None:
        n = b.shape[0] if b_transposed else b.shape[1]
    tm, tn, tk = min(tm, m), min(tn, n), min(tk, kdim)
    nk = kdim // tk
    has_res = residual is not None
    j0 = n_start // tn
    b_spec = (pl.BlockSpec((tn, tk), lambda i, j, k: (j0 + j, k)) if b_transposed
              else pl.BlockSpec((tk, tn), lambda i, j, k: (k, j0 + j)))
    in_specs = [pl.BlockSpec((tm, tk), lambda i, j, k: (i, k)), b_spec]
    args = [a, b]
    if has_res:
        in_specs.append(pl.BlockSpec((tm, tn), lambda i, j, k: (i, j)))
        args.append(residual)
    scratch = [pltpu.VMEM((tm, tn), F32)] if nk > 1 else []
    return pl.pallas_call(
        functools.partial(_matmul_kernel, nk=nk, has_res=has_res, b_transposed=b_transposed),
        grid=(m // tm, n // tn, nk),
        in_specs=in_specs,
        out_specs=pl.BlockSpec((tm, tn), lambda i, j, k: (i, j)),
        out_shape=jax.ShapeDtypeStruct((m, n), out_dtype),
        scratch_shapes=scratch,
        compiler_params=_cparams(("parallel", "parallel", "arbitrary")),
        name=name,
    )(*args)


def _silu(x):
    hx = 0.5 * x
    return hx + hx * jnp.tanh(hx)


def _softplus(x):
    return jnp.maximum(x, 0.0) + jnp.log(1.0 + jnp.exp(-jnp.abs(x)))


_P_CONV, _P_BIAS, _P_DSKIP, _P_NORM = 0, 4, 5, 6


def _ssd_kernel(z_ref, xs_ref, b_ref, c_ref, dt_ref, pg_ref, pdt_ref, o_ref, state_ref, ext_ref, *, chunks):
    g = pl.program_id(1)
    c = pl.program_id(2)
    L = CHUNK
    W = GROUP_W
    N = D_STATE
    XBC = W + 2 * N

    @pl.when(c == 0)
    def _():
        state_ref[...] = jnp.zeros_like(state_ref)
        ext_ref[0:CARRY_ROWS, :] = jnp.zeros((CARRY_ROWS, XBC), F32)

    row = lax.broadcasted_iota(jnp.int32, (L, L), 0)
    col = lax.broadcasted_iota(jnp.int32, (L, L), 1)
    causal = row >= col
    tril = causal.astype(BF16)
    er = lax.broadcasted_iota(jnp.int32, (128, W), 0)
    ec = lax.broadcasted_iota(jnp.int32, (128, W), 1)
    expand = (er == ec // HEAD_DIM).astype(BF16)

    def chunk(ci, carry):
        rs = pl.ds(pl.multiple_of(ci * L, L), L)
        ext_ref[CARRY_ROWS:CARRY_ROWS + L, 0:W] = xs_ref[rs, :].astype(F32)
        ext_ref[CARRY_ROWS:CARRY_ROWS + L, W:W + N] = b_ref[rs, :].astype(F32)
        ext_ref[CARRY_ROWS:CARRY_ROWS + L, W + N:XBC] = c_ref[rs, :].astype(F32)
        acc = ext_ref[CARRY_ROWS:CARRY_ROWS + L, :] * pg_ref[_P_CONV + 3:_P_CONV + 4, :]
        for k in range(3):
            acc = acc + ext_ref[CARRY_ROWS - 3 + k:CARRY_ROWS - 3 + k + L, :] * pg_ref[_P_CONV + k:_P_CONV + k + 1, :]
        xbc = _silu(acc + pg_ref[_P_BIAS:_P_BIAS + 1, :])
        ext_ref[0:CARRY_ROWS, :] = ext_ref[L:L + CARRY_ROWS, :]
        xs = xbc[:, 0:W]
        b_b = xbc[:, W:W + N].astype(BF16)
        c_b = xbc[:, W + N:XBC].astype(BF16)

        dt_all = _softplus(dt_ref[rs, :] + pdt_ref[0:1, :])
        da_all = dt_all * (-jnp.exp(pdt_ref[1:2, :]))
        shift = (128 - g * HEADS_PER_GROUP) % 128
        dt_g = pltpu.roll(dt_all, shift, 1)
        da_g = pltpu.roll(da_all, shift, 1)

        hi = da_g.astype(BF16)
        r1 = da_g - hi.astype(F32)
        mid = r1.astype(BF16)
        lo = (r1 - mid.astype(F32)).astype(BF16)
        acum = (jnp.dot(tril, hi, preferred_element_type=F32)
                + jnp.dot(tril, mid, preferred_element_type=F32)
                + jnp.dot(tril, lo, preferred_element_type=F32))
        acum_t = acum.T
        a_last = acum[L - 1:L, :]

        def expand_heads(v):
            vh = v.astype(BF16)
            vl = (v - vh.astype(F32)).astype(BF16)
            return jnp.dot(vh, expand, preferred_element_type=F32) + jnp.dot(vl, expand, preferred_element_type=F32)

        dt_x = expand_heads(dt_g)
        ea_x = expand_heads(jnp.exp(acum))
        to_end_x = expand_heads(jnp.exp(a_last - acum))

        xdt = xs * dt_x
        xdt_b = xdt.astype(BF16)

        cb = lax.dot_general(c_b, b_b, (((1,), (1,)), ((), ())), preferred_element_type=F32)
        state = state_ref[...]
        y_off = jnp.dot(c_b, state.astype(BF16), preferred_element_type=F32) * ea_x

        y_parts = []
        for r in range(HEADS_PER_GROUP):
            seg = acum[:, r:r + 1] - acum_t[r:r + 1, :]
            decay = jnp.exp(jnp.where(causal, seg, -jnp.inf))
            m = (cb * decay).astype(BF16)
            y_parts.append(jnp.dot(m, xdt_b[:, r * HEAD_DIM:(r + 1) * HEAD_DIM], preferred_element_type=F32))
        y = jnp.concatenate(y_parts, axis=1) + y_off + pg_ref[_P_DSKIP:_P_DSKIP + 1, 0:W] * xs

        upd = lax.dot_general(b_b, (xdt * to_end_x).astype(BF16), (((0,), (0,)), ((), ())),
                              preferred_element_type=F32)
        state_ref[...] = state * ea_x[L - 1:L, :] + upd

        y = y * _silu(z_ref[rs, :].astype(F32))
        ms = jnp.mean(y * y, axis=-1, keepdims=True)
        o_ref[rs, :] = (y * lax.rsqrt(ms + RMS_EPS) * pg_ref[_P_NORM:_P_NORM + 1, 0:W]).astype(o_ref.dtype)
        return carry

    lax.fori_loop(0, chunks, chunk, 0)


def ssd_mixer(proj, dt_raw, conv_w, conv_b, dt_bias, a_log, d_skip, norm_w, *, batch, seq, col_off, chunks=8):
    t = batch * seq
    chunks = min(chunks, seq // CHUNK)
    nchunk = seq // (CHUNK * chunks)
    W, N, L, G = GROUP_W, D_STATE, CHUNK * chunks, N_GROUPS
    zb, xb, bb, cb_ = col_off["z"] // W, col_off["xs"] // W, col_off["B"] // N, col_off["C"] // N

    def rows(b, g, c):
        return b * nchunk + c

    def per_group(v):
        r = v.shape[0]
        x_part = v[:, :D_SSD].reshape(r, G, W)
        b_part = v[:, D_SSD:D_SSD + G * N].reshape(r, G, N)
        c_part = v[:, D_SSD + G * N:].reshape(r, G, N)
        return jnp.concatenate([x_part, b_part, c_part], axis=2).transpose(1, 0, 2)

    def x_only(v):
        return jnp.pad(v.astype(F32).reshape(G, 1, W), ((0, 0), (0, 0), (0, 2 * N)))

    pg = jnp.concatenate([per_group(conv_w.astype(F32)), per_group(conv_b.astype(F32).reshape(1, -1)),
                          x_only(jnp.repeat(d_skip, HEAD_DIM)), x_only(norm_w),
                          jnp.zeros((G, 1, W + 2 * N), F32)], axis=1)
    pad = 128 - N_HEADS
    pdt = jnp.pad(jnp.stack([dt_bias.astype(F32), a_log.astype(F32)]), ((0, 6), (0, pad)))

    in_specs = [
        pl.BlockSpec((L, W), lambda b, g, c: (rows(b, g, c), zb + g)),
        pl.BlockSpec((L, W), lambda b, g, c: (rows(b, g, c), xb + g)),
        pl.BlockSpec((L, N), lambda b, g, c: (rows(b, g, c), bb + g)),
        pl.BlockSpec((L, N), lambda b, g, c: (rows(b, g, c), cb_ + g)),
        pl.BlockSpec((L, 128), lambda b, g, c: (rows(b, g, c), 0)),
        pl.BlockSpec((None, 8, W + 2 * N), lambda b, g, c: (g, 0, 0)),
        pl.BlockSpec((8, 128), lambda b, g, c: (0, 0)),
    ]
    return pl.pallas_call(
        functools.partial(_ssd_kernel, chunks=chunks),
        grid=(batch, N_GROUPS, nchunk),
        in_specs=in_specs,
        out_specs=pl.BlockSpec((L, W), lambda b, g, c: (rows(b, g, c), g)),
        out_shape=jax.ShapeDtypeStruct((t, D_SSD), BF16),
        scratch_shapes=[pltpu.VMEM((N, W), F32), pltpu.VMEM((CHUNK + CARRY_ROWS, W + 2 * N), F32)],
        compiler_params=_cparams(("parallel", "parallel", "arbitrary")),
        name="ssd_mixer",
    )(proj, proj, proj, proj, dt_raw, pg, pdt)


def _shortconv_kernel(b_ref, c_ref, h_ref, w_ref, nw_ref, o_ref, ext_ref, y_ref, *, rows, col_chunk):
    s = pl.program_id(1)
    d = D_CONV

    @pl.when(s == 0)
    def _():
        ext_ref[0:CARRY_ROWS, :] = jnp.zeros((CARRY_ROWS, d), F32)

    ss = jnp.zeros((rows, 1), F32)
    for j in range(0, d, col_chunk):
        sl = slice(j, j + col_chunk)
        ext_ref[CARRY_ROWS:CARRY_ROWS + rows, sl] = c_ref[:, sl].astype(F32) * h_ref[:, sl].astype(F32)
        acc = ext_ref[CARRY_ROWS:CARRY_ROWS + rows, sl] * w_ref[2:3, sl]
        for k in range(2):
            acc = acc + ext_ref[CARRY_ROWS - 2 + k:CARRY_ROWS - 2 + k + rows, sl] * w_ref[k:k + 1, sl]
        y = b_ref[:, sl].astype(F32) * acc
        y_ref[:, sl] = y
        ss = ss + jnp.sum(y * y, axis=-1, keepdims=True)
    ext_ref[0:CARRY_ROWS, :] = ext_ref[rows:rows + CARRY_ROWS, :]
    inv = lax.rsqrt(ss * (1.0 / d) + RMS_EPS)
    for j in range(0, d, col_chunk):
        sl = slice(j, j + col_chunk)
        o_ref[:, sl] = (y_ref[:, sl] * inv * nw_ref[:, sl]).astype(o_ref.dtype)


def shortconv_mixer(proj, conv_w, norm_w, *, batch, seq, col_off, rows=256, col_chunk=512):
    t = batch * seq
    nblk = seq // rows
    d = D_CONV
    bb, cb_, hb = col_off["scb"] // d, col_off["scc"] // d, col_off["sch"] // d
    return pl.pallas_call(
        functools.partial(_shortconv_kernel, rows=rows, col_chunk=col_chunk),
        grid=(batch, nblk),
        in_specs=[
            pl.BlockSpec((rows, d), lambda b, s: (b * nblk + s, bb)),
            pl.BlockSpec((rows, d), lambda b, s: (b * nblk + s, cb_)),
            pl.BlockSpec((rows, d), lambda b, s: (b * nblk + s, hb)),
            pl.BlockSpec((3, d), lambda b, s: (0, 0)),
            pl.BlockSpec((1, d), lambda b, s: (0, 0)),
        ],
        out_specs=pl.BlockSpec((rows, d), lambda b, s: (b * nblk + s, 0)),
        out_shape=jax.ShapeDtypeStruct((t, d), BF16),
        scratch_shapes=[pltpu.VMEM((rows + CARRY_ROWS, d), F32), pltpu.VMEM((rows, d), F32)],
        compiler_params=_cparams(("parallel", "arbitrary")),
        name="shortconv_mixer",
    )(proj, proj, proj, conv_w.astype(F32), norm_w.astype(F32).reshape(1, d))


def _outproj_kernel(a1_ref, a2_ref, b1_ref, b2_ref, r_ref, u_ref, v_ref, o_ref, ub_ref, vb_ref):
    acc = jnp.dot(a1_ref[...], b1_ref[...], preferred_element_type=F32)
    acc = acc + jnp.dot(a2_ref[...], b2_ref[...], preferred_element_type=F32)
    o_ref[...] = acc + r_ref[...]
    ub_ref[...] = u_ref[...].astype(ub_ref.dtype)
    vb_ref[...] = v_ref[...].astype(vb_ref.dtype)


def out_projection(y_ssd, y_sc, w_out, x, u, v, *, tm=512, tn=512):
    m, kh = y_ssd.shape
    _, n = w_out.shape
    e, d = u.shape
    tm = min(tm, m)
    nj = n // tn
    steps = (m // tm) * nj
    er = e // steps
    assert er * steps == e and er % 16 == 0, "PEER table rows must split evenly over the grid steps"
    tspec = pl.BlockSpec((er, d), lambda i, j: (i * nj + j, 0))
    return pl.pallas_call(
        _outproj_kernel,
        grid=(m // tm, nj),
        in_specs=[
            pl.BlockSpec((tm, kh), lambda i, j: (i, 0)),
            pl.BlockSpec((tm, kh), lambda i, j: (i, 0)),
            pl.BlockSpec((kh, tn), lambda i, j: (0, j)),
            pl.BlockSpec((kh, tn), lambda i, j: (1, j)),
            pl.BlockSpec((tm, tn), lambda i, j: (i, j)),
            tspec, tspec,
        ],
        out_specs=[pl.BlockSpec((tm, tn), lambda i, j: (i, j)), tspec, tspec],
        out_shape=[jax.ShapeDtypeStruct((m, n), F32), jax.ShapeDtypeStruct((e, d), BF16),
                   jax.ShapeDtypeStruct((e, d), BF16)],
        compiler_params=_cparams(("parallel", "parallel")),
        name="out_projection",
    )(y_ssd, y_sc, w_out, w_out, x, u, v)


def _top_values(x, n):
    outs = []
    for r in range(n):
        m = jnp.max(x, axis=0, keepdims=True)
        outs.append(m)
        if r < n - 1:
            x = jnp.where(x >= m, -jnp.inf, x)
    return outs


def _query_kernel(x_ref, w_ref, o_ref):
    acc = jnp.dot(x_ref[...], w_ref[...], preferred_element_type=F32)
    for c in range(o_ref.shape[0]):
        o_ref[c] = acc[:, c * N_KEYS:(c + 1) * N_KEYS].astype(o_ref.dtype)


def query_projection(xn, wq, *, tm=1024, tn=1024):
    t, d = xn.shape
    n = wq.shape[1]
    tm = min(tm, t)
    per = tn // N_KEYS
    return pl.pallas_call(
        _query_kernel,
        grid=(t // tm, n // tn),
        in_specs=[pl.BlockSpec((tm, d), lambda i, j: (i, 0)), pl.BlockSpec((d, tn), lambda i, j: (0, j))],
        out_specs=pl.BlockSpec((per, tm, N_KEYS), lambda i, j: (j, i, 0)),
        out_shape=jax.ShapeDtypeStruct((n // N_KEYS, t, N_KEYS), BF16),
        compiler_params=_cparams(("parallel", "parallel")),
        name="query_projection",
    )(xn, wq)


_R_PHI, _R_E0, _R_E1 = 0, 1, 2


def _route_kernel(q_ref, keys_ref, r_ref):
    nt = (((1,), (1,)), ((), ()))

    def head(h, carry):
        s0 = lax.dot_general(keys_ref[2 * h], q_ref[2 * h], nt, preferred_element_type=F32)
        s1 = lax.dot_general(keys_ref[2 * h + 1], q_ref[2 * h + 1], nt, preferred_element_type=F32)
        top0 = _top_values(s0, TOPK + 1)
        top1 = _top_values(s1, TOPK + 1)
        t1 = jnp.concatenate(top1, axis=0)
        pieces = [top0[r] + t1[0:(TOPK + 1) // (r + 1)] for r in range(TOPK + 1)]
        n_cand = sum(p.shape[0] for p in pieces)
        pieces.append(jnp.full((-n_cand % 8, s0.shape[1]), -jnp.inf, F32))
        cand = jnp.concatenate(pieces, axis=0)
        best = _top_values(cand, TOPK + 1)
        m = best[0]
        z = jnp.exp(best[0] - m)
        for r in range(1, TOPK):
            z = z + jnp.exp(best[r] - m)
        inv_z = 1.0 / z
        tau = 0.5 * (best[TOPK - 1] + best[TOPK])
        m0, m1 = top0[0], top1[0]
        r_ref[_R_E0, h] = jnp.exp(s0 - m0)
        r_ref[_R_E1, h] = jnp.exp(s1 - m1) * inv_z
        r_ref[_R_PHI, h] = jnp.exp((tau - m1) - s0) * inv_z
        return carry

    lax.fori_loop(0, PEER_HEADS, head, 0)


def peer_routing(q, keys, *, tb=512):
    _, t, _ = q.shape
    return pl.pallas_call(
        _route_kernel,
        grid=(t // tb,),
        in_specs=[pl.BlockSpec((2 * PEER_HEADS, tb, N_KEYS), lambda i: (0, i, 0)),
                  pl.BlockSpec((2 * PEER_HEADS, N_KEYS, N_KEYS), lambda i: (0, 0, 0))],
        out_specs=pl.BlockSpec((3, PEER_HEADS, N_KEYS, tb), lambda i: (0, 0, 0, i)),
        out_shape=jax.ShapeDtypeStruct((3, PEER_HEADS, N_KEYS, t), F32),
        compiler_params=_cparams(("parallel",)),
        name="peer_routing",
    )(q, keys)


def _gelu_exact(x):
    return 0.5 * x * (1.0 + lax.erf(x * (1.0 / math.sqrt(2.0))))


def _peer_dense_kernel(x_ref, u_ref, v_ref, r_ref, o_ref, a_ref, acc_ref, *, te):
    et = pl.program_id(1)
    nblk = te // N_KEYS
    tb = x_ref.shape[0]
    tq = 128
    nt = (((1,), (1,)), ((), ()))

    @pl.when(et == 0)
    def _():
        acc_ref[...] = jnp.zeros_like(acc_ref)

    ht = lax.dot_general(u_ref[...], x_ref[...], nt, preferred_element_type=F32)
    for j in range(nblk):
        i = et * nblk + j
        rows = slice(j * N_KEYS, (j + 1) * N_KEYS)
        phi_rows = [r_ref[_R_PHI, h, pl.ds(i, 1), :] for h in range(PEER_HEADS)]
        e0_rows = [r_ref[_R_E0, h, pl.ds(i, 1), :] for h in range(PEER_HEADS)]
        for q in range(tb // tq):
            ts = slice(q * tq, (q + 1) * tq)
            gate = None
            for h in range(PEER_HEADS):
                e1 = r_ref[_R_E1, h, :, ts]
                sel = jnp.where(e1 >= phi_rows[h][:, ts], e1, 0.0) * e0_rows[h][:, ts]
                gate = sel if gate is None else gate + sel
            act = _gelu_exact(ht[rows, ts]) * gate
            a_ref[ts, rows] = act.T.astype(a_ref.dtype)
    acc_ref[...] += jnp.dot(a_ref[...], v_ref[...], preferred_element_type=F32)

    @pl.when(et == pl.num_programs(1) - 1)
    def _():
        o_ref[...] = acc_ref[...].astype(o_ref.dtype)


def peer_dense(xn, u, v, r, *, tb=512, te=512):
    t, d = xn.shape
    e = u.shape[0]
    tb = min(tb, t)
    return pl.pallas_call(
        functools.partial(_peer_dense_kernel, te=te),
        grid=(t // tb, e // te),
        in_specs=[
            pl.BlockSpec((tb, d), lambda i, j: (i, 0)),
            pl.BlockSpec((te, d), lambda i, j: (j, 0)),
            pl.BlockSpec((te, d), lambda i, j: (j, 0)),
            pl.BlockSpec((3, PEER_HEADS, N_KEYS, tb), lambda i, j: (0, 0, 0, i)),
        ],
        out_specs=pl.BlockSpec((tb, d), lambda i, j: (i, 0)),
        out_shape=jax.ShapeDtypeStruct((t, d), BF16),
        scratch_shapes=[pltpu.VMEM((tb, te), BF16), pltpu.VMEM((tb, d), F32)],
        compiler_params=_cparams(("parallel", "arbitrary")),
        name="peer_dense",
    )(xn, u, v, r)


def _final_kernel(h_ref, p_ref, w_ref, o_ref):
    x = h_ref[...] + p_ref[...]
    ms = jnp.mean(x * x, axis=-1, keepdims=True)
    o_ref[...] = x * lax.rsqrt(ms + RMS_EPS) * w_ref[...]


def final_norm(h1, peer, w, *, tm=256):
    t, d = h1.shape
    return pl.pallas_call(
        _final_kernel,
        grid=(t // tm,),
        in_specs=[pl.BlockSpec((tm, d), lambda i: (i, 0)), pl.BlockSpec((tm, d), lambda i: (i, 0)),
                  pl.BlockSpec((1, d), lambda i: (0, 0))],
        out_specs=pl.BlockSpec((tm, d), lambda i: (i, 0)),
        out_shape=jax.ShapeDtypeStruct((t, d), F32),
        compiler_params=_cparams(("parallel",)),
        name="final_norm",
    )(h1, peer, w.reshape(1, d))


def kernel(x, w_in, ssd_conv_w, ssd_conv_b, ssd_dt_bias, ssd_a_log, ssd_d, ssd_norm_w, sc_conv_w, sc_norm_w, w_out,
           norm_mix_w, norm_ffn_w, peer_w_query, peer_sub_keys, peer_u, peer_v, norm_final_w):
    batch, seq, d = x.shape
    t = batch * seq
    h = x.reshape(t, d)
    assert w_in.shape[0] == 1, "single-layer block"
    o_c = 2 * D_SSD + N_GROUPS * D_STATE
    o2 = o_c + N_GROUPS * D_STATE
    o3 = o2 + N_HEADS
    ssd_off = {"z": 0, "xs": D_SSD, "B": 2 * D_SSD, "C": o_c}
    sc_off = {"scb": 0, "scc": D_CONV, "sch": 2 * D_CONV}
    wt = w_in[0].T.astype(BF16)
    w_sc = wt[o3:]
    w_out_b = w_out[0].astype(BF16)
    wq_b = peer_w_query[0].astype(BF16)
    keys = peer_sub_keys[0].reshape(2 * PEER_HEADS, N_KEYS, N_KEYS).astype(BF16)

    xn = rmsnorm_rows(h, norm_mix_w[0])
    proj_ssd = matmul(xn, wt, tm=1024, tn=1024, tk=d, out_dtype=BF16, b_transposed=True, n=o2,
                      name="in_projection_ssd")
    proj_sc = matmul(xn, w_sc, tm=1024, tn=1024, tk=d, out_dtype=BF16, b_transposed=True, name="in_projection_sc")
    dt_raw = matmul(xn, wt, tm=1024, tn=128, tk=d, out_dtype=F32, b_transposed=True, n=128, n_start=o2,
                    name="dt_projection")
    y_ssd = ssd_mixer(proj_ssd, dt_raw, ssd_conv_w[0], ssd_conv_b[0], ssd_dt_bias[0], ssd_a_log[0], ssd_d[0],
                      ssd_norm_w[0], batch=batch, seq=seq, col_off=ssd_off)
    y_sc = shortconv_mixer(proj_sc, sc_conv_w[0], sc_norm_w[0], batch=batch, seq=seq, col_off=sc_off)
    h, u_b, v_b = out_projection(y_ssd, y_sc, w_out_b, h, peer_u[0], peer_v[0])

    xn2 = rmsnorm_rows(h, norm_ffn_w[0])
    q = query_projection(xn2, wq_b)
    r = peer_routing(q, keys)
    peer = peer_dense(xn2, u_b, v_b, r)
    out = final_norm(h, peer, norm_final_w)
    return out.reshape(batch, seq, d)
```

```python
import functools
import math

import jax
import jax.numpy as jnp
from jax import lax
from jax.experimental import pallas as pl
from jax.experimental.pallas import tpu as pltpu

F32 = jnp.float32
BF16 = jnp.bfloat16

D_MODEL = 4096
D_SSD = 4096
D_CONV = 4096
HEAD_DIM = 64
N_HEADS = 64
N_GROUPS = 8
HEADS_PER_GROUP = 8
D_STATE = 128
CHUNK = 128
GROUP_W = HEADS_PER_GROUP * HEAD_DIM
PEER_HEADS = 8
N_KEYS = 128
N_EXPERTS = N_KEYS * N_KEYS
TOPK = 16
RMS_EPS = 1e-6

VMEM_LIMIT_BYTES = 58 * 1024 * 1024
CARRY_ROWS = 8


def _cparams(sem):
    return pltpu.CompilerParams(dimension_semantics=sem, vmem_limit_bytes=VMEM_LIMIT_BYTES)


_NT = (((1,), (1,)), ((), ()))


def _rms_bf16(x, w):
    ms = jnp.mean(x * x, axis=-1, keepdims=True)
    return (x * lax.rsqrt(ms + RMS_EPS) * w).astype(BF16)


def _proj_kernel(a_ref, b_ref, o_ref):
    o_ref[...] = lax.dot_general(a_ref[...], b_ref[...], _NT, preferred_element_type=F32).astype(o_ref.dtype)


def projection(a, w_t, *, n, n_start=0, tm, tn, out_dtype, name):
    m, kdim = a.shape
    tm = min(tm, m)
    j0 = n_start // tn
    return pl.pallas_call(
        _proj_kernel,
        grid=(m // tm, n // tn),
        in_specs=[pl.BlockSpec((tm, kdim), lambda i, j: (i, 0)), pl.BlockSpec((tn, kdim), lambda i, j: (j0 + j, 0))],
        out_specs=pl.BlockSpec((tm, tn), lambda i, j: (i, j)),
        out_shape=jax.ShapeDtypeStruct((m, n), out_dtype),
        compiler_params=_cparams(("parallel", "parallel")),
        name=name,
    )(a, w_t)


def _norm_proj_kernel(x_ref, nw_ref, b_ref, o_ref, xn_ref):
    @pl.when(pl.program_id(1) == 0)
    def _():
        xn_ref[...] = _rms_bf16(x_ref[...], nw_ref[...])

    o_ref[...] = lax.dot_general(xn_ref[...], b_ref[...], _NT, preferred_element_type=F32).astype(o_ref.dtype)


def norm_projection(x, norm_w, w_t, *, n, tm, tn, out_dtype, name):
    m, kdim = x.shape
    tm = min(tm, m)
    return pl.pallas_call(
        _norm_proj_kernel,
        grid=(m // tm, n // tn),
        in_specs=[pl.BlockSpec((tm, kdim), lambda i, j: (i, 0)), pl.BlockSpec((1, kdim), lambda i, j: (0, 0)),
                  pl.BlockSpec((tn, kdim), lambda i, j: (j, 0))],
        out_specs=[pl.BlockSpec((tm, tn), lambda i, j: (i, j)), pl.BlockSpec((tm, kdim), lambda i, j: (i, 0))],
        out_shape=[jax.ShapeDtypeStruct((m, n), out_dtype), jax.ShapeDtypeStruct((m, kdim), BF16)],
        compiler_params=_cparams(("parallel", "arbitrary")),
        name=name,
    )(x, norm_w.reshape(1, kdim), w_t)


def _silu(x):
    hx = 0.5 * x
    return hx + hx * jnp.tanh(hx)


def _softplus(x):
    return jnp.maximum(x, 0.0) + jnp.log(1.0 + jnp.exp(-jnp.abs(x)))


_P_CONV, _P_BIAS, _P_DSKIP, _P_NORM = 0, 4, 5, 6


def _ssd_kernel(z_ref, xs_ref, b_ref, c_ref, dt_ref, pg_ref, pdt_ref, o_ref, state_ref, ext_ref, *, chunks):
    g = pl.program_id(1)
    c = pl.program_id(2)
    L = CHUNK
    W = GROUP_W
    N = D_STATE
    XBC = W + 2 * N

    @pl.when(c == 0)
    def _():
        state_ref[...] = jnp.zeros_like(state_ref)
        ext_ref[0:CARRY_ROWS, :] = jnp.zeros((CARRY_ROWS, XBC), F32)

    row = lax.broadcasted_iota(jnp.int32, (L, L), 0)
    col = lax.broadcasted_iota(jnp.int32, (L, L), 1)
    causal = row >= col
    tril = causal.astype(BF16)
    er = lax.broadcasted_iota(jnp.int32, (128, W), 0)
    ec = lax.broadcasted_iota(jnp.int32, (128, W), 1)
    expand = (er == ec // HEAD_DIM).astype(BF16)

    def chunk(ci, carry):
        rs = pl.ds(pl.multiple_of(ci * L, L), L)
        ext_ref[CARRY_ROWS:CARRY_ROWS + L, 0:W] = xs_ref[rs, :].astype(F32)
        ext_ref[CARRY_ROWS:CARRY_ROWS + L, W:W + N] = b_ref[rs, :].astype(F32)
        ext_ref[CARRY_ROWS:CARRY_ROWS + L, W + N:XBC] = c_ref[rs, :].astype(F32)
        acc = ext_ref[CARRY_ROWS:CARRY_ROWS + L, :] * pg_ref[_P_CONV + 3:_P_CONV + 4, :]
        for k in range(3):
            acc = acc + ext_ref[CARRY_ROWS - 3 + k:CARRY_ROWS - 3 + k + L, :] * pg_ref[_P_CONV + k:_P_CONV + k + 1, :]
        xbc = _silu(acc + pg_ref[_P_BIAS:_P_BIAS + 1, :])
        ext_ref[0:CARRY_ROWS, :] = ext_ref[L:L + CARRY_ROWS, :]
        xs = xbc[:, 0:W]
        b_b = xbc[:, W:W + N].astype(BF16)
        c_b = xbc[:, W + N:XBC].astype(BF16)

        dt_all = _softplus(dt_ref[rs, :] + pdt_ref[0:1, :])
        da_all = dt_all * (-jnp.exp(pdt_ref[1:2, :]))
        shift = (128 - g * HEADS_PER_GROUP) % 128
        dt_g = pltpu.roll(dt_all, shift, 1)
        da_g = pltpu.roll(da_all, shift, 1)

        hi = da_g.astype(BF16)
        r1 = da_g - hi.astype(F32)
        mid = r1.astype(BF16)
        lo = (r1 - mid.astype(F32)).astype(BF16)
        acum = (jnp.dot(tril, hi, preferred_element_type=F32)
                + jnp.dot(tril, mid, preferred_element_type=F32)
                + jnp.dot(tril, lo, preferred_element_type=F32))
        acum_t = acum.T
        a_last = acum[L - 1:L, :]

        def expand_heads(v):
            vh = v.astype(BF16)
            vl = (v - vh.astype(F32)).astype(BF16)
            return jnp.dot(vh, expand, preferred_element_type=F32) + jnp.dot(vl, expand, preferred_element_type=F32)

        dt_x = expand_heads(dt_g)
        ea_x = expand_heads(jnp.exp(acum))
        to_end_x = expand_heads(jnp.exp(a_last - acum))

        xdt = xs * dt_x
        xdt_b = xdt.astype(BF16)

        cb = lax.dot_general(c_b, b_b, (((1,), (1,)), ((), ())), preferred_element_type=F32)
        state = state_ref[...]
        y_off = jnp.dot(c_b, state.astype(BF16), preferred_element_type=F32) * ea_x

        y_parts = []
        for r in range(HEADS_PER_GROUP):
            seg = acum[:, r:r + 1] - acum_t[r:r + 1, :]
            decay = jnp.exp(jnp.where(causal, seg, -jnp.inf))
            m = (cb * decay).astype(BF16)
            y_parts.append(jnp.dot(m, xdt_b[:, r * HEAD_DIM:(r + 1) * HEAD_DIM], preferred_element_type=F32))
        y = jnp.concatenate(y_parts, axis=1) + y_off + pg_ref[_P_DSKIP:_P_DSKIP + 1, 0:W] * xs

        upd = lax.dot_general(b_b, (xdt * to_end_x).astype(BF16), (((0,), (0,)), ((), ())),
                              preferred_element_type=F32)
        state_ref[...] = state * ea_x[L - 1:L, :] + upd

        y = y * _silu(z_ref[rs, :].astype(F32))
        ms = jnp.mean(y * y, axis=-1, keepdims=True)
        o_ref[rs, :] = (y * lax.rsqrt(ms + RMS_EPS) * pg_ref[_P_NORM:_P_NORM + 1, 0:W]).astype(o_ref.dtype)
        return carry

    lax.fori_loop(0, chunks, chunk, 0)


def ssd_mixer(proj, dt_raw, conv_w, conv_b, dt_bias, a_log, d_skip, norm_w, *, batch, seq, col_off, chunks=8):
    t = batch * seq
    chunks = min(chunks, seq // CHUNK)
    nchunk = seq // (CHUNK * chunks)
    W, N, L, G = GROUP_W, D_STATE, CHUNK * chunks, N_GROUPS
    zb, xb, bb, cb_ = col_off["z"] // W, col_off["xs"] // W, col_off["B"] // N, col_off["C"] // N

    def rows(b, g, c):
        return b * nchunk + c

    def per_group(v):
        r = v.shape[0]
        x_part = v[:, :D_SSD].reshape(r, G, W)
        b_part = v[:, D_SSD:D_SSD + G * N].reshape(r, G, N)
        c_part = v[:, D_SSD + G * N:].reshape(r, G, N)
        return jnp.concatenate([x_part, b_part, c_part], axis=2).transpose(1, 0, 2)

    def x_only(v):
        return jnp.pad(v.astype(F32).reshape(G, 1, W), ((0, 0), (0, 0), (0, 2 * N)))

    pg = jnp.concatenate([per_group(conv_w.astype(F32)), per_group(conv_b.astype(F32).reshape(1, -1)),
                          x_only(jnp.repeat(d_skip, HEAD_DIM)), x_only(norm_w),
                          jnp.zeros((G, 1, W + 2 * N), F32)], axis=1)
    pad = 128 - N_HEADS
    pdt = jnp.pad(jnp.stack([dt_bias.astype(F32), a_log.astype(F32)]), ((0, 6), (0, pad)))

    in_specs = [
        pl.BlockSpec((L, W), lambda b, g, c: (rows(b, g, c), zb + g)),
        pl.BlockSpec((L, W), lambda b, g, c: (rows(b, g, c), xb + g)),
        pl.BlockSpec((L, N), lambda b, g, c: (rows(b, g, c), bb + g)),
        pl.BlockSpec((L, N), lambda b, g, c: (rows(b, g, c), cb_ + g)),
        pl.BlockSpec((L, 128), lambda b, g, c: (rows(b, g, c), 0)),
        pl.BlockSpec((None, 8, W + 2 * N), lambda b, g, c: (g, 0, 0)),
        pl.BlockSpec((8, 128), lambda b, g, c: (0, 0)),
    ]
    return pl.pallas_call(
        functools.partial(_ssd_kernel, chunks=chunks),
        grid=(batch, N_GROUPS, nchunk),
        in_specs=in_specs,
        out_specs=pl.BlockSpec((L, W), lambda b, g, c: (rows(b, g, c), g)),
        out_shape=jax.ShapeDtypeStruct((t, D_SSD), BF16),
        scratch_shapes=[pltpu.VMEM((N, W), F32), pltpu.VMEM((CHUNK + CARRY_ROWS, W + 2 * N), F32)],
        compiler_params=_cparams(("parallel", "parallel", "arbitrary")),
        name="ssd_mixer",
    )(proj, proj, proj, proj, dt_raw, pg, pdt)


def _shortconv_kernel(b_ref, c_ref, h_ref, w_ref, nw_ref, o_ref, ext_ref, y_ref, *, rows, col_chunk):
    s = pl.program_id(1)
    d = D_CONV

    @pl.when(s == 0)
    def _():
        ext_ref[0:CARRY_ROWS, :] = jnp.zeros((CARRY_ROWS, d), F32)

    ss = jnp.zeros((rows, 1), F32)
    for j in range(0, d, col_chunk):
        sl = slice(j, j + col_chunk)
        ext_ref[CARRY_ROWS:CARRY_ROWS + rows, sl] = c_ref[:, sl].astype(F32) * h_ref[:, sl].astype(F32)
        acc = ext_ref[CARRY_ROWS:CARRY_ROWS + rows, sl] * w_ref[2:3, sl]
        for k in range(2):
            acc = acc + ext_ref[CARRY_ROWS - 2 + k:CARRY_ROWS - 2 + k + rows, sl] * w_ref[k:k + 1, sl]
        y = b_ref[:, sl].astype(F32) * acc
        y_ref[:, sl] = y
        ss = ss + jnp.sum(y * y, axis=-1, keepdims=True)
    ext_ref[0:CARRY_ROWS, :] = ext_ref[rows:rows + CARRY_ROWS, :]
    inv = lax.rsqrt(ss * (1.0 / d) + RMS_EPS)
    for j in range(0, d, col_chunk):
        sl = slice(j, j + col_chunk)
        o_ref[:, sl] = (y_ref[:, sl] * inv * nw_ref[:, sl]).astype(o_ref.dtype)


def shortconv_mixer(proj, conv_w, norm_w, *, batch, seq, col_off, rows=256, col_chunk=512):
    t = batch * seq
    nblk = seq // rows
    d = D_CONV
    bb, cb_, hb = col_off["scb"] // d, col_off["scc"] // d, col_off["sch"] // d
    return pl.pallas_call(
        functools.partial(_shortconv_kernel, rows=rows, col_chunk=col_chunk),
        grid=(batch, nblk),
        in_specs=[
            pl.BlockSpec((rows, d), lambda b, s: (b * nblk + s, bb)),
            pl.BlockSpec((rows, d), lambda b, s: (b * nblk + s, cb_)),
            pl.BlockSpec((rows, d), lambda b, s: (b * nblk + s, hb)),
            pl.BlockSpec((3, d), lambda b, s: (0, 0)),
            pl.BlockSpec((1, d), lambda b, s: (0, 0)),
        ],
        out_specs=pl.BlockSpec((rows, d), lambda b, s: (b * nblk + s, 0)),
        out_shape=jax.ShapeDtypeStruct((t, d), BF16),
        scratch_shapes=[pltpu.VMEM((rows + CARRY_ROWS, d), F32), pltpu.VMEM((rows, d), F32)],
        compiler_params=_cparams(("parallel", "arbitrary")),
        name="shortconv_mixer",
    )(proj, proj, proj, conv_w.astype(F32), norm_w.astype(F32).reshape(1, d))


def _outproj_kernel(a1_ref, a2_ref, b1_ref, b2_ref, r_ref, u_ref, v_ref, o_ref, ub_ref, vb_ref):
    acc = jnp.dot(a1_ref[...], b1_ref[...], preferred_element_type=F32)
    acc = acc + jnp.dot(a2_ref[...], b2_ref[...], preferred_element_type=F32)
    o_ref[...] = acc + r_ref[...]
    ub_ref[...] = u_ref[...].astype(ub_ref.dtype)
    vb_ref[...] = v_ref[...].astype(vb_ref.dtype)


def out_projection(y_ssd, y_sc, w_out, x, u, v, *, tm=512, tn=512):
    m, kh = y_ssd.shape
    _, n = w_out.shape
    e, d = u.shape
    tm = min(tm, m)
    nj = n // tn
    steps = (m // tm) * nj
    er = e // steps
    assert er * steps == e and er % 16 == 0, "PEER table rows must split evenly over the grid steps"
    tspec = pl.BlockSpec((er, d), lambda i, j: (i * nj + j, 0))
    return pl.pallas_call(
        _outproj_kernel,
        grid=(m // tm, nj),
        in_specs=[
            pl.BlockSpec((tm, kh), lambda i, j: (i, 0)),
            pl.BlockSpec((tm, kh), lambda i, j: (i, 0)),
            pl.BlockSpec((kh, tn), lambda i, j: (0, j)),
            pl.BlockSpec((kh, tn), lambda i, j: (1, j)),
            pl.BlockSpec((tm, tn), lambda i, j: (i, j)),
            tspec, tspec,
        ],
        out_specs=[pl.BlockSpec((tm, tn), lambda i, j: (i, j)), tspec, tspec],
        out_shape=[jax.ShapeDtypeStruct((m, n), F32), jax.ShapeDtypeStruct((e, d), BF16),
                   jax.ShapeDtypeStruct((e, d), BF16)],
        compiler_params=_cparams(("parallel", "parallel")),
        name="out_projection",
    )(y_ssd, y_sc, w_out, w_out, x, u, v)


def _top_values(x, n):
    outs = []
    for r in range(n):
        m = jnp.max(x, axis=0, keepdims=True)
        outs.append(m)
        if r < n - 1:
            x = jnp.where(x >= m, -jnp.inf, x)
    return outs


def _norm_query_kernel(x_ref, nw_ref, w_ref, q_ref, xn_ref):
    @pl.when(pl.program_id(1) == 0)
    def _():
        xn_ref[...] = _rms_bf16(x_ref[...], nw_ref[...])

    acc = jnp.dot(xn_ref[...], w_ref[...], preferred_element_type=F32)
    for c in range(q_ref.shape[0]):
        q_ref[c] = acc[:, c * N_KEYS:(c + 1) * N_KEYS].astype(q_ref.dtype)


def norm_query_projection(x, norm_w, wq, *, tm=512, tn=1024):
    t, d = x.shape
    n = wq.shape[1]
    tm = min(tm, t)
    per = tn // N_KEYS
    return pl.pallas_call(
        _norm_query_kernel,
        grid=(t // tm, n // tn),
        in_specs=[pl.BlockSpec((tm, d), lambda i, j: (i, 0)), pl.BlockSpec((1, d), lambda i, j: (0, 0)),
                  pl.BlockSpec((d, tn), lambda i, j: (0, j))],
        out_specs=[pl.BlockSpec((per, tm, N_KEYS), lambda i, j: (j, i, 0)), pl.BlockSpec((tm, d), lambda i, j: (i, 0))],
        out_shape=[jax.ShapeDtypeStruct((n // N_KEYS, t, N_KEYS), BF16), jax.ShapeDtypeStruct((t, d), BF16)],
        compiler_params=_cparams(("parallel", "arbitrary")),
        name="query_projection",
    )(x, norm_w.reshape(1, d), wq)


_R_PHI, _R_E0, _R_E1 = 0, 1, 2


def _route_kernel(q_ref, keys_ref, r_ref):
    nt = (((1,), (1,)), ((), ()))

    def head(h, carry):
        s0 = lax.dot_general(keys_ref[2 * h], q_ref[2 * h], nt, preferred_element_type=F32)
        s1 = lax.dot_general(keys_ref[2 * h + 1], q_ref[2 * h + 1], nt, preferred_element_type=F32)
        top0 = _top_values(s0, TOPK + 1)
        top1 = _top_values(s1, TOPK + 1)
        t1 = jnp.concatenate(top1, axis=0)
        pieces = [top0[r] + t1[0:(TOPK + 1) // (r + 1)] for r in range(TOPK + 1)]
        n_cand = sum(p.shape[0] for p in pieces)
        pieces.append(jnp.full((-n_cand % 8, s0.shape[1]), -jnp.inf, F32))
        cand = jnp.concatenate(pieces, axis=0)
        best = _top_values(cand, TOPK + 1)
        m = best[0]
        z = jnp.exp(best[0] - m)
        for r in range(1, TOPK):
            z = z + jnp.exp(best[r] - m)
        inv_z = 0.5 / z
        tau = 0.5 * (best[TOPK - 1] + best[TOPK])
        m0, m1 = top0[0], top1[0]
        r_ref[_R_E0, h] = jnp.exp(s0 - m0)
        r_ref[_R_E1, h] = jnp.exp(s1 - m1) * inv_z
        r_ref[_R_PHI, h] = jnp.exp((tau - m1) - s0) * inv_z
        return carry

    lax.fori_loop(0, PEER_HEADS, head, 0)


def peer_routing(q, keys, *, tb=512):
    _, t, _ = q.shape
    return pl.pallas_call(
        _route_kernel,
        grid=(t // tb,),
        in_specs=[pl.BlockSpec((2 * PEER_HEADS, tb, N_KEYS), lambda i: (0, i, 0)),
                  pl.BlockSpec((2 * PEER_HEADS, N_KEYS, N_KEYS), lambda i: (0, 0, 0))],
        out_specs=pl.BlockSpec((3, PEER_HEADS, N_KEYS, tb), lambda i: (0, 0, 0, i)),
        out_shape=jax.ShapeDtypeStruct((3, PEER_HEADS, N_KEYS, t), F32),
        compiler_params=_cparams(("parallel",)),
        name="peer_routing",
    )(q, keys)


def _gelu_times2(x):
    return x * (1.0 + lax.erf(x * (1.0 / math.sqrt(2.0))))


def _peer_dense_kernel(x_ref, u_ref, v_ref, r_ref, o_ref, a_ref, acc_ref, *, te):
    et = pl.program_id(1)
    nblk = te // N_KEYS
    tb = x_ref.shape[0]
    tq = 128
    nt = (((1,), (1,)), ((), ()))

    @pl.when(et == 0)
    def _():
        acc_ref[...] = jnp.zeros_like(acc_ref)

    ht = lax.dot_general(u_ref[...], x_ref[...], nt, preferred_element_type=F32)
    for j in range(nblk):
        i = et * nblk + j
        rows = slice(j * N_KEYS, (j + 1) * N_KEYS)
        phi_rows = [r_ref[_R_PHI, h, pl.ds(i, 1), :] for h in range(PEER_HEADS)]
        e0_rows = [r_ref[_R_E0, h, pl.ds(i, 1), :] for h in range(PEER_HEADS)]
        for q in range(tb // tq):
            ts = slice(q * tq, (q + 1) * tq)
            gate = None
            for h in range(PEER_HEADS):
                e1 = r_ref[_R_E1, h, :, ts]
                sel = jnp.where(e1 >= phi_rows[h][:, ts], e1, 0.0) * e0_rows[h][:, ts]
                gate = sel if gate is None else gate + sel
            act = _gelu_times2(ht[rows, ts]) * gate
            a_ref[ts, rows] = act.T.astype(a_ref.dtype)
    acc_ref[...] += jnp.dot(a_ref[...], v_ref[...], preferred_element_type=F32)

    @pl.when(et == pl.num_programs(1) - 1)
    def _():
        o_ref[...] = acc_ref[...].astype(o_ref.dtype)


def peer_dense(xn, u, v, r, *, tb=512, te=512):
    t, d = xn.shape
    e = u.shape[0]
    tb = min(tb, t)
    return pl.pallas_call(
        functools.partial(_peer_dense_kernel, te=te),
        grid=(t // tb, e // te),
        in_specs=[
            pl.BlockSpec((tb, d), lambda i, j: (i, 0)),
            pl.BlockSpec((te, d), lambda i, j: (j, 0)),
            pl.BlockSpec((te, d), lambda i, j: (j, 0)),
            pl.BlockSpec((3, PEER_HEADS, N_KEYS, tb), lambda i, j: (0, 0, 0, i)),
        ],
        out_specs=pl.BlockSpec((tb, d), lambda i, j: (i, 0)),
        out_shape=jax.ShapeDtypeStruct((t, d), BF16),
        scratch_shapes=[pltpu.VMEM((tb, te), BF16), pltpu.VMEM((tb, d), F32)],
        compiler_params=_cparams(("parallel", "arbitrary")),
        name="peer_dense",
    )(xn, u, v, r)


def _final_kernel(h_ref, p_ref, w_ref, o_ref):
    x = h_ref[...] + p_ref[...]
    ms = jnp.mean(x * x, axis=-1, keepdims=True)
    o_ref[...] = x * lax.rsqrt(ms + RMS_EPS) * w_ref[...]


def final_norm(h1, peer, w, *, tm=256):
    t, d = h1.shape
    return pl.pallas_call(
        _final_kernel,
        grid=(t // tm,),
        in_specs=[pl.BlockSpec((tm, d), lambda i: (i, 0)), pl.BlockSpec((tm, d), lambda i: (i, 0)),
                  pl.BlockSpec((1, d), lambda i: (0, 0))],
        out_specs=pl.BlockSpec((tm, d), lambda i: (i, 0)),
        out_shape=jax.ShapeDtypeStruct((t, d), F32),
        compiler_params=_cparams(("parallel",)),
        name="final_norm",
    )(h1, peer, w.reshape(1, d))


def kernel(x, w_in, ssd_conv_w, ssd_conv_b, ssd_dt_bias, ssd_a_log, ssd_d, ssd_norm_w, sc_conv_w, sc_norm_w, w_out,
           norm_mix_w, norm_ffn_w, peer_w_query, peer_sub_keys, peer_u, peer_v, norm_final_w):
    batch, seq, d = x.shape
    t = batch * seq
    h = x.reshape(t, d)
    assert w_in.shape[0] == 1, "single-layer block"
    o_c = 2 * D_SSD + N_GROUPS * D_STATE
    o2 = o_c + N_GROUPS * D_STATE
    o3 = o2 + N_HEADS
    ssd_off = {"z": 0, "xs": D_SSD, "B": 2 * D_SSD, "C": o_c}
    sc_off = {"scb": 0, "scc": D_CONV, "sch": 2 * D_CONV}
    wt = w_in[0].T.astype(BF16)
    w_sc = wt[o3:]
    w_out_b = w_out[0].astype(BF16)
    wq_b = peer_w_query[0].astype(BF16)
    keys = peer_sub_keys[0].reshape(2 * PEER_HEADS, N_KEYS, N_KEYS).astype(BF16)

    proj_ssd, xn = norm_projection(h, norm_mix_w[0], wt, n=o2, tm=512, tn=1024, out_dtype=BF16,
                                   name="in_projection_ssd")
    proj_sc = projection(xn, w_sc, n=w_sc.shape[0], tm=1024, tn=1024, out_dtype=BF16, name="in_projection_sc")
    dt_raw = projection(xn, wt, n=128, n_start=o2, tm=1024, tn=128, out_dtype=F32, name="dt_projection")
    y_ssd = ssd_mixer(proj_ssd, dt_raw, ssd_conv_w[0], ssd_conv_b[0], ssd_dt_bias[0], ssd_a_log[0], ssd_d[0],
                      ssd_norm_w[0], batch=batch, seq=seq, col_off=ssd_off)
    y_sc = shortconv_mixer(proj_sc, sc_conv_w[0], sc_norm_w[0], batch=batch, seq=seq, col_off=sc_off)
    h, u_b, v_b = out_projection(y_ssd, y_sc, w_out_b, h, peer_u[0], peer_v[0])

    q, xn2 = norm_query_projection(h, norm_ffn_w[0], wq_b)
    r = peer_routing(q, keys)
    peer = peer_dense(xn2, u_b, v_b, r)
    out = final_norm(h, peer, norm_final_w)
    return out.reshape(batch, seq, d)
```

```python
import functools
import math

import jax
import jax.numpy as jnp
from jax import lax
from jax.experimental import pallas as pl
from jax.experimental.pallas import tpu as pltpu

F32 = jnp.float32
BF16 = jnp.bfloat16

D_MODEL = 4096
D_SSD = 4096
D_CONV = 4096
HEAD_DIM = 64
N_HEADS = 64
N_GROUPS = 8
HEADS_PER_GROUP = 8
D_STATE = 128
CHUNK = 128
GROUP_W = HEADS_PER_GROUP * HEAD_DIM
PEER_HEADS = 8
N_KEYS = 128
N_EXPERTS = N_KEYS * N_KEYS
TOPK = 16
RMS_EPS = 1e-6

VMEM_LIMIT_BYTES = 58 * 1024 * 1024
CARRY_ROWS = 8


def _cparams(sem):
    return pltpu.CompilerParams(dimension_semantics=sem, vmem_limit_bytes=VMEM_LIMIT_BYTES)


_NT = (((1,), (1,)), ((), ()))


def _rms_bf16(x, w):
    ms = jnp.mean(x * x, axis=-1, keepdims=True)
    return (x * lax.rsqrt(ms + RMS_EPS) * w).astype(BF16)


def _proj_kernel(a_ref, b_ref, o_ref):
    o_ref[...] = lax.dot_general(a_ref[...], b_ref[...], _NT, preferred_element_type=F32).astype(o_ref.dtype)


def projection(a, w_t, *, n, n_start=0, tm, tn, out_dtype, name):
    m, kdim = a.shape
    tm = min(tm, m)
    j0 = n_start // tn
    return pl.pallas_call(
        _proj_kernel,
        grid=(m // tm, n // tn),
        in_specs=[pl.BlockSpec((tm, kdim), lambda i, j: (i, 0)), pl.BlockSpec((tn, kdim), lambda i, j: (j0 + j, 0))],
        out_specs=pl.BlockSpec((tm, tn), lambda i, j: (i, j)),
        out_shape=jax.ShapeDtypeStruct((m, n), out_dtype),
        compiler_params=_cparams(("parallel", "parallel")),
        name=name,
    )(a, w_t)


def _rmsnorm_kernel(x_ref, w_ref, o_ref):
    o_ref[...] = _rms_bf16(x_ref[...], w_ref[...])


def rmsnorm_rows(x, w, *, tm=256):
    t, d = x.shape
    return pl.pallas_call(
        _rmsnorm_kernel,
        grid=(t // tm,),
        in_specs=[pl.BlockSpec((tm, d), lambda i: (i, 0)), pl.BlockSpec((1, d), lambda i: (0, 0))],
        out_specs=pl.BlockSpec((tm, d), lambda i: (i, 0)),
        out_shape=jax.ShapeDtypeStruct((t, d), BF16),
        compiler_params=_cparams(("parallel",)),
        name="rmsnorm_rows",
    )(x, w.reshape(1, d))


def _silu(x):
    hx = 0.5 * x
    return hx + hx * jnp.tanh(hx)


def _softplus(x):
    return jnp.maximum(x, 0.0) + jnp.log(1.0 + jnp.exp(-jnp.abs(x)))


_P_CONV, _P_BIAS, _P_DSKIP, _P_NORM = 0, 4, 5, 6


def _ssd_kernel(z_ref, xs_ref, b_ref, c_ref, dt_ref, pg_ref, pdt_ref, o_ref, state_ref, ext_ref, *, chunks):
    g = pl.program_id(1)
    c = pl.program_id(2)
    L = CHUNK
    W = GROUP_W
    N = D_STATE
    XBC = W + 2 * N

    @pl.when(c == 0)
    def _():
        state_ref[...] = jnp.zeros_like(state_ref)
        ext_ref[0:CARRY_ROWS, :] = jnp.zeros((CARRY_ROWS, XBC), F32)

    row = lax.broadcasted_iota(jnp.int32, (L, L), 0)
    col = lax.broadcasted_iota(jnp.int32, (L, L), 1)
    causal = row >= col
    tril = causal.astype(BF16)
    er = lax.broadcasted_iota(jnp.int32, (128, W), 0)
    ec = lax.broadcasted_iota(jnp.int32, (128, W), 1)
    expand = (er == ec // HEAD_DIM).astype(BF16)

    def chunk(ci, carry):
        rs = pl.ds(pl.multiple_of(ci * L, L), L)
        ext_ref[CARRY_ROWS:CARRY_ROWS + L, 0:W] = xs_ref[rs, :].astype(F32)
        ext_ref[CARRY_ROWS:CARRY_ROWS + L, W:W + N] = b_ref[rs, :].astype(F32)
        ext_ref[CARRY_ROWS:CARRY_ROWS + L, W + N:XBC] = c_ref[rs, :].astype(F32)
        acc = ext_ref[CARRY_ROWS:CARRY_ROWS + L, :] * pg_ref[_P_CONV + 3:_P_CONV + 4, :]
        for k in range(3):
            acc = acc + ext_ref[CARRY_ROWS - 3 + k:CARRY_ROWS - 3 + k + L, :] * pg_ref[_P_CONV + k:_P_CONV + k + 1, :]
        xbc = _silu(acc + pg_ref[_P_BIAS:_P_BIAS + 1, :])
        ext_ref[0:CARRY_ROWS, :] = ext_ref[L:L + CARRY_ROWS, :]
        xs = xbc[:, 0:W]
        b_b = xbc[:, W:W + N].astype(BF16)
        c_b = xbc[:, W + N:XBC].astype(BF16)

        dt_all = _softplus(dt_ref[rs, :] + pdt_ref[0:1, :])
        da_all = dt_all * (-jnp.exp(pdt_ref[1:2, :]))
        shift = (128 - g * HEADS_PER_GROUP) % 128
        dt_g = pltpu.roll(dt_all, shift, 1)
        da_g = pltpu.roll(da_all, shift, 1)

        hi = da_g.astype(BF16)
        r1 = da_g - hi.astype(F32)
        mid = r1.astype(BF16)
        lo = (r1 - mid.astype(F32)).astype(BF16)
        acum = (jnp.dot(tril, hi, preferred_element_type=F32)
                + jnp.dot(tril, mid, preferred_element_type=F32)
                + jnp.dot(tril, lo, preferred_element_type=F32))
        acum_t = acum.T
        a_last = acum[L - 1:L, :]

        def expand_heads(v):
            vh = v.astype(BF16)
            vl = (v - vh.astype(F32)).astype(BF16)
            return jnp.dot(vh, expand, preferred_element_type=F32) + jnp.dot(vl, expand, preferred_element_type=F32)

        dt_x = expand_heads(dt_g)
        ea_x = expand_heads(jnp.exp(acum))
        to_end_x = expand_heads(jnp.exp(a_last - acum))

        xdt = xs * dt_x
        xdt_b = xdt.astype(BF16)

        cb = lax.dot_general(c_b, b_b, (((1,), (1,)), ((), ())), preferred_element_type=F32)
        state = state_ref[...]
        y_off = jnp.dot(c_b, state.astype(BF16), preferred_element_type=F32) * ea_x

        y_parts = []
        for r in range(HEADS_PER_GROUP):
            seg = acum[:, r:r + 1] - acum_t[r:r + 1, :]
            decay = jnp.exp(jnp.where(causal, seg, -jnp.inf))
            m = (cb * decay).astype(BF16)
            y_parts.append(jnp.dot(m, xdt_b[:, r * HEAD_DIM:(r + 1) * HEAD_DIM], preferred_element_type=F32))
        y = jnp.concatenate(y_parts, axis=1) + y_off + pg_ref[_P_DSKIP:_P_DSKIP + 1, 0:W] * xs

        upd = lax.dot_general(b_b, (xdt * to_end_x).astype(BF16), (((0,), (0,)), ((), ())),
                              preferred_element_type=F32)
        state_ref[...] = state * ea_x[L - 1:L, :] + upd

        y = y * _silu(z_ref[rs, :].astype(F32))
        ms = jnp.mean(y * y, axis=-1, keepdims=True)
        o_ref[rs, :] = (y * lax.rsqrt(ms + RMS_EPS) * pg_ref[_P_NORM:_P_NORM + 1, 0:W]).astype(o_ref.dtype)
        return carry

    lax.fori_loop(0, chunks, chunk, 0)


def ssd_mixer(proj, dt_raw, conv_w, conv_b, dt_bias, a_log, d_skip, norm_w, *, batch, seq, col_off, chunks=8):
    t = batch * seq
    chunks = min(chunks, seq // CHUNK)
    nchunk = seq // (CHUNK * chunks)
    W, N, L, G = GROUP_W, D_STATE, CHUNK * chunks, N_GROUPS
    zb, xb, bb, cb_ = col_off["z"] // W, col_off["xs"] // W, col_off["B"] // N, col_off["C"] // N

    def rows(b, g, c):
        return b * nchunk + c

    def per_group(v):
        r = v.shape[0]
        x_part = v[:, :D_SSD].reshape(r, G, W)
        b_part = v[:, D_SSD:D_SSD + G * N].reshape(r, G, N)
        c_part = v[:, D_SSD + G * N:].reshape(r, G, N)
        return jnp.concatenate([x_part, b_part, c_part], axis=2).transpose(1, 0, 2)

    def x_only(v):
        return jnp.pad(v.astype(F32).reshape(G, 1, W), ((0, 0), (0, 0), (0, 2 * N)))

    pg = jnp.concatenate([per_group(conv_w.astype(F32)), per_group(conv_b.astype(F32).reshape(1, -1)),
                          x_only(jnp.repeat(d_skip, HEAD_DIM)), x_only(norm_w),
                          jnp.zeros((G, 1, W + 2 * N), F32)], axis=1)
    pad = 128 - N_HEADS
    pdt = jnp.pad(jnp.stack([dt_bias.astype(F32), a_log.astype(F32)]), ((0, 6), (0, pad)))

    in_specs = [
        pl.BlockSpec((L, W), lambda b, g, c: (rows(b, g, c), zb + g)),
        pl.BlockSpec((L, W), lambda b, g, c: (rows(b, g, c), xb + g)),
        pl.BlockSpec((L, N), lambda b, g, c: (rows(b, g, c), bb + g)),
        pl.BlockSpec((L, N), lambda b, g, c: (rows(b, g, c), cb_ + g)),
        pl.BlockSpec((L, 128), lambda b, g, c: (rows(b, g, c), 0)),
        pl.BlockSpec((None, 8, W + 2 * N), lambda b, g, c: (g, 0, 0)),
        pl.BlockSpec((8, 128), lambda b, g, c: (0, 0)),
    ]
    return pl.pallas_call(
        functools.partial(_ssd_kernel, chunks=chunks),
        grid=(batch, N_GROUPS, nchunk),
        in_specs=in_specs,
        out_specs=pl.BlockSpec((L, W), lambda b, g, c: (rows(b, g, c), g)),
        out_shape=jax.ShapeDtypeStruct((t, D_SSD), BF16),
        scratch_shapes=[pltpu.VMEM((N, W), F32), pltpu.VMEM((CHUNK + CARRY_ROWS, W + 2 * N), F32)],
        compiler_params=_cparams(("parallel", "parallel", "arbitrary")),
        name="ssd_mixer",
    )(proj, proj, proj, proj, dt_raw, pg, pdt)


def _shortconv_kernel(b_ref, c_ref, h_ref, w_ref, nw_ref, o_ref, ext_ref, y_ref, *, rows, col_chunk):
    s = pl.program_id(1)
    d = D_CONV

    @pl.when(s == 0)
    def _():
        ext_ref[0:CARRY_ROWS, :] = jnp.zeros((CARRY_ROWS, d), F32)

    ss = jnp.zeros((rows, 1), F32)
    for j in range(0, d, col_chunk):
        sl = slice(j, j + col_chunk)
        ext_ref[CARRY_ROWS:CARRY_ROWS + rows, sl] = c_ref[:, sl].astype(F32) * h_ref[:, sl].astype(F32)
        acc = ext_ref[CARRY_ROWS:CARRY_ROWS + rows, sl] * w_ref[2:3, sl]
        for k in range(2):
            acc = acc + ext_ref[CARRY_ROWS - 2 + k:CARRY_ROWS - 2 + k + rows, sl] * w_ref[k:k + 1, sl]
        y = b_ref[:, sl].astype(F32) * acc
        y_ref[:, sl] = y
        ss = ss + jnp.sum(y * y, axis=-1, keepdims=True)
    ext_ref[0:CARRY_ROWS, :] = ext_ref[rows:rows + CARRY_ROWS, :]
    inv = lax.rsqrt(ss * (1.0 / d) + RMS_EPS)
    for j in range(0, d, col_chunk):
        sl = slice(j, j + col_chunk)
        o_ref[:, sl] = (y_ref[:, sl] * inv * nw_ref[:, sl]).astype(o_ref.dtype)


def shortconv_mixer(proj, conv_w, norm_w, *, batch, seq, col_off, rows=256, col_chunk=512):
    t = batch * seq
    nblk = seq // rows
    d = D_CONV
    bb, cb_, hb = col_off["scb"] // d, col_off["scc"] // d, col_off["sch"] // d
    return pl.pallas_call(
        functools.partial(_shortconv_kernel, rows=rows, col_chunk=col_chunk),
        grid=(batch, nblk),
        in_specs=[
            pl.BlockSpec((rows, d), lambda b, s: (b * nblk + s, bb)),
            pl.BlockSpec((rows, d), lambda b, s: (b * nblk + s, cb_)),
            pl.BlockSpec((rows, d), lambda b, s: (b * nblk + s, hb)),
            pl.BlockSpec((3, d), lambda b, s: (0, 0)),
            pl.BlockSpec((1, d), lambda b, s: (0, 0)),
        ],
        out_specs=pl.BlockSpec((rows, d), lambda b, s: (b * nblk + s, 0)),
        out_shape=jax.ShapeDtypeStruct((t, d), BF16),
        scratch_shapes=[pltpu.VMEM((rows + CARRY_ROWS, d), F32), pltpu.VMEM((rows, d), F32)],
        compiler_params=_cparams(("parallel", "arbitrary")),
        name="shortconv_mixer",
    )(proj, proj, proj, conv_w.astype(F32), norm_w.astype(F32).reshape(1, d))


def _outproj_kernel(a1_ref, a2_ref, b1_ref, b2_ref, r_ref, u_ref, v_ref, o_ref, ub_ref, vb_ref):
    acc = jnp.dot(a1_ref[...], b1_ref[...], preferred_element_type=F32)
    acc = acc + jnp.dot(a2_ref[...], b2_ref[...], preferred_element_type=F32)
    o_ref[...] = acc + r_ref[...]
    ub_ref[...] = u_ref[...].astype(ub_ref.dtype)
    vb_ref[...] = v_ref[...].astype(vb_ref.dtype)


def out_projection(y_ssd, y_sc, w_out, x, u, v, *, tm=1024, tn=256):
    m, kh = y_ssd.shape
    _, n = w_out.shape
    e, d = u.shape
    tm = min(tm, m)
    nj = n // tn
    steps = (m // tm) * nj
    er = e // steps
    assert er * steps == e and er % 16 == 0, "PEER table rows must split evenly over the grid steps"
    tspec = pl.BlockSpec((er, d), lambda i, j: (i * nj + j, 0))
    return pl.pallas_call(
        _outproj_kernel,
        grid=(m // tm, nj),
        in_specs=[
            pl.BlockSpec((tm, kh), lambda i, j: (i, 0)),
            pl.BlockSpec((tm, kh), lambda i, j: (i, 0)),
            pl.BlockSpec((kh, tn), lambda i, j: (0, j)),
            pl.BlockSpec((kh, tn), lambda i, j: (1, j)),
            pl.BlockSpec((tm, tn), lambda i, j: (i, j)),
            tspec, tspec,
        ],
        out_specs=[pl.BlockSpec((tm, tn), lambda i, j: (i, j)), tspec, tspec],
        out_shape=[jax.ShapeDtypeStruct((m, n), F32), jax.ShapeDtypeStruct((e, d), BF16),
                   jax.ShapeDtypeStruct((e, d), BF16)],
        compiler_params=_cparams(("parallel", "parallel")),
        name="out_projection",
    )(y_ssd, y_sc, w_out, w_out, x, u, v)


def _top_values(x, n):
    outs = []
    for r in range(n):
        m = jnp.max(x, axis=0, keepdims=True)
        outs.append(m)
        if r < n - 1:
            x = jnp.where(x >= m, -jnp.inf, x)
    return outs


def _query_kernel(x_ref, w_ref, o_ref):
    acc = jnp.dot(x_ref[...], w_ref[...], preferred_element_type=F32)
    for c in range(o_ref.shape[0]):
        o_ref[c] = acc[:, c * N_KEYS:(c + 1) * N_KEYS].astype(o_ref.dtype)


def query_projection(xn, wq, *, tm=1024, tn=1024):
    t, d = xn.shape
    n = wq.shape[1]
    tm = min(tm, t)
    per = tn // N_KEYS
    return pl.pallas_call(
        _query_kernel,
        grid=(t // tm, n // tn),
        in_specs=[pl.BlockSpec((tm, d), lambda i, j: (i, 0)), pl.BlockSpec((d, tn), lambda i, j: (0, j))],
        out_specs=pl.BlockSpec((per, tm, N_KEYS), lambda i, j: (j, i, 0)),
        out_shape=jax.ShapeDtypeStruct((n // N_KEYS, t, N_KEYS), BF16),
        compiler_params=_cparams(("parallel", "parallel")),
        name="query_projection",
    )(xn, wq)


_R_PHI, _R_E0, _R_E1 = 0, 1, 2


def _route_kernel(q_ref, keys_ref, r_ref):
    nt = (((1,), (1,)), ((), ()))

    def head(h, carry):
        s0 = lax.dot_general(keys_ref[2 * h], q_ref[2 * h], nt, preferred_element_type=F32)
        s1 = lax.dot_general(keys_ref[2 * h + 1], q_ref[2 * h + 1], nt, preferred_element_type=F32)
        top0 = _top_values(s0, TOPK + 1)
        top1 = _top_values(s1, TOPK + 1)
        t1 = jnp.concatenate(top1, axis=0)
        pieces = [top0[r] + t1[0:(TOPK + 1) // (r + 1)] for r in range(TOPK + 1)]
        n_cand = sum(p.shape[0] for p in pieces)
        pieces.append(jnp.full((-n_cand % 8, s0.shape[1]), -jnp.inf, F32))
        cand = jnp.concatenate(pieces, axis=0)
        best = _top_values(cand, TOPK + 1)
        m = best[0]
        z = jnp.exp(best[0] - m)
        for r in range(1, TOPK):
            z = z + jnp.exp(best[r] - m)
        inv_z = 0.5 / z
        tau = 0.5 * (best[TOPK - 1] + best[TOPK])
        m0, m1 = top0[0], top1[0]
        r_ref[_R_E0, h] = jnp.exp(s0 - m0)
        r_ref[_R_E1, h] = jnp.exp(s1 - m1) * inv_z
        r_ref[_R_PHI, h] = jnp.exp((tau - m1) - s0) * inv_z
        return carry

    lax.fori_loop(0, PEER_HEADS, head, 0)


def peer_routing(q, keys, *, tb=512):
    _, t, _ = q.shape
    return pl.pallas_call(
        _route_kernel,
        grid=(t // tb,),
        in_specs=[pl.BlockSpec((2 * PEER_HEADS, tb, N_KEYS), lambda i: (0, i, 0)),
                  pl.BlockSpec((2 * PEER_HEADS, N_KEYS, N_KEYS), lambda i: (0, 0, 0))],
        out_specs=pl.BlockSpec((3, PEER_HEADS, N_KEYS, tb), lambda i: (0, 0, 0, i)),
        out_shape=jax.ShapeDtypeStruct((3, PEER_HEADS, N_KEYS, t), F32),
        compiler_params=_cparams(("parallel",)),
        name="peer_routing",
    )(q, keys)


def _gelu_times2(x):
    return x * (1.0 + lax.erf(x * (1.0 / math.sqrt(2.0))))


def _peer_dense_kernel(x_ref, u_ref, v_ref, r_ref, o_ref, a_ref, acc_ref, *, te):
    et = pl.program_id(1)
    nblk = te // N_KEYS
    tb = x_ref.shape[0]
    tq = 128
    nt = (((1,), (1,)), ((), ()))

    @pl.when(et == 0)
    def _():
        acc_ref[...] = jnp.zeros_like(acc_ref)

    ht = lax.dot_general(u_ref[...], x_ref[...], nt, preferred_element_type=F32)
    for j in range(nblk):
        i = et * nblk + j
        rows = slice(j * N_KEYS, (j + 1) * N_KEYS)
        phi_rows = [r_ref[_R_PHI, h, pl.ds(i, 1), :] for h in range(PEER_HEADS)]
        e0_rows = [r_ref[_R_E0, h, pl.ds(i, 1), :] for h in range(PEER_HEADS)]
        for q in range(tb // tq):
            ts = slice(q * tq, (q + 1) * tq)
            gate = None
            for h in range(PEER_HEADS):
                e1 = r_ref[_R_E1, h, :, ts]
                sel = jnp.where(e1 >= phi_rows[h][:, ts], e1, 0.0) * e0_rows[h][:, ts]
                gate = sel if gate is None else gate + sel
            act = _gelu_times2(ht[rows, ts]) * gate
            a_ref[ts, rows] = act.T.astype(a_ref.dtype)
    acc_ref[...] += jnp.dot(a_ref[...], v_ref[...], preferred_element_type=F32)

    @pl.when(et == pl.num_programs(1) - 1)
    def _():
        o_ref[...] = acc_ref[...].astype(o_ref.dtype)


def peer_dense(xn, u, v, r, *, tb=512, te=512):
    t, d = xn.shape
    e = u.shape[0]
    tb = min(tb, t)
    return pl.pallas_call(
        functools.partial(_peer_dense_kernel, te=te),
        grid=(t // tb, e // te),
        in_specs=[
            pl.BlockSpec((tb, d), lambda i, j: (i, 0)),
            pl.BlockSpec((te, d), lambda i, j: (j, 0)),
            pl.BlockSpec((te, d), lambda i, j: (j, 0)),
            pl.BlockSpec((3, PEER_HEADS, N_KEYS, tb), lambda i, j: (0, 0, 0, i)),
        ],
        out_specs=pl.BlockSpec((tb, d), lambda i, j: (i, 0)),
        out_shape=jax.ShapeDtypeStruct((t, d), BF16),
        scratch_shapes=[pltpu.VMEM((tb, te), BF16), pltpu.VMEM((tb, d), F32)],
        compiler_params=_cparams(("parallel", "arbitrary")),
        name="peer_dense",
    )(xn, u, v, r)


def _final_kernel(h_ref, p_ref, w_ref, o_ref):
    x = h_ref[...] + p_ref[...]
    ms = jnp.mean(x * x, axis=-1, keepdims=True)
    o_ref[...] = x * lax.rsqrt(ms + RMS_EPS) * w_ref[...]


def final_norm(h1, peer, w, *, tm=256):
    t, d = h1.shape
    return pl.pallas_call(
        _final_kernel,
        grid=(t // tm,),
        in_specs=[pl.BlockSpec((tm, d), lambda i: (i, 0)), pl.BlockSpec((tm, d), lambda i: (i, 0)),
                  pl.BlockSpec((1, d), lambda i: (0, 0))],
        out_specs=pl.BlockSpec((tm, d), lambda i: (i, 0)),
        out_shape=jax.ShapeDtypeStruct((t, d), F32),
        compiler_params=_cparams(("parallel",)),
        name="final_norm",
    )(h1, peer, w.reshape(1, d))


def kernel(x, w_in, ssd_conv_w, ssd_conv_b, ssd_dt_bias, ssd_a_log, ssd_d, ssd_norm_w, sc_conv_w, sc_norm_w, w_out,
           norm_mix_w, norm_ffn_w, peer_w_query, peer_sub_keys, peer_u, peer_v, norm_final_w):
    batch, seq, d = x.shape
    t = batch * seq
    h = x.reshape(t, d)
    assert w_in.shape[0] == 1, "single-layer block"
    o_c = 2 * D_SSD + N_GROUPS * D_STATE
    o2 = o_c + N_GROUPS * D_STATE
    o3 = o2 + N_HEADS
    ssd_off = {"z": 0, "xs": D_SSD, "B": 2 * D_SSD, "C": o_c}
    sc_off = {"scb": 0, "scc": D_CONV, "sch": 2 * D_CONV}
    wt = w_in[0].T.astype(BF16)
    w_sc = wt[o3:]
    w_out_b = w_out[0].astype(BF16)
    wq_b = peer_w_query[0].astype(BF16)
    keys = peer_sub_keys[0].reshape(2 * PEER_HEADS, N_KEYS, N_KEYS).astype(BF16)

    xn = rmsnorm_rows(h, norm_mix_w[0])
    proj_ssd = projection(xn, wt, n=o2, tm=1024, tn=1024, out_dtype=BF16, name="in_projection_ssd")
    proj_sc = projection(xn, w_sc, n=w_sc.shape[0], tm=1024, tn=1024, out_dtype=BF16, name="in_projection_sc")
    dt_raw = projection(xn, wt, n=128, n_start=o2, tm=1024, tn=128, out_dtype=F32, name="dt_projection")
    y_ssd = ssd_mixer(proj_ssd, dt_raw, ssd_conv_w[0], ssd_conv_b[0], ssd_dt_bias[0], ssd_a_log[0], ssd_d[0],
                      ssd_norm_w[0], batch=batch, seq=seq, col_off=ssd_off)
    y_sc = shortconv_mixer(proj_sc, sc_conv_w[0], sc_norm_w[0], batch=batch, seq=seq, col_off=sc_off)
    h, u_b, v_b = out_projection(y_ssd, y_sc, w_out_b, h, peer_u[0], peer_v[0])

    xn2 = rmsnorm_rows(h, norm_ffn_w[0])
    q = query_projection(xn2, wq_b)
    r = peer_routing(q, keys)
    peer = peer_dense(xn2, u_b, v_b, r)
    out = final_norm(h, peer, norm_final_w)
    return out.reshape(batch, seq, d)
```

```python
import functools
import math

import jax
import jax.numpy as jnp
from jax import lax
from jax.experimental import pallas as pl
from jax.experimental.pallas import tpu as pltpu

F32 = jnp.float32
BF16 = jnp.bfloat16

D_MODEL = 4096
D_SSD = 4096
D_CONV = 4096
HEAD_DIM = 64
N_HEADS = 64
N_GROUPS = 8
HEADS_PER_GROUP = 8
D_STATE = 128
CHUNK = 128
GROUP_W = HEADS_PER_GROUP * HEAD_DIM
PEER_HEADS = 8
N_KEYS = 128
N_EXPERTS = N_KEYS * N_KEYS
TOPK = 16
RMS_EPS = 1e-6

VMEM_LIMIT_BYTES = 58 * 1024 * 1024
CARRY_ROWS = 8


def _cparams(sem):
    return pltpu.CompilerParams(dimension_semantics=sem, vmem_limit_bytes=VMEM_LIMIT_BYTES)


_NT = (((1,), (1,)), ((), ()))


def _rms_bf16(x, w):
    ms = jnp.mean(x * x, axis=-1, keepdims=True)
    return (x * lax.rsqrt(ms + RMS_EPS) * w).astype(BF16)


def _proj_kernel(a_ref, b_ref, *refs):
    n_cast = len(refs) // 2
    src_refs, o_ref, dst_refs = refs[:n_cast], refs[n_cast], refs[n_cast + 1:]
    o_ref[...] = lax.dot_general(a_ref[...], b_ref[...], _NT, preferred_element_type=F32).astype(o_ref.dtype)
    for src, dst in zip(src_refs, dst_refs):
        dst[...] = src[...].astype(dst.dtype)


def projection(a, w_t, *, n, n_start=0, tm, tn, out_dtype, name, casts=()):
    m, kdim = a.shape
    tm = min(tm, m)
    j0 = n_start // tn
    nj = n // tn
    steps = (m // tm) * nj
    cast_specs = []
    for c in casts:
        rows = c.shape[0] // steps
        assert rows * steps == c.shape[0] and rows % 16 == 0, "side-cast rows must split evenly over the grid steps"
        cast_specs.append(pl.BlockSpec((rows, c.shape[1]), lambda i, j: (i * nj + j, 0)))
    out = pl.pallas_call(
        _proj_kernel,
        grid=(m // tm, nj),
        in_specs=[pl.BlockSpec((tm, kdim), lambda i, j: (i, 0)), pl.BlockSpec((tn, kdim), lambda i, j: (j0 + j, 0)),
                  *cast_specs],
        out_specs=[pl.BlockSpec((tm, tn), lambda i, j: (i, j)), *cast_specs],
        out_shape=[jax.ShapeDtypeStruct((m, n), out_dtype)] + [jax.ShapeDtypeStruct(c.shape, BF16) for c in casts],
        compiler_params=_cparams(("parallel", "parallel")),
        name=name,
    )(a, w_t, *casts)
    return out if casts else out[0]


def _rmsnorm_kernel(x_ref, w_ref, o_ref):
    o_ref[...] = _rms_bf16(x_ref[...], w_ref[...])


def rmsnorm_rows(x, w, *, tm=256):
    t, d = x.shape
    return pl.pallas_call(
        _rmsnorm_kernel,
        grid=(t // tm,),
        in_specs=[pl.BlockSpec((tm, d), lambda i: (i, 0)), pl.BlockSpec((1, d), lambda i: (0, 0))],
        out_specs=pl.BlockSpec((tm, d), lambda i: (i, 0)),
        out_shape=jax.ShapeDtypeStruct((t, d), BF16),
        compiler_params=_cparams(("parallel",)),
        name="rmsnorm_rows",
    )(x, w.reshape(1, d))


def _silu(x):
    hx = 0.5 * x
    return hx + hx * jnp.tanh(hx)


def _softplus(x):
    return jnp.maximum(x, 0.0) + jnp.log(1.0 + jnp.exp(-jnp.abs(x)))


_P_CONV, _P_BIAS, _P_DSKIP, _P_NORM = 0, 4, 5, 6


def _ssd_kernel(z_ref, xs_ref, b_ref, c_ref, dt_ref, pg_ref, pdt_ref, o_ref, state_ref, ext_ref, *, chunks):
    g = pl.program_id(1)
    c = pl.program_id(2)
    L = CHUNK
    W = GROUP_W
    N = D_STATE
    XBC = W + 2 * N

    @pl.when(c == 0)
    def _():
        state_ref[...] = jnp.zeros_like(state_ref)
        ext_ref[0:CARRY_ROWS, :] = jnp.zeros((CARRY_ROWS, XBC), F32)

    row = lax.broadcasted_iota(jnp.int32, (L, L), 0)
    col = lax.broadcasted_iota(jnp.int32, (L, L), 1)
    causal = row >= col
    tril = causal.astype(BF16)
    er = lax.broadcasted_iota(jnp.int32, (128, W), 0)
    ec = lax.broadcasted_iota(jnp.int32, (128, W), 1)
    expand = (er == ec // HEAD_DIM).astype(BF16)

    def chunk(ci, carry):
        rs = pl.ds(pl.multiple_of(ci * L, L), L)
        ext_ref[CARRY_ROWS:CARRY_ROWS + L, 0:W] = xs_ref[rs, :].astype(F32)
        ext_ref[CARRY_ROWS:CARRY_ROWS + L, W:W + N] = b_ref[rs, :].astype(F32)
        ext_ref[CARRY_ROWS:CARRY_ROWS + L, W + N:XBC] = c_ref[rs, :].astype(F32)
        acc = ext_ref[CARRY_ROWS:CARRY_ROWS + L, :] * pg_ref[_P_CONV + 3:_P_CONV + 4, :]
        for k in range(3):
            acc = acc + ext_ref[CARRY_ROWS - 3 + k:CARRY_ROWS - 3 + k + L, :] * pg_ref[_P_CONV + k:_P_CONV + k + 1, :]
        xbc = _silu(acc + pg_ref[_P_BIAS:_P_BIAS + 1, :])
        ext_ref[0:CARRY_ROWS, :] = ext_ref[L:L + CARRY_ROWS, :]
        xs = xbc[:, 0:W]
        b_b = xbc[:, W:W + N].astype(BF16)
        c_b = xbc[:, W + N:XBC].astype(BF16)

        dt_all = _softplus(dt_ref[rs, :] + pdt_ref[0:1, :])
        da_all = dt_all * (-jnp.exp(pdt_ref[1:2, :]))
        shift = (128 - g * HEADS_PER_GROUP) % 128
        dt_g = pltpu.roll(dt_all, shift, 1)
        da_g = pltpu.roll(da_all, shift, 1)

        hi = da_g.astype(BF16)
        r1 = da_g - hi.astype(F32)
        mid = r1.astype(BF16)
        lo = (r1 - mid.astype(F32)).astype(BF16)
        acum = (jnp.dot(tril, hi, preferred_element_type=F32)
                + jnp.dot(tril, mid, preferred_element_type=F32)
                + jnp.dot(tril, lo, preferred_element_type=F32))
        acum_t = acum.T
        a_last = acum[L - 1:L, :]

        def expand_heads(v):
            vh = v.astype(BF16)
            vl = (v - vh.astype(F32)).astype(BF16)
            return jnp.dot(vh, expand, preferred_element_type=F32) + jnp.dot(vl, expand, preferred_element_type=F32)

        dt_x = expand_heads(dt_g)
        ea_x = expand_heads(jnp.exp(acum))
        to_end_x = expand_heads(jnp.exp(a_last - acum))

        xdt = xs * dt_x
        xdt_b = xdt.astype(BF16)

        cb = lax.dot_general(c_b, b_b, (((1,), (1,)), ((), ())), preferred_element_type=F32)
        state = state_ref[...]
        y_off = jnp.dot(c_b, state.astype(BF16), preferred_element_type=F32) * ea_x

        y_parts = []
        for r in range(HEADS_PER_GROUP):
            seg = acum[:, r:r + 1] - acum_t[r:r + 1, :]
            decay = jnp.exp(jnp.where(causal, seg, -jnp.inf))
            m = (cb * decay).astype(BF16)
            y_parts.append(jnp.dot(m, xdt_b[:, r * HEAD_DIM:(r + 1) * HEAD_DIM], preferred_element_type=F32))
        y = jnp.concatenate(y_parts, axis=1) + y_off + pg_ref[_P_DSKIP:_P_DSKIP + 1, 0:W] * xs

        upd = lax.dot_general(b_b, (xdt * to_end_x).astype(BF16), (((0,), (0,)), ((), ())),
                              preferred_element_type=F32)
        state_ref[...] = state * ea_x[L - 1:L, :] + upd

        y = y * _silu(z_ref[rs, :].astype(F32))
        ms = jnp.mean(y * y, axis=-1, keepdims=True)
        o_ref[rs, :] = (y * lax.rsqrt(ms + RMS_EPS) * pg_ref[_P_NORM:_P_NORM + 1, 0:W]).astype(o_ref.dtype)
        return carry

    lax.fori_loop(0, chunks, chunk, 0)


def ssd_mixer(proj, dt_raw, conv_w, conv_b, dt_bias, a_log, d_skip, norm_w, *, batch, seq, col_off, chunks=8):
    t = batch * seq
    chunks = min(chunks, seq // CHUNK)
    nchunk = seq // (CHUNK * chunks)
    W, N, L, G = GROUP_W, D_STATE, CHUNK * chunks, N_GROUPS
    zb, xb, bb, cb_ = col_off["z"] // W, col_off["xs"] // W, col_off["B"] // N, col_off["C"] // N

    def rows(b, g, c):
        return b * nchunk + c

    def per_group(v):
        r = v.shape[0]
        x_part = v[:, :D_SSD].reshape(r, G, W)
        b_part = v[:, D_SSD:D_SSD + G * N].reshape(r, G, N)
        c_part = v[:, D_SSD + G * N:].reshape(r, G, N)
        return jnp.concatenate([x_part, b_part, c_part], axis=2).transpose(1, 0, 2)

    def x_only(v):
        return jnp.pad(v.astype(F32).reshape(G, 1, W), ((0, 0), (0, 0), (0, 2 * N)))

    pg = jnp.concatenate([per_group(conv_w.astype(F32)), per_group(conv_b.astype(F32).reshape(1, -1)),
                          x_only(jnp.repeat(d_skip, HEAD_DIM)), x_only(norm_w),
                          jnp.zeros((G, 1, W + 2 * N), F32)], axis=1)
    pad = 128 - N_HEADS
    pdt = jnp.pad(jnp.stack([dt_bias.astype(F32), a_log.astype(F32)]), ((0, 6), (0, pad)))

    in_specs = [
        pl.BlockSpec((L, W), lambda b, g, c: (rows(b, g, c), zb + g)),
        pl.BlockSpec((L, W), lambda b, g, c: (rows(b, g, c), xb + g)),
        pl.BlockSpec((L, N), lambda b, g, c: (rows(b, g, c), bb + g)),
        pl.BlockSpec((L, N), lambda b, g, c: (rows(b, g, c), cb_ + g)),
        pl.BlockSpec((L, 128), lambda b, g, c: (rows(b, g, c), 0)),
        pl.BlockSpec((None, 8, W + 2 * N), lambda b, g, c: (g, 0, 0)),
        pl.BlockSpec((8, 128), lambda b, g, c: (0, 0)),
    ]
    return pl.pallas_call(
        functools.partial(_ssd_kernel, chunks=chunks),
        grid=(batch, N_GROUPS, nchunk),
        in_specs=in_specs,
        out_specs=pl.BlockSpec((L, W), lambda b, g, c: (rows(b, g, c), g)),
        out_shape=jax.ShapeDtypeStruct((t, D_SSD), BF16),
        scratch_shapes=[pltpu.VMEM((N, W), F32), pltpu.VMEM((CHUNK + CARRY_ROWS, W + 2 * N), F32)],
        compiler_params=_cparams(("parallel", "parallel", "arbitrary")),
        name="ssd_mixer",
    )(proj, proj, proj, proj, dt_raw, pg, pdt)


def _shortconv_kernel(b_ref, c_ref, h_ref, w_ref, nw_ref, o_ref, ext_ref, y_ref, *, rows, col_chunk):
    s = pl.program_id(1)
    d = D_CONV

    @pl.when(s == 0)
    def _():
        ext_ref[0:CARRY_ROWS, :] = jnp.zeros((CARRY_ROWS, d), F32)

    ss = jnp.zeros((rows, 1), F32)
    for j in range(0, d, col_chunk):
        sl = slice(j, j + col_chunk)
        ext_ref[CARRY_ROWS:CARRY_ROWS + rows, sl] = c_ref[:, sl].astype(F32) * h_ref[:, sl].astype(F32)
        acc = ext_ref[CARRY_ROWS:CARRY_ROWS + rows, sl] * w_ref[2:3, sl]
        for k in range(2):
            acc = acc + ext_ref[CARRY_ROWS - 2 + k:CARRY_ROWS - 2 + k + rows, sl] * w_ref[k:k + 1, sl]
        y = b_ref[:, sl].astype(F32) * acc
        y_ref[:, sl] = y
        ss = ss + jnp.sum(y * y, axis=-1, keepdims=True)
    ext_ref[0:CARRY_ROWS, :] = ext_ref[rows:rows + CARRY_ROWS, :]
    inv = lax.rsqrt(ss * (1.0 / d) + RMS_EPS)
    for j in range(0, d, col_chunk):
        sl = slice(j, j + col_chunk)
        o_ref[:, sl] = (y_ref[:, sl] * inv * nw_ref[:, sl]).astype(o_ref.dtype)


def shortconv_mixer(proj, conv_w, norm_w, *, batch, seq, col_off, rows=256, col_chunk=512):
    t = batch * seq
    nblk = seq // rows
    d = D_CONV
    bb, cb_, hb = col_off["scb"] // d, col_off["scc"] // d, col_off["sch"] // d
    return pl.pallas_call(
        functools.partial(_shortconv_kernel, rows=rows, col_chunk=col_chunk),
        grid=(batch, nblk),
        in_specs=[
            pl.BlockSpec((rows, d), lambda b, s: (b * nblk + s, bb)),
            pl.BlockSpec((rows, d), lambda b, s: (b * nblk + s, cb_)),
            pl.BlockSpec((rows, d), lambda b, s: (b * nblk + s, hb)),
            pl.BlockSpec((3, d), lambda b, s: (0, 0)),
            pl.BlockSpec((1, d), lambda b, s: (0, 0)),
        ],
        out_specs=pl.BlockSpec((rows, d), lambda b, s: (b * nblk + s, 0)),
        out_shape=jax.ShapeDtypeStruct((t, d), BF16),
        scratch_shapes=[pltpu.VMEM((rows + CARRY_ROWS, d), F32), pltpu.VMEM((rows, d), F32)],
        compiler_params=_cparams(("parallel", "arbitrary")),
        name="shortconv_mixer",
    )(proj, proj, proj, conv_w.astype(F32), norm_w.astype(F32).reshape(1, d))


def _outproj_kernel(a1_ref, a2_ref, b1_ref, b2_ref, r_ref, u_ref, v_ref, o_ref, ub_ref, vb_ref):
    acc = jnp.dot(a1_ref[...], b1_ref[...], preferred_element_type=F32)
    acc = acc + jnp.dot(a2_ref[...], b2_ref[...], preferred_element_type=F32)
    o_ref[...] = acc + r_ref[...]
    ub_ref[...] = u_ref[...].astype(ub_ref.dtype)
    vb_ref[...] = v_ref[...].astype(vb_ref.dtype)


def out_projection(y_ssd, y_sc, w_out, x, u, v, *, tm=1024, tn=256):
    m, kh = y_ssd.shape
    _, n = w_out.shape
    e, d = u.shape
    tm = min(tm, m)
    nj = n // tn
    steps = (m // tm) * nj
    er = e // steps
    assert er * steps == e and er % 16 == 0, "PEER table rows must split evenly over the grid steps"
    tspec = pl.BlockSpec((er, d), lambda i, j: (i * nj + j, 0))
    return pl.pallas_call(
        _outproj_kernel,
        grid=(m // tm, nj),
        in_specs=[
            pl.BlockSpec((tm, kh), lambda i, j: (i, 0)),
            pl.BlockSpec((tm, kh), lambda i, j: (i, 0)),
            pl.BlockSpec((kh, tn), lambda i, j: (0, j)),
            pl.BlockSpec((kh, tn), lambda i, j: (1, j)),
            pl.BlockSpec((tm, tn), lambda i, j: (i, j)),
            tspec, tspec,
        ],
        out_specs=[pl.BlockSpec((tm, tn), lambda i, j: (i, j)), tspec, tspec],
        out_shape=[jax.ShapeDtypeStruct((m, n), F32), jax.ShapeDtypeStruct((e, d), BF16),
                   jax.ShapeDtypeStruct((e, d), BF16)],
        compiler_params=_cparams(("parallel", "parallel")),
        name="out_projection",
    )(y_ssd, y_sc, w_out, w_out, x, u, v)


def _top_values(x, n):
    outs = []
    for r in range(n):
        m = jnp.max(x, axis=0, keepdims=True)
        outs.append(m)
        if r < n - 1:
            x = jnp.where(x >= m, -jnp.inf, x)
    return outs


def _query_kernel(x_ref, w_ref, o_ref):
    acc = jnp.dot(x_ref[...], w_ref[...], preferred_element_type=F32)
    for c in range(o_ref.shape[0]):
        o_ref[c] = acc[:, c * N_KEYS:(c + 1) * N_KEYS].astype(o_ref.dtype)


def query_projection(xn, wq, *, tm=1024, tn=1024):
    t, d = xn.shape
    n = wq.shape[1]
    tm = min(tm, t)
    per = tn // N_KEYS
    return pl.pallas_call(
        _query_kernel,
        grid=(t // tm, n // tn),
        in_specs=[pl.BlockSpec((tm, d), lambda i, j: (i, 0)), pl.BlockSpec((d, tn), lambda i, j: (0, j))],
        out_specs=pl.BlockSpec((per, tm, N_KEYS), lambda i, j: (j, i, 0)),
        out_shape=jax.ShapeDtypeStruct((n // N_KEYS, t, N_KEYS), BF16),
        compiler_params=_cparams(("parallel", "parallel")),
        name="query_projection",
    )(xn, wq)


_R_PHI, _R_E0, _R_E1 = 0, 1, 2


def _route_kernel(q_ref, keys_ref, r_ref):
    nt = (((1,), (1,)), ((), ()))

    def head(h, carry):
        s0 = lax.dot_general(keys_ref[2 * h], q_ref[2 * h], nt, preferred_element_type=F32)
        s1 = lax.dot_general(keys_ref[2 * h + 1], q_ref[2 * h + 1], nt, preferred_element_type=F32)
        top0 = _top_values(s0, TOPK + 1)
        top1 = _top_values(s1, TOPK + 1)
        t1 = jnp.concatenate(top1, axis=0)
        pieces = [top0[r] + t1[0:(TOPK + 1) // (r + 1)] for r in range(TOPK + 1)]
        n_cand = sum(p.shape[0] for p in pieces)
        pieces.append(jnp.full((-n_cand % 8, s0.shape[1]), -jnp.inf, F32))
        cand = jnp.concatenate(pieces, axis=0)
        best = _top_values(cand, TOPK + 1)
        m = best[0]
        z = jnp.exp(best[0] - m)
        for r in range(1, TOPK):
            z = z + jnp.exp(best[r] - m)
        inv_z = 0.5 / z
        tau = 0.5 * (best[TOPK - 1] + best[TOPK])
        m0, m1 = top0[0], top1[0]
        r_ref[_R_E0, h] = jnp.exp(s0 - m0)
        r_ref[_R_E1, h] = jnp.exp(s1 - m1) * inv_z
        r_ref[_R_PHI, h] = jnp.exp((tau - m1) - s0) * inv_z
        return carry

    lax.fori_loop(0, PEER_HEADS, head, 0)


def peer_routing(q, keys, *, tb=512):
    _, t, _ = q.shape
    return pl.pallas_call(
        _route_kernel,
        grid=(t // tb,),
        in_specs=[pl.BlockSpec((2 * PEER_HEADS, tb, N_KEYS), lambda i: (0, i, 0)),
                  pl.BlockSpec((2 * PEER_HEADS, N_KEYS, N_KEYS), lambda i: (0, 0, 0))],
        out_specs=pl.BlockSpec((3, PEER_HEADS, N_KEYS, tb), lambda i: (0, 0, 0, i)),
        out_shape=jax.ShapeDtypeStruct((3, PEER_HEADS, N_KEYS, t), F32),
        compiler_params=_cparams(("parallel",)),
        name="peer_routing",
    )(q, keys)


def _gelu_times2(x):
    return x * (1.0 + lax.erf(x * (1.0 / math.sqrt(2.0))))


def _peer_dense_kernel(x_ref, u_ref, v_ref, r_ref, o_ref, a_ref, acc_ref, *, te):
    et = pl.program_id(1)
    nblk = te // N_KEYS
    tb = x_ref.shape[0]
    tq = 128
    nt = (((1,), (1,)), ((), ()))

    @pl.when(et == 0)
    def _():
        acc_ref[...] = jnp.zeros_like(acc_ref)

    ht = lax.dot_general(u_ref[...], x_ref[...], nt, preferred_element_type=F32)
    for j in range(nblk):
        i = et * nblk + j
        rows = slice(j * N_KEYS, (j + 1) * N_KEYS)
        phi_rows = [r_ref[_R_PHI, h, pl.ds(i, 1), :] for h in range(PEER_HEADS)]
        e0_rows = [r_ref[_R_E0, h, pl.ds(i, 1), :] for h in range(PEER_HEADS)]
        for q in range(tb // tq):
            ts = slice(q * tq, (q + 1) * tq)
            gate = None
            for h in range(PEER_HEADS):
                e1 = r_ref[_R_E1, h, :, ts]
                sel = jnp.where(e1 >= phi_rows[h][:, ts], e1, 0.0) * e0_rows[h][:, ts]
                gate = sel if gate is None else gate + sel
            act = _gelu_times2(ht[rows, ts]) * gate
            a_ref[ts, rows] = act.T.astype(a_ref.dtype)
    acc_ref[...] += jnp.dot(a_ref[...], v_ref[...], preferred_element_type=F32)

    @pl.when(et == pl.num_programs(1) - 1)
    def _():
        o_ref[...] = acc_ref[...].astype(o_ref.dtype)


def peer_dense(xn, u, v, r, *, tb=512, te=512):
    t, d = xn.shape
    e = u.shape[0]
    tb = min(tb, t)
    return pl.pallas_call(
        functools.partial(_peer_dense_kernel, te=te),
        grid=(t // tb, e // te),
        in_specs=[
            pl.BlockSpec((tb, d), lambda i, j: (i, 0)),
            pl.BlockSpec((te, d), lambda i, j: (j, 0)),
            pl.BlockSpec((te, d), lambda i, j: (j, 0)),
            pl.BlockSpec((3, PEER_HEADS, N_KEYS, tb), lambda i, j: (0, 0, 0, i)),
        ],
        out_specs=pl.BlockSpec((tb, d), lambda i, j: (i, 0)),
        out_shape=jax.ShapeDtypeStruct((t, d), BF16),
        scratch_shapes=[pltpu.VMEM((tb, te), BF16), pltpu.VMEM((tb, d), F32)],
        compiler_params=_cparams(("parallel", "arbitrary")),
        name="peer_dense",
    )(xn, u, v, r)


def _final_kernel(h_ref, p_ref, w_ref, o_ref):
    x = h_ref[...] + p_ref[...]
    ms = jnp.mean(x * x, axis=-1, keepdims=True)
    o_ref[...] = x * lax.rsqrt(ms + RMS_EPS) * w_ref[...]


def final_norm(h1, peer, w, *, tm=256):
    t, d = h1.shape
    return pl.pallas_call(
        _final_kernel,
        grid=(t // tm,),
        in_specs=[pl.BlockSpec((tm, d), lambda i: (i, 0)), pl.BlockSpec((tm, d), lambda i: (i, 0)),
                  pl.BlockSpec((1, d), lambda i: (0, 0))],
        out_specs=pl.BlockSpec((tm, d), lambda i: (i, 0)),
        out_shape=jax.ShapeDtypeStruct((t, d), F32),
        compiler_params=_cparams(("parallel",)),
        name="final_norm",
    )(h1, peer, w.reshape(1, d))


def kernel(x, w_in, ssd_conv_w, ssd_conv_b, ssd_dt_bias, ssd_a_log, ssd_d, ssd_norm_w, sc_conv_w, sc_norm_w, w_out,
           norm_mix_w, norm_ffn_w, peer_w_query, peer_sub_keys, peer_u, peer_v, norm_final_w):
    batch, seq, d = x.shape
    t = batch * seq
    h = x.reshape(t, d)
    assert w_in.shape[0] == 1, "single-layer block"
    o_c = 2 * D_SSD + N_GROUPS * D_STATE
    o2 = o_c + N_GROUPS * D_STATE
    o3 = o2 + N_HEADS
    ssd_off = {"z": 0, "xs": D_SSD, "B": 2 * D_SSD, "C": o_c}
    sc_off = {"scb": 0, "scc": D_CONV, "sch": 2 * D_CONV}
    wt = w_in[0].T.astype(BF16)
    w_sc = wt[o3:]
    keys = peer_sub_keys[0].reshape(2 * PEER_HEADS, N_KEYS, N_KEYS).astype(BF16)

    xn = rmsnorm_rows(h, norm_mix_w[0])
    proj_ssd = projection(xn, wt, n=o2, tm=1024, tn=1024, out_dtype=BF16, name="in_projection_ssd")
    proj_sc, w_out_b, wq_b = projection(xn, w_sc, n=w_sc.shape[0], tm=1024, tn=768, out_dtype=BF16,
                                        name="in_projection_sc", casts=(w_out[0], peer_w_query[0]))
    dt_raw = projection(xn, wt, n=128, n_start=o2, tm=1024, tn=128, out_dtype=F32, name="dt_projection")
    y_ssd = ssd_mixer(proj_ssd, dt_raw, ssd_conv_w[0], ssd_conv_b[0], ssd_dt_bias[0], ssd_a_log[0], ssd_d[0],
                      ssd_norm_w[0], batch=batch, seq=seq, col_off=ssd_off)
    y_sc = shortconv_mixer(proj_sc, sc_conv_w[0], sc_norm_w[0], batch=batch, seq=seq, col_off=sc_off)
    h, u_b, v_b = out_projection(y_ssd, y_sc, w_out_b, h, peer_u[0], peer_v[0])

    xn2 = rmsnorm_rows(h, norm_ffn_w[0])
    q = query_projection(xn2, wq_b)
    r = peer_routing(q, keys)
    peer = peer_dense(xn2, u_b, v_b, r)
    out = final_norm(h, peer, norm_final_w)
    return out.reshape(batch, seq, d)
```

```python
import functools
import math

import jax
import jax.numpy as jnp
from jax import lax
from jax.experimental import pallas as pl
from jax.experimental.pallas import tpu as pltpu

F32 = jnp.float32
BF16 = jnp.bfloat16

D_MODEL = 4096
D_SSD = 4096
D_CONV = 4096
HEAD_DIM = 64
N_HEADS = 64
N_GROUPS = 8
HEADS_PER_GROUP = 8
D_STATE = 128
CHUNK = 128
GROUP_W = HEADS_PER_GROUP * HEAD_DIM
PEER_HEADS = 8
N_KEYS = 128
N_EXPERTS = N_KEYS * N_KEYS
TOPK = 16
RMS_EPS = 1e-6

VMEM_LIMIT_BYTES = 58 * 1024 * 1024
CARRY_ROWS = 8


def _cparams(sem):
    return pltpu.CompilerParams(dimension_semantics=sem, vmem_limit_bytes=VMEM_LIMIT_BYTES)


_NT = (((1,), (1,)), ((), ()))


def _rms_bf16(x, w):
    ms = jnp.mean(x * x, axis=-1, keepdims=True)
    return (x * lax.rsqrt(ms + RMS_EPS) * w).astype(BF16)


def _proj_kernel(a_ref, b_ref, *refs):
    n_cast = len(refs) // 2
    src_refs, o_ref, dst_refs = refs[:n_cast], refs[n_cast], refs[n_cast + 1:]
    o_ref[...] = lax.dot_general(a_ref[...], b_ref[...], _NT, preferred_element_type=F32).astype(o_ref.dtype)
    for src, dst in zip(src_refs, dst_refs):
        dst[...] = src[...].astype(dst.dtype)


def projection(a, w_t, *, n, n_start=0, tm, tn, out_dtype, name, casts=()):
    m, kdim = a.shape
    tm = min(tm, m)
    j0 = n_start // tn
    nj = n // tn
    steps = (m // tm) * nj
    cast_specs = []
    for c in casts:
        rows = c.shape[0] // steps
        assert rows * steps == c.shape[0] and rows % 16 == 0, "side-cast rows must split evenly over the grid steps"
        cast_specs.append(pl.BlockSpec((rows, c.shape[1]), lambda i, j: (i * nj + j, 0)))
    out = pl.pallas_call(
        _proj_kernel,
        grid=(m // tm, nj),
        in_specs=[pl.BlockSpec((tm, kdim), lambda i, j: (i, 0)), pl.BlockSpec((tn, kdim), lambda i, j: (j0 + j, 0)),
                  *cast_specs],
        out_specs=[pl.BlockSpec((tm, tn), lambda i, j: (i, j)), *cast_specs],
        out_shape=[jax.ShapeDtypeStruct((m, n), out_dtype)] + [jax.ShapeDtypeStruct(c.shape, BF16) for c in casts],
        compiler_params=_cparams(("parallel", "parallel")),
        name=name,
    )(a, w_t, *casts)
    return out if casts else out[0]


def _rmsnorm_kernel(x_ref, w_ref, o_ref):
    o_ref[...] = _rms_bf16(x_ref[...], w_ref[...])


def rmsnorm_rows(x, w, *, tm=256):
    t, d = x.shape
    return pl.pallas_call(
        _rmsnorm_kernel,
        grid=(t // tm,),
        in_specs=[pl.BlockSpec((tm, d), lambda i: (i, 0)), pl.BlockSpec((1, d), lambda i: (0, 0))],
        out_specs=pl.BlockSpec((tm, d), lambda i: (i, 0)),
        out_shape=jax.ShapeDtypeStruct((t, d), BF16),
        compiler_params=_cparams(("parallel",)),
        name="rmsnorm_rows",
    )(x, w.reshape(1, d))


def _silu(x):
    hx = 0.5 * x
    return hx + hx * jnp.tanh(hx)


def _softplus(x):
    return jnp.maximum(x, 0.0) + jnp.log(1.0 + jnp.exp(-jnp.abs(x)))


_P_CONV, _P_BIAS, _P_DSKIP, _P_NORM = 0, 4, 5, 6


def _ssd_kernel(z_ref, xs_ref, b_ref, c_ref, dt_ref, pg_ref, pdt_ref, o_ref, state_ref, ext_ref, *, chunks):
    g = pl.program_id(1)
    c = pl.program_id(2)
    L = CHUNK
    W = GROUP_W
    N = D_STATE
    XBC = W + 2 * N

    @pl.when(c == 0)
    def _():
        state_ref[...] = jnp.zeros_like(state_ref)
        ext_ref[0:CARRY_ROWS, :] = jnp.zeros((CARRY_ROWS, XBC), F32)

    row = lax.broadcasted_iota(jnp.int32, (L, L), 0)
    col = lax.broadcasted_iota(jnp.int32, (L, L), 1)
    causal = row >= col
    tril = causal.astype(BF16)
    er = lax.broadcasted_iota(jnp.int32, (128, W), 0)
    ec = lax.broadcasted_iota(jnp.int32, (128, W), 1)
    expand = (er == ec // HEAD_DIM).astype(BF16)

    def chunk(ci, carry):
        rs = pl.ds(pl.multiple_of(ci * L, L), L)
        ext_ref[CARRY_ROWS:CARRY_ROWS + L, 0:W] = xs_ref[rs, :].astype(F32)
        ext_ref[CARRY_ROWS:CARRY_ROWS + L, W:W + N] = b_ref[rs, :].astype(F32)
        ext_ref[CARRY_ROWS:CARRY_ROWS + L, W + N:XBC] = c_ref[rs, :].astype(F32)
        acc = ext_ref[CARRY_ROWS:CARRY_ROWS + L, :] * pg_ref[_P_CONV + 3:_P_CONV + 4, :]
        for k in range(3):
            acc = acc + ext_ref[CARRY_ROWS - 3 + k:CARRY_ROWS - 3 + k + L, :] * pg_ref[_P_CONV + k:_P_CONV + k + 1, :]
        xbc = _silu(acc + pg_ref[_P_BIAS:_P_BIAS + 1, :])
        ext_ref[0:CARRY_ROWS, :] = ext_ref[L:L + CARRY_ROWS, :]
        xs = xbc[:, 0:W]
        b_b = xbc[:, W:W + N].astype(BF16)
        c_b = xbc[:, W + N:XBC].astype(BF16)

        dt_all = _softplus(dt_ref[rs, :] + pdt_ref[0:1, :])
        da_all = dt_all * (-jnp.exp(pdt_ref[1:2, :]))
        shift = (128 - g * HEADS_PER_GROUP) % 128
        dt_g = pltpu.roll(dt_all, shift, 1)
        da_g = pltpu.roll(da_all, shift, 1)

        hi = da_g.astype(BF16)
        r1 = da_g - hi.astype(F32)
        mid = r1.astype(BF16)
        lo = (r1 - mid.astype(F32)).astype(BF16)
        acum = (jnp.dot(tril, hi, preferred_element_type=F32)
                + jnp.dot(tril, mid, preferred_element_type=F32)
                + jnp.dot(tril, lo, preferred_element_type=F32))
        acum_t = acum.T
        a_last = acum[L - 1:L, :]

        def expand_heads(v):
            vh = v.astype(BF16)
            vl = (v - vh.astype(F32)).astype(BF16)
            return jnp.dot(vh, expand, preferred_element_type=F32) + jnp.dot(vl, expand, preferred_element_type=F32)

        dt_x = expand_heads(dt_g)
        ea_x = expand_heads(jnp.exp(acum))
        to_end_x = expand_heads(jnp.exp(a_last - acum))

        xdt = xs * dt_x
        xdt_b = xdt.astype(BF16)

        cb = lax.dot_general(c_b, b_b, (((1,), (1,)), ((), ())), preferred_element_type=F32)
        state = state_ref[...]
        y_off = jnp.dot(c_b, state.astype(BF16), preferred_element_type=F32) * ea_x

        y_parts = []
        for r in range(HEADS_PER_GROUP):
            seg = acum[:, r:r + 1] - acum_t[r:r + 1, :]
            decay = jnp.exp(jnp.where(causal, seg, -jnp.inf))
            m = (cb * decay).astype(BF16)
            y_parts.append(jnp.dot(m, xdt_b[:, r * HEAD_DIM:(r + 1) * HEAD_DIM], preferred_element_type=F32))
        y = jnp.concatenate(y_parts, axis=1) + y_off + pg_ref[_P_DSKIP:_P_DSKIP + 1, 0:W] * xs

        upd = lax.dot_general(b_b, (xdt * to_end_x).astype(BF16), (((0,), (0,)), ((), ())),
                              preferred_element_type=F32)
        state_ref[...] = state * ea_x[L - 1:L, :] + upd

        y = y * _silu(z_ref[rs, :].astype(F32))
        ms = jnp.mean(y * y, axis=-1, keepdims=True)
        o_ref[rs, :] = (y * lax.rsqrt(ms + RMS_EPS) * pg_ref[_P_NORM:_P_NORM + 1, 0:W]).astype(o_ref.dtype)
        return carry

    lax.fori_loop(0, chunks, chunk, 0, unroll=4)


def ssd_mixer(proj, dt_raw, conv_w, conv_b, dt_bias, a_log, d_skip, norm_w, *, batch, seq, col_off, chunks=8):
    t = batch * seq
    chunks = min(chunks, seq // CHUNK)
    nchunk = seq // (CHUNK * chunks)
    W, N, L, G = GROUP_W, D_STATE, CHUNK * chunks, N_GROUPS
    zb, xb, bb, cb_ = col_off["z"] // W, col_off["xs"] // W, col_off["B"] // N, col_off["C"] // N

    def rows(b, g, c):
        return b * nchunk + c

    def per_group(v):
        r = v.shape[0]
        x_part = v[:, :D_SSD].reshape(r, G, W)
        b_part = v[:, D_SSD:D_SSD + G * N].reshape(r, G, N)
        c_part = v[:, D_SSD + G * N:].reshape(r, G, N)
        return jnp.concatenate([x_part, b_part, c_part], axis=2).transpose(1, 0, 2)

    def x_only(v):
        return jnp.pad(v.astype(F32).reshape(G, 1, W), ((0, 0), (0, 0), (0, 2 * N)))

    pg = jnp.concatenate([per_group(conv_w.astype(F32)), per_group(conv_b.astype(F32).reshape(1, -1)),
                          x_only(jnp.repeat(d_skip, HEAD_DIM)), x_only(norm_w),
                          jnp.zeros((G, 1, W + 2 * N), F32)], axis=1)
    pad = 128 - N_HEADS
    pdt = jnp.pad(jnp.stack([dt_bias.astype(F32), a_log.astype(F32)]), ((0, 6), (0, pad)))

    in_specs = [
        pl.BlockSpec((L, W), lambda b, g, c: (rows(b, g, c), zb + g)),
        pl.BlockSpec((L, W), lambda b, g, c: (rows(b, g, c), xb + g)),
        pl.BlockSpec((L, N), lambda b, g, c: (rows(b, g, c), bb + g)),
        pl.BlockSpec((L, N), lambda b, g, c: (rows(b, g, c), cb_ + g)),
        pl.BlockSpec((L, 128), lambda b, g, c: (rows(b, g, c), 0)),
        pl.BlockSpec((None, 8, W + 2 * N), lambda b, g, c: (g, 0, 0)),
        pl.BlockSpec((8, 128), lambda b, g, c: (0, 0)),
    ]
    return pl.pallas_call(
        functools.partial(_ssd_kernel, chunks=chunks),
        grid=(batch, N_GROUPS, nchunk),
        in_specs=in_specs,
        out_specs=pl.BlockSpec((L, W), lambda b, g, c: (rows(b, g, c), g)),
        out_shape=jax.ShapeDtypeStruct((t, D_SSD), BF16),
        scratch_shapes=[pltpu.VMEM((N, W), F32), pltpu.VMEM((CHUNK + CARRY_ROWS, W + 2 * N), F32)],
        compiler_params=_cparams(("parallel", "parallel", "arbitrary")),
        name="ssd_mixer",
    )(proj, proj, proj, proj, dt_raw, pg, pdt)


def _shortconv_kernel(b_ref, c_ref, h_ref, w_ref, nw_ref, o_ref, ext_ref, y_ref, *, rows, col_chunk):
    s = pl.program_id(1)
    d = D_CONV

    @pl.when(s == 0)
    def _():
        ext_ref[0:CARRY_ROWS, :] = jnp.zeros((CARRY_ROWS, d), F32)

    ss = jnp.zeros((rows, 1), F32)
    for j in range(0, d, col_chunk):
        sl = slice(j, j + col_chunk)
        ext_ref[CARRY_ROWS:CARRY_ROWS + rows, sl] = c_ref[:, sl].astype(F32) * h_ref[:, sl].astype(F32)
        acc = ext_ref[CARRY_ROWS:CARRY_ROWS + rows, sl] * w_ref[2:3, sl]
        for k in range(2):
            acc = acc + ext_ref[CARRY_ROWS - 2 + k:CARRY_ROWS - 2 + k + rows, sl] * w_ref[k:k + 1, sl]
        y = b_ref[:, sl].astype(F32) * acc
        y_ref[:, sl] = y
        ss = ss + jnp.sum(y * y, axis=-1, keepdims=True)
    ext_ref[0:CARRY_ROWS, :] = ext_ref[rows:rows + CARRY_ROWS, :]
    inv = lax.rsqrt(ss * (1.0 / d) + RMS_EPS)
    for j in range(0, d, col_chunk):
        sl = slice(j, j + col_chunk)
        o_ref[:, sl] = (y_ref[:, sl] * inv * nw_ref[:, sl]).astype(o_ref.dtype)


def shortconv_mixer(proj, conv_w, norm_w, *, batch, seq, col_off, rows=256, col_chunk=512):
    t = batch * seq
    nblk = seq // rows
    d = D_CONV
    bb, cb_, hb = col_off["scb"] // d, col_off["scc"] // d, col_off["sch"] // d
    return pl.pallas_call(
        functools.partial(_shortconv_kernel, rows=rows, col_chunk=col_chunk),
        grid=(batch, nblk),
        in_specs=[
            pl.BlockSpec((rows, d), lambda b, s: (b * nblk + s, bb)),
            pl.BlockSpec((rows, d), lambda b, s: (b * nblk + s, cb_)),
            pl.BlockSpec((rows, d), lambda b, s: (b * nblk + s, hb)),
            pl.BlockSpec((3, d), lambda b, s: (0, 0)),
            pl.BlockSpec((1, d), lambda b, s: (0, 0)),
        ],
        out_specs=pl.BlockSpec((rows, d), lambda b, s: (b * nblk + s, 0)),
        out_shape=jax.ShapeDtypeStruct((t, d), BF16),
        scratch_shapes=[pltpu.VMEM((rows + CARRY_ROWS, d), F32), pltpu.VMEM((rows, d), F32)],
        compiler_params=_cparams(("parallel", "arbitrary")),
        name="shortconv_mixer",
    )(proj, proj, proj, conv_w.astype(F32), norm_w.astype(F32).reshape(1, d))


def _outproj_kernel(a1_ref, a2_ref, b1_ref, b2_ref, r_ref, u_ref, v_ref, o_ref, ub_ref, vb_ref):
    acc = jnp.dot(a1_ref[...], b1_ref[...], preferred_element_type=F32)
    acc = acc + jnp.dot(a2_ref[...], b2_ref[...], preferred_element_type=F32)
    o_ref[...] = acc + r_ref[...]
    ub_ref[...] = u_ref[...].astype(ub_ref.dtype)
    vb_ref[...] = v_ref[...].astype(vb_ref.dtype)


def out_projection(y_ssd, y_sc, w_out, x, u, v, *, tm=1024, tn=256):
    m, kh = y_ssd.shape
    _, n = w_out.shape
    e, d = u.shape
    tm = min(tm, m)
    nj = n // tn
    steps = (m // tm) * nj
    er = e // steps
    assert er * steps == e and er % 16 == 0, "PEER table rows must split evenly over the grid steps"
    tspec = pl.BlockSpec((er, d), lambda i, j: (i * nj + j, 0))
    return pl.pallas_call(
        _outproj_kernel,
        grid=(m // tm, nj),
        in_specs=[
            pl.BlockSpec((tm, kh), lambda i, j: (i, 0)),
            pl.BlockSpec((tm, kh), lambda i, j: (i, 0)),
            pl.BlockSpec((kh, tn), lambda i, j: (0, j)),
            pl.BlockSpec((kh, tn), lambda i, j: (1, j)),
            pl.BlockSpec((tm, tn), lambda i, j: (i, j)),
            tspec, tspec,
        ],
        out_specs=[pl.BlockSpec((tm, tn), lambda i, j: (i, j)), tspec, tspec],
        out_shape=[jax.ShapeDtypeStruct((m, n), F32), jax.ShapeDtypeStruct((e, d), BF16),
                   jax.ShapeDtypeStruct((e, d), BF16)],
        compiler_params=_cparams(("parallel", "parallel")),
        name="out_projection",
    )(y_ssd, y_sc, w_out, w_out, x, u, v)


def _top_values(x, n):
    outs = []
    for r in range(n):
        m = jnp.max(x, axis=0, keepdims=True)
        outs.append(m)
        if r < n - 1:
            x = jnp.where(x >= m, -jnp.inf, x)
    return outs


def _query_kernel(x_ref, w_ref, o_ref):
    acc = jnp.dot(x_ref[...], w_ref[...], preferred_element_type=F32)
    for c in range(o_ref.shape[0]):
        o_ref[c] = acc[:, c * N_KEYS:(c + 1) * N_KEYS].astype(o_ref.dtype)


def query_projection(xn, wq, *, tm=1024, tn=1024):
    t, d = xn.shape
    n = wq.shape[1]
    tm = min(tm, t)
    per = tn // N_KEYS
    return pl.pallas_call(
        _query_kernel,
        grid=(t // tm, n // tn),
        in_specs=[pl.BlockSpec((tm, d), lambda i, j: (i, 0)), pl.BlockSpec((d, tn), lambda i, j: (0, j))],
        out_specs=pl.BlockSpec((per, tm, N_KEYS), lambda i, j: (j, i, 0)),
        out_shape=jax.ShapeDtypeStruct((n // N_KEYS, t, N_KEYS), BF16),
        compiler_params=_cparams(("parallel", "parallel")),
        name="query_projection",
    )(xn, wq)


_R_PHI, _R_E0, _R_E1 = 0, 1, 2


def _route_kernel(q_ref, keys_ref, r_ref):
    nt = (((1,), (1,)), ((), ()))

    def head(h, carry):
        s0 = lax.dot_general(keys_ref[2 * h], q_ref[2 * h], nt, preferred_element_type=F32)
        s1 = lax.dot_general(keys_ref[2 * h + 1], q_ref[2 * h + 1], nt, preferred_element_type=F32)
        top0 = _top_values(s0, TOPK + 1)
        top1 = _top_values(s1, TOPK + 1)
        t1 = jnp.concatenate(top1, axis=0)
        pieces = [top0[r] + t1[0:(TOPK + 1) // (r + 1)] for r in range(TOPK + 1)]
        n_cand = sum(p.shape[0] for p in pieces)
        pieces.append(jnp.full((-n_cand % 8, s0.shape[1]), -jnp.inf, F32))
        cand = jnp.concatenate(pieces, axis=0)
        best = _top_values(cand, TOPK + 1)
        m = best[0]
        z = jnp.exp(best[0] - m)
        for r in range(1, TOPK):
            z = z + jnp.exp(best[r] - m)
        inv_z = 0.5 / z
        tau = 0.5 * (best[TOPK - 1] + best[TOPK])
        m0, m1 = top0[0], top1[0]
        r_ref[_R_E0, h] = jnp.exp(s0 - m0)
        r_ref[_R_E1, h] = jnp.exp(s1 - m1) * inv_z
        r_ref[_R_PHI, h] = jnp.exp((tau - m1) - s0) * inv_z
        return carry

    lax.fori_loop(0, PEER_HEADS, head, 0, unroll=2)


def peer_routing(q, keys, *, tb=512):
    _, t, _ = q.shape
    return pl.pallas_call(
        _route_kernel,
        grid=(t // tb,),
        in_specs=[pl.BlockSpec((2 * PEER_HEADS, tb, N_KEYS), lambda i: (0, i, 0)),
                  pl.BlockSpec((2 * PEER_HEADS, N_KEYS, N_KEYS), lambda i: (0, 0, 0))],
        out_specs=pl.BlockSpec((3, PEER_HEADS, N_KEYS, tb), lambda i: (0, 0, 0, i)),
        out_shape=jax.ShapeDtypeStruct((3, PEER_HEADS, N_KEYS, t), F32),
        compiler_params=_cparams(("parallel",)),
        name="peer_routing",
    )(q, keys)


def _gelu_times2(x):
    return x * (1.0 + lax.erf(x * (1.0 / math.sqrt(2.0))))


def _peer_dense_kernel(x_ref, u_ref, v_ref, r_ref, o_ref, a_ref, acc_ref, *, te):
    et = pl.program_id(1)
    nblk = te // N_KEYS
    tb = x_ref.shape[0]
    tq = 128
    nt = (((1,), (1,)), ((), ()))

    @pl.when(et == 0)
    def _():
        acc_ref[...] = jnp.zeros_like(acc_ref)

    ht = lax.dot_general(u_ref[...], x_ref[...], nt, preferred_element_type=F32)
    for j in range(nblk):
        i = et * nblk + j
        rows = slice(j * N_KEYS, (j + 1) * N_KEYS)
        phi_rows = [r_ref[_R_PHI, h, pl.ds(i, 1), :] for h in range(PEER_HEADS)]
        e0_rows = [r_ref[_R_E0, h, pl.ds(i, 1), :] for h in range(PEER_HEADS)]
        for q in range(tb // tq):
            ts = slice(q * tq, (q + 1) * tq)
            gate = None
            for h in range(PEER_HEADS):
                e1 = r_ref[_R_E1, h, :, ts]
                sel = jnp.where(e1 >= phi_rows[h][:, ts], e1, 0.0) * e0_rows[h][:, ts]
                gate = sel if gate is None else gate + sel
            act = _gelu_times2(ht[rows, ts]) * gate
            a_ref[ts, rows] = act.T.astype(a_ref.dtype)
    acc_ref[...] += jnp.dot(a_ref[...], v_ref[...], preferred_element_type=F32)

    @pl.when(et == pl.num_programs(1) - 1)
    def _():
        o_ref[...] = acc_ref[...].astype(o_ref.dtype)


def peer_dense(xn, u, v, r, *, tb=512, te=512):
    t, d = xn.shape
    e = u.shape[0]
    tb = min(tb, t)
    return pl.pallas_call(
        functools.partial(_peer_dense_kernel, te=te),
        grid=(t // tb, e // te),
        in_specs=[
            pl.BlockSpec((tb, d), lambda i, j: (i, 0)),
            pl.BlockSpec((te, d), lambda i, j: (j, 0)),
            pl.BlockSpec((te, d), lambda i, j: (j, 0)),
            pl.BlockSpec((3, PEER_HEADS, N_KEYS, tb), lambda i, j: (0, 0, 0, i)),
        ],
        out_specs=pl.BlockSpec((tb, d), lambda i, j: (i, 0)),
        out_shape=jax.ShapeDtypeStruct((t, d), BF16),
        scratch_shapes=[pltpu.VMEM((tb, te), BF16), pltpu.VMEM((tb, d), F32)],
        compiler_params=_cparams(("parallel", "arbitrary")),
        name="peer_dense",
    )(xn, u, v, r)


def _final_kernel(h_ref, p_ref, w_ref, o_ref):
    x = h_ref[...] + p_ref[...]
    ms = jnp.mean(x * x, axis=-1, keepdims=True)
    o_ref[...] = x * lax.rsqrt(ms + RMS_EPS) * w_ref[...]


def final_norm(h1, peer, w, *, tm=256):
    t, d = h1.shape
    return pl.pallas_call(
        _final_kernel,
        grid=(t // tm,),
        in_specs=[pl.BlockSpec((tm, d), lambda i: (i, 0)), pl.BlockSpec((tm, d), lambda i: (i, 0)),
                  pl.BlockSpec((1, d), lambda i: (0, 0))],
        out_specs=pl.BlockSpec((tm, d), lambda i: (i, 0)),
        out_shape=jax.ShapeDtypeStruct((t, d), F32),
        compiler_params=_cparams(("parallel",)),
        name="final_norm",
    )(h1, peer, w.reshape(1, d))


def kernel(x, w_in, ssd_conv_w, ssd_conv_b, ssd_dt_bias, ssd_a_log, ssd_d, ssd_norm_w, sc_conv_w, sc_norm_w, w_out,
           norm_mix_w, norm_ffn_w, peer_w_query, peer_sub_keys, peer_u, peer_v, norm_final_w):
    batch, seq, d = x.shape
    t = batch * seq
    h = x.reshape(t, d)
    assert w_in.shape[0] == 1, "single-layer block"
    o_c = 2 * D_SSD + N_GROUPS * D_STATE
    o2 = o_c + N_GROUPS * D_STATE
    o3 = o2 + N_HEADS
    ssd_off = {"z": 0, "xs": D_SSD, "B": 2 * D_SSD, "C": o_c}
    sc_off = {"scb": 0, "scc": D_CONV, "sch": 2 * D_CONV}
    wt = w_in[0].T.astype(BF16)
    w_sc = wt[o3:]
    keys = peer_sub_keys[0].reshape(2 * PEER_HEADS, N_KEYS, N_KEYS).astype(BF16)

    xn = rmsnorm_rows(h, norm_mix_w[0])
    proj_ssd = projection(xn, wt, n=o2, tm=1024, tn=1024, out_dtype=BF16, name="in_projection_ssd")
    proj_sc, w_out_b, wq_b = projection(xn, w_sc, n=w_sc.shape[0], tm=1024, tn=768, out_dtype=BF16,
                                        name="in_projection_sc", casts=(w_out[0], peer_w_query[0]))
    dt_raw = projection(xn, wt, n=128, n_start=o2, tm=1024, tn=128, out_dtype=F32, name="dt_projection")
    y_ssd = ssd_mixer(proj_ssd, dt_raw, ssd_conv_w[0], ssd_conv_b[0], ssd_dt_bias[0], ssd_a_log[0], ssd_d[0],
                      ssd_norm_w[0], batch=batch, seq=seq, col_off=ssd_off)
    y_sc = shortconv_mixer(proj_sc, sc_conv_w[0], sc_norm_w[0], batch=batch, seq=seq, col_off=sc_off)
    h, u_b, v_b = out_projection(y_ssd, y_sc, w_out_b, h, peer_u[0], peer_v[0])

    xn2 = rmsnorm_rows(h, norm_ffn_w[0])
    q = query_projection(xn2, wq_b)
    r = peer_routing(q, keys)
    peer = peer_dense(xn2, u_b, v_b, r)
    out = final_norm(h, peer, norm_final_w)
    return out.reshape(batch, seq, d)
```

```python
import functools
import math

import jax
import jax.numpy as jnp
from jax import lax
from jax.experimental import pallas as pl
from jax.experimental.pallas import tpu as pltpu

F32 = jnp.float32
BF16 = jnp.bfloat16

D_MODEL = 4096
D_SSD = 4096
D_CONV = 4096
HEAD_DIM = 64
N_HEADS = 64
N_GROUPS = 8
HEADS_PER_GROUP = 8
D_STATE = 128
CHUNK = 128
GROUP_W = HEADS_PER_GROUP * HEAD_DIM
PEER_HEADS = 8
N_KEYS = 128
N_EXPERTS = N_KEYS * N_KEYS
TOPK = 16
RMS_EPS = 1e-6

VMEM_LIMIT_BYTES = 58 * 1024 * 1024
CARRY_ROWS = 8


def _cparams(sem):
    return pltpu.CompilerParams(dimension_semantics=sem, vmem_limit_bytes=VMEM_LIMIT_BYTES)


_NT = (((1,), (1,)), ((), ()))


def _rms_bf16(x, w):
    ms = jnp.mean(x * x, axis=-1, keepdims=True)
    return (x * lax.rsqrt(ms + RMS_EPS) * w).astype(BF16)


def _proj_kernel(a_ref, b_ref, *refs):
    n_cast = len(refs) // 2
    src_refs, o_ref, dst_refs = refs[:n_cast], refs[n_cast], refs[n_cast + 1:]
    o_ref[...] = lax.dot_general(a_ref[...], b_ref[...], _NT, preferred_element_type=F32).astype(o_ref.dtype)
    for src, dst in zip(src_refs, dst_refs):
        dst[...] = src[...].astype(dst.dtype)


def projection(a, w_t, *, n, n_start=0, tm, tn, out_dtype, name, casts=()):
    m, kdim = a.shape
    tm = min(tm, m)
    j0 = n_start // tn
    nj = n // tn
    steps = (m // tm) * nj
    cast_specs = []
    for c in casts:
        rows = c.shape[0] // steps
        assert rows * steps == c.shape[0] and rows % 16 == 0, "side-cast rows must split evenly over the grid steps"
        cast_specs.append(pl.BlockSpec((rows, c.shape[1]), lambda i, j: (i * nj + j, 0)))
    out = pl.pallas_call(
        _proj_kernel,
        grid=(m // tm, nj),
        in_specs=[pl.BlockSpec((tm, kdim), lambda i, j: (i, 0)), pl.BlockSpec((tn, kdim), lambda i, j: (j0 + j, 0)),
                  *cast_specs],
        out_specs=[pl.BlockSpec((tm, tn), lambda i, j: (i, j)), *cast_specs],
        out_shape=[jax.ShapeDtypeStruct((m, n), out_dtype)] + [jax.ShapeDtypeStruct(c.shape, BF16) for c in casts],
        compiler_params=_cparams(("parallel", "parallel")),
        name=name,
    )(a, w_t, *casts)
    return out if casts else out[0]


def _rmsnorm_kernel(x_ref, w_ref, o_ref):
    o_ref[...] = _rms_bf16(x_ref[...], w_ref[...])


def rmsnorm_rows(x, w, *, tm=256):
    t, d = x.shape
    return pl.pallas_call(
        _rmsnorm_kernel,
        grid=(t // tm,),
        in_specs=[pl.BlockSpec((tm, d), lambda i: (i, 0)), pl.BlockSpec((1, d), lambda i: (0, 0))],
        out_specs=pl.BlockSpec((tm, d), lambda i: (i, 0)),
        out_shape=jax.ShapeDtypeStruct((t, d), BF16),
        compiler_params=_cparams(("parallel",)),
        name="rmsnorm_rows",
    )(x, w.reshape(1, d))


def _silu(x):
    hx = 0.5 * x
    return hx + hx * jnp.tanh(hx)


def _softplus(x):
    return jnp.maximum(x, 0.0) + jnp.log(1.0 + jnp.exp(-jnp.abs(x)))


_P_CONV, _P_BIAS, _P_DSKIP, _P_NORM = 0, 4, 5, 6


def _ssd_kernel(z_ref, xs_ref, b_ref, c_ref, dt_ref, pg_ref, pdt_ref, o_ref, state_ref, ext_ref, *, chunks):
    g = pl.program_id(1)
    c = pl.program_id(2)
    L = CHUNK
    W = GROUP_W
    N = D_STATE
    XBC = W + 2 * N

    @pl.when(c == 0)
    def _():
        state_ref[...] = jnp.zeros_like(state_ref)
        ext_ref[0:CARRY_ROWS, :] = jnp.zeros((CARRY_ROWS, XBC), F32)

    row = lax.broadcasted_iota(jnp.int32, (L, L), 0)
    col = lax.broadcasted_iota(jnp.int32, (L, L), 1)
    causal = row >= col
    tril = causal.astype(BF16)
    er = lax.broadcasted_iota(jnp.int32, (128, W), 0)
    ec = lax.broadcasted_iota(jnp.int32, (128, W), 1)
    expand = (er == ec // HEAD_DIM).astype(BF16)

    def chunk(ci, carry):
        rs = pl.ds(pl.multiple_of(ci * L, L), L)
        ext_ref[CARRY_ROWS:CARRY_ROWS + L, 0:W] = xs_ref[rs, :].astype(F32)
        ext_ref[CARRY_ROWS:CARRY_ROWS + L, W:W + N] = b_ref[rs, :].astype(F32)
        ext_ref[CARRY_ROWS:CARRY_ROWS + L, W + N:XBC] = c_ref[rs, :].astype(F32)
        acc = ext_ref[CARRY_ROWS:CARRY_ROWS + L, :] * pg_ref[_P_CONV + 3:_P_CONV + 4, :]
        for k in range(3):
            acc = acc + ext_ref[CARRY_ROWS - 3 + k:CARRY_ROWS - 3 + k + L, :] * pg_ref[_P_CONV + k:_P_CONV + k + 1, :]
        xbc = _silu(acc + pg_ref[_P_BIAS:_P_BIAS + 1, :])
        ext_ref[0:CARRY_ROWS, :] = ext_ref[L:L + CARRY_ROWS, :]
        xs = xbc[:, 0:W]
        b_b = xbc[:, W:W + N].astype(BF16)
        c_b = xbc[:, W + N:XBC].astype(BF16)

        dt_all = _softplus(dt_ref[rs, :] + pdt_ref[0:1, :])
        da_all = dt_all * (-jnp.exp(pdt_ref[1:2, :]))
        shift = (128 - g * HEADS_PER_GROUP) % 128
        dt_g = pltpu.roll(dt_all, shift, 1)
        da_g = pltpu.roll(da_all, shift, 1)

        hi = da_g.astype(BF16)
        r1 = da_g - hi.astype(F32)
        mid = r1.astype(BF16)
        lo = (r1 - mid.astype(F32)).astype(BF16)
        acum = (jnp.dot(tril, hi, preferred_element_type=F32)
                + jnp.dot(tril, mid, preferred_element_type=F32)
                + jnp.dot(tril, lo, preferred_element_type=F32))
        acum_t = acum.T
        a_last = acum[L - 1:L, :]

        def expand_heads(v):
            vh = v.astype(BF16)
            vl = (v - vh.astype(F32)).astype(BF16)
            return jnp.dot(vh, expand, preferred_element_type=F32) + jnp.dot(vl, expand, preferred_element_type=F32)

        dt_x = expand_heads(dt_g)
        ea_x = expand_heads(jnp.exp(acum))
        to_end_x = expand_heads(jnp.exp(a_last - acum))

        xdt = xs * dt_x
        xdt_b = xdt.astype(BF16)

        cb = lax.dot_general(c_b, b_b, (((1,), (1,)), ((), ())), preferred_element_type=F32)
        state = state_ref[...]
        y_off = jnp.dot(c_b, state.astype(BF16), preferred_element_type=F32) * ea_x

        y_parts = []
        for r in range(HEADS_PER_GROUP):
            seg = acum[:, r:r + 1] - acum_t[r:r + 1, :]
            decay = jnp.exp(jnp.where(causal, seg, -jnp.inf))
            m = (cb * decay).astype(BF16)
            y_parts.append(jnp.dot(m, xdt_b[:, r * HEAD_DIM:(r + 1) * HEAD_DIM], preferred_element_type=F32))
        y = jnp.concatenate(y_parts, axis=1) + y_off + pg_ref[_P_DSKIP:_P_DSKIP + 1, 0:W] * xs

        upd = lax.dot_general(b_b, (xdt * to_end_x).astype(BF16), (((0,), (0,)), ((), ())),
                              preferred_element_type=F32)
        state_ref[...] = state * ea_x[L - 1:L, :] + upd

        y = y * _silu(z_ref[rs, :].astype(F32))
        ms = jnp.mean(y * y, axis=-1, keepdims=True)
        o_ref[rs, :] = (y * lax.rsqrt(ms + RMS_EPS) * pg_ref[_P_NORM:_P_NORM + 1, 0:W]).astype(o_ref.dtype)
        return carry

    lax.fori_loop(0, chunks, chunk, 0, unroll=8)


def ssd_mixer(proj, dt_raw, conv_w, conv_b, dt_bias, a_log, d_skip, norm_w, *, batch, seq, col_off, chunks=8):
    t = batch * seq
    chunks = min(chunks, seq // CHUNK)
    nchunk = seq // (CHUNK * chunks)
    W, N, L, G = GROUP_W, D_STATE, CHUNK * chunks, N_GROUPS
    zb, xb, bb, cb_ = col_off["z"] // W, col_off["xs"] // W, col_off["B"] // N, col_off["C"] // N

    def rows(b, g, c):
        return b * nchunk + c

    def per_group(v):
        r = v.shape[0]
        x_part = v[:, :D_SSD].reshape(r, G, W)
        b_part = v[:, D_SSD:D_SSD + G * N].reshape(r, G, N)
        c_part = v[:, D_SSD + G * N:].reshape(r, G, N)
        return jnp.concatenate([x_part, b_part, c_part], axis=2).transpose(1, 0, 2)

    def x_only(v):
        return jnp.pad(v.astype(F32).reshape(G, 1, W), ((0, 0), (0, 0), (0, 2 * N)))

    pg = jnp.concatenate([per_group(conv_w.astype(F32)), per_group(conv_b.astype(F32).reshape(1, -1)),
                          x_only(jnp.repeat(d_skip, HEAD_DIM)), x_only(norm_w),
                          jnp.zeros((G, 1, W + 2 * N), F32)], axis=1)
    pad = 128 - N_HEADS
    pdt = jnp.pad(jnp.stack([dt_bias.astype(F32), a_log.astype(F32)]), ((0, 6), (0, pad)))

    in_specs = [
        pl.BlockSpec((L, W), lambda b, g, c: (rows(b, g, c), zb + g)),
        pl.BlockSpec((L, W), lambda b, g, c: (rows(b, g, c), xb + g)),
        pl.BlockSpec((L, N), lambda b, g, c: (rows(b, g, c), bb + g)),
        pl.BlockSpec((L, N), lambda b, g, c: (rows(b, g, c), cb_ + g)),
        pl.BlockSpec((L, 128), lambda b, g, c: (rows(b, g, c), 0)),
        pl.BlockSpec((None, 8, W + 2 * N), lambda b, g, c: (g, 0, 0)),
        pl.BlockSpec((8, 128), lambda b, g, c: (0, 0)),
    ]
    return pl.pallas_call(
        functools.partial(_ssd_kernel, chunks=chunks),
        grid=(batch, N_GROUPS, nchunk),
        in_specs=in_specs,
        out_specs=pl.BlockSpec((L, W), lambda b, g, c: (rows(b, g, c), g)),
        out_shape=jax.ShapeDtypeStruct((t, D_SSD), BF16),
        scratch_shapes=[pltpu.VMEM((N, W), F32), pltpu.VMEM((CHUNK + CARRY_ROWS, W + 2 * N), F32)],
        compiler_params=_cparams(("parallel", "parallel", "arbitrary")),
        name="ssd_mixer",
    )(proj, proj, proj, proj, dt_raw, pg, pdt)


def _shortconv_kernel(b_ref, c_ref, h_ref, w_ref, nw_ref, o_ref, ext_ref, y_ref, *, rows, col_chunk):
    s = pl.program_id(1)
    d = D_CONV

    @pl.when(s == 0)
    def _():
        ext_ref[0:CARRY_ROWS, :] = jnp.zeros((CARRY_ROWS, d), F32)

    ss = jnp.zeros((rows, 1), F32)
    for j in range(0, d, col_chunk):
        sl = slice(j, j + col_chunk)
        ext_ref[CARRY_ROWS:CARRY_ROWS + rows, sl] = c_ref[:, sl].astype(F32) * h_ref[:, sl].astype(F32)
        acc = ext_ref[CARRY_ROWS:CARRY_ROWS + rows, sl] * w_ref[2:3, sl]
        for k in range(2):
            acc = acc + ext_ref[CARRY_ROWS - 2 + k:CARRY_ROWS - 2 + k + rows, sl] * w_ref[k:k + 1, sl]
        y = b_ref[:, sl].astype(F32) * acc
        y_ref[:, sl] = y
        ss = ss + jnp.sum(y * y, axis=-1, keepdims=True)
    ext_ref[0:CARRY_ROWS, :] = ext_ref[rows:rows + CARRY_ROWS, :]
    inv = lax.rsqrt(ss * (1.0 / d) + RMS_EPS)
    for j in range(0, d, col_chunk):
        sl = slice(j, j + col_chunk)
        o_ref[:, sl] = (y_ref[:, sl] * inv * nw_ref[:, sl]).astype(o_ref.dtype)


def shortconv_mixer(proj, conv_w, norm_w, *, batch, seq, col_off, rows=256, col_chunk=512):
    t = batch * seq
    nblk = seq // rows
    d = D_CONV
    bb, cb_, hb = col_off["scb"] // d, col_off["scc"] // d, col_off["sch"] // d
    return pl.pallas_call(
        functools.partial(_shortconv_kernel, rows=rows, col_chunk=col_chunk),
        grid=(batch, nblk),
        in_specs=[
            pl.BlockSpec((rows, d), lambda b, s: (b * nblk + s, bb)),
            pl.BlockSpec((rows, d), lambda b, s: (b * nblk + s, cb_)),
            pl.BlockSpec((rows, d), lambda b, s: (b * nblk + s, hb)),
            pl.BlockSpec((3, d), lambda b, s: (0, 0)),
            pl.BlockSpec((1, d), lambda b, s: (0, 0)),
        ],
        out_specs=pl.BlockSpec((rows, d), lambda b, s: (b * nblk + s, 0)),
        out_shape=jax.ShapeDtypeStruct((t, d), BF16),
        scratch_shapes=[pltpu.VMEM((rows + CARRY_ROWS, d), F32), pltpu.VMEM((rows, d), F32)],
        compiler_params=_cparams(("parallel", "arbitrary")),
        name="shortconv_mixer",
    )(proj, proj, proj, conv_w.astype(F32), norm_w.astype(F32).reshape(1, d))


def _outproj_kernel(a1_ref, a2_ref, b1_ref, b2_ref, r_ref, u_ref, v_ref, o_ref, ub_ref, vb_ref):
    acc = jnp.dot(a1_ref[...], b1_ref[...], preferred_element_type=F32)
    acc = acc + jnp.dot(a2_ref[...], b2_ref[...], preferred_element_type=F32)
    o_ref[...] = acc + r_ref[...]
    ub_ref[...] = u_ref[...].astype(ub_ref.dtype)
    vb_ref[...] = v_ref[...].astype(vb_ref.dtype)


def out_projection(y_ssd, y_sc, w_out, x, u, v, *, tm=1024, tn=256):
    m, kh = y_ssd.shape
    _, n = w_out.shape
    e, d = u.shape
    tm = min(tm, m)
    nj = n // tn
    steps = (m // tm) * nj
    er = e // steps
    assert er * steps == e and er % 16 == 0, "PEER table rows must split evenly over the grid steps"
    tspec = pl.BlockSpec((er, d), lambda i, j: (i * nj + j, 0))
    return pl.pallas_call(
        _outproj_kernel,
        grid=(m // tm, nj),
        in_specs=[
            pl.BlockSpec((tm, kh), lambda i, j: (i, 0)),
            pl.BlockSpec((tm, kh), lambda i, j: (i, 0)),
            pl.BlockSpec((kh, tn), lambda i, j: (0, j)),
            pl.BlockSpec((kh, tn), lambda i, j: (1, j)),
            pl.BlockSpec((tm, tn), lambda i, j: (i, j)),
            tspec, tspec,
        ],
        out_specs=[pl.BlockSpec((tm, tn), lambda i, j: (i, j)), tspec, tspec],
        out_shape=[jax.ShapeDtypeStruct((m, n), F32), jax.ShapeDtypeStruct((e, d), BF16),
                   jax.ShapeDtypeStruct((e, d), BF16)],
        compiler_params=_cparams(("parallel", "parallel")),
        name="out_projection",
    )(y_ssd, y_sc, w_out, w_out, x, u, v)


def _top_values(x, n):
    outs = []
    for r in range(n):
        m = jnp.max(x, axis=0, keepdims=True)
        outs.append(m)
        if r < n - 1:
            x = jnp.where(x >= m, -jnp.inf, x)
    return outs


def _query_kernel(x_ref, w_ref, o_ref):
    acc = jnp.dot(x_ref[...], w_ref[...], preferred_element_type=F32)
    for c in range(o_ref.shape[0]):
        o_ref[c] = acc[:, c * N_KEYS:(c + 1) * N_KEYS].astype(o_ref.dtype)


def query_projection(xn, wq, *, tm=1024, tn=1024):
    t, d = xn.shape
    n = wq.shape[1]
    tm = min(tm, t)
    per = tn // N_KEYS
    return pl.pallas_call(
        _query_kernel,
        grid=(t // tm, n // tn),
        in_specs=[pl.BlockSpec((tm, d), lambda i, j: (i, 0)), pl.BlockSpec((d, tn), lambda i, j: (0, j))],
        out_specs=pl.BlockSpec((per, tm, N_KEYS), lambda i, j: (j, i, 0)),
        out_shape=jax.ShapeDtypeStruct((n // N_KEYS, t, N_KEYS), BF16),
        compiler_params=_cparams(("parallel", "parallel")),
        name="query_projection",
    )(xn, wq)


_R_PHI, _R_E0, _R_E1 = 0, 1, 2


def _route_kernel(q_ref, keys_ref, r_ref):
    nt = (((1,), (1,)), ((), ()))

    def head(h, carry):
        s0 = lax.dot_general(keys_ref[2 * h], q_ref[2 * h], nt, preferred_element_type=F32)
        s1 = lax.dot_general(keys_ref[2 * h + 1], q_ref[2 * h + 1], nt, preferred_element_type=F32)
        top0 = _top_values(s0, TOPK + 1)
        top1 = _top_values(s1, TOPK + 1)
        t1 = jnp.concatenate(top1, axis=0)
        pieces = [top0[r] + t1[0:(TOPK + 1) // (r + 1)] for r in range(TOPK + 1)]
        n_cand = sum(p.shape[0] for p in pieces)
        pieces.append(jnp.full((-n_cand % 8, s0.shape[1]), -jnp.inf, F32))
        cand = jnp.concatenate(pieces, axis=0)
        best = _top_values(cand, TOPK + 1)
        m = best[0]
        z = jnp.exp(best[0] - m)
        for r in range(1, TOPK):
            z = z + jnp.exp(best[r] - m)
        inv_z = 0.5 / z
        tau = 0.5 * (best[TOPK - 1] + best[TOPK])
        m0, m1 = top0[0], top1[0]
        r_ref[_R_E0, h] = jnp.exp(s0 - m0)
        r_ref[_R_E1, h] = jnp.exp(s1 - m1) * inv_z
        r_ref[_R_PHI, h] = jnp.exp((tau - m1) - s0) * inv_z
        return carry

    lax.fori_loop(0, PEER_HEADS, head, 0, unroll=4)


def peer_routing(q, keys, *, tb=512):
    _, t, _ = q.shape
    return pl.pallas_call(
        _route_kernel,
        grid=(t // tb,),
        in_specs=[pl.BlockSpec((2 * PEER_HEADS, tb, N_KEYS), lambda i: (0, i, 0)),
                  pl.BlockSpec((2 * PEER_HEADS, N_KEYS, N_KEYS), lambda i: (0, 0, 0))],
        out_specs=pl.BlockSpec((3, PEER_HEADS, N_KEYS, tb), lambda i: (0, 0, 0, i)),
        out_shape=jax.ShapeDtypeStruct((3, PEER_HEADS, N_KEYS, t), F32),
        compiler_params=_cparams(("parallel",)),
        name="peer_routing",
    )(q, keys)


def _gelu_times2(x):
    return x * (1.0 + lax.erf(x * (1.0 / math.sqrt(2.0))))


def _peer_dense_kernel(x_ref, u_ref, v_ref, r_ref, o_ref, a_ref, acc_ref, *, te):
    et = pl.program_id(1)
    nblk = te // N_KEYS
    tb = x_ref.shape[0]
    tq = 128
    nt = (((1,), (1,)), ((), ()))

    @pl.when(et == 0)
    def _():
        acc_ref[...] = jnp.zeros_like(acc_ref)

    ht = lax.dot_general(u_ref[...], x_ref[...], nt, preferred_element_type=F32)
    for j in range(nblk):
        i = et * nblk + j
        rows = slice(j * N_KEYS, (j + 1) * N_KEYS)
        phi_rows = [r_ref[_R_PHI, h, pl.ds(i, 1), :] for h in range(PEER_HEADS)]
        e0_rows = [r_ref[_R_E0, h, pl.ds(i, 1), :] for h in range(PEER_HEADS)]
        for q in range(tb // tq):
            ts = slice(q * tq, (q + 1) * tq)
            gate = None
            for h in range(PEER_HEADS):
                e1 = r_ref[_R_E1, h, :, ts]
                sel = jnp.where(e1 >= phi_rows[h][:, ts], e1, 0.0) * e0_rows[h][:, ts]
                gate = sel if gate is None else gate + sel
            act = _gelu_times2(ht[rows, ts]) * gate
            a_ref[ts, rows] = act.T.astype(a_ref.dtype)
    acc_ref[...] += jnp.dot(a_ref[...], v_ref[...], preferred_element_type=F32)

    @pl.when(et == pl.num_programs(1) - 1)
    def _():
        o_ref[...] = acc_ref[...].astype(o_ref.dtype)


def peer_dense(xn, u, v, r, *, tb=512, te=512):
    t, d = xn.shape
    e = u.shape[0]
    tb = min(tb, t)
    return pl.pallas_call(
        functools.partial(_peer_dense_kernel, te=te),
        grid=(t // tb, e // te),
        in_specs=[
            pl.BlockSpec((tb, d), lambda i, j: (i, 0)),
            pl.BlockSpec((te, d), lambda i, j: (j, 0)),
            pl.BlockSpec((te, d), lambda i, j: (j, 0)),
            pl.BlockSpec((3, PEER_HEADS, N_KEYS, tb), lambda i, j: (0, 0, 0, i)),
        ],
        out_specs=pl.BlockSpec((tb, d), lambda i, j: (i, 0)),
        out_shape=jax.ShapeDtypeStruct((t, d), BF16),
        scratch_shapes=[pltpu.VMEM((tb, te), BF16), pltpu.VMEM((tb, d), F32)],
        compiler_params=_cparams(("parallel", "arbitrary")),
        name="peer_dense",
    )(xn, u, v, r)


def _final_kernel(h_ref, p_ref, w_ref, o_ref):
    x = h_ref[...] + p_ref[...]
    ms = jnp.mean(x * x, axis=-1, keepdims=True)
    o_ref[...] = x * lax.rsqrt(ms + RMS_EPS) * w_ref[...]


def final_norm(h1, peer, w, *, tm=256):
    t, d = h1.shape
    return pl.pallas_call(
        _final_kernel,
        grid=(t // tm,),
        in_specs=[pl.BlockSpec((tm, d), lambda i: (i, 0)), pl.BlockSpec((tm, d), lambda i: (i, 0)),
                  pl.BlockSpec((1, d), lambda i: (0, 0))],
        out_specs=pl.BlockSpec((tm, d), lambda i: (i, 0)),
        out_shape=jax.ShapeDtypeStruct((t, d), F32),
        compiler_params=_cparams(("parallel",)),
        name="final_norm",
    )(h1, peer, w.reshape(1, d))


def kernel(x, w_in, ssd_conv_w, ssd_conv_b, ssd_dt_bias, ssd_a_log, ssd_d, ssd_norm_w, sc_conv_w, sc_norm_w, w_out,
           norm_mix_w, norm_ffn_w, peer_w_query, peer_sub_keys, peer_u, peer_v, norm_final_w):
    batch, seq, d = x.shape
    t = batch * seq
    h = x.reshape(t, d)
    assert w_in.shape[0] == 1, "single-layer block"
    o_c = 2 * D_SSD + N_GROUPS * D_STATE
    o2 = o_c + N_GROUPS * D_STATE
    o3 = o2 + N_HEADS
    ssd_off = {"z": 0, "xs": D_SSD, "B": 2 * D_SSD, "C": o_c}
    sc_off = {"scb": 0, "scc": D_CONV, "sch": 2 * D_CONV}
    wt = w_in[0].T.astype(BF16)
    w_sc = wt[o3:]
    keys = peer_sub_keys[0].reshape(2 * PEER_HEADS, N_KEYS, N_KEYS).astype(BF16)

    xn = rmsnorm_rows(h, norm_mix_w[0])
    proj_ssd = projection(xn, wt, n=o2, tm=1024, tn=1024, out_dtype=BF16, name="in_projection_ssd")
    proj_sc, w_out_b, wq_b = projection(xn, w_sc, n=w_sc.shape[0], tm=1024, tn=768, out_dtype=BF16,
                                        name="in_projection_sc", casts=(w_out[0], peer_w_query[0]))
    dt_raw = projection(xn, wt, n=128, n_start=o2, tm=1024, tn=128, out_dtype=F32, name="dt_projection")
    y_ssd = ssd_mixer(proj_ssd, dt_raw, ssd_conv_w[0], ssd_conv_b[0], ssd_dt_bias[0], ssd_a_log[0], ssd_d[0],
                      ssd_norm_w[0], batch=batch, seq=seq, col_off=ssd_off)
    y_sc = shortconv_mixer(proj_sc, sc_conv_w[0], sc_norm_w[0], batch=batch, seq=seq, col_off=sc_off)
    h, u_b, v_b = out_projection(y_ssd, y_sc, w_out_b, h, peer_u[0], peer_v[0])

    xn2 = rmsnorm_rows(h, norm_ffn_w[0])
    q = query_projection(xn2, wq_b)
    r = peer_routing(q, keys)
    peer = peer_dense(xn2, u_b, v_b, r)
    out = final_norm(h, peer, norm_final_w)
    return out.reshape(batch, seq, d)
```

```python
import functools
import math

import jax
import jax.numpy as jnp
from jax import lax
from jax.experimental import pallas as pl
from jax.experimental.pallas import tpu as pltpu

F32 = jnp.float32
BF16 = jnp.bfloat16

D_SSD = 4096
D_CONV = 4096
HEAD_DIM = 64
N_HEADS = 64
N_GROUPS = 8
HEADS_PER_GROUP = 8
D_STATE = 128
CHUNK = 128
GROUP_W = HEADS_PER_GROUP * HEAD_DIM
PEER_HEADS = 8
N_KEYS = 128
TOPK = 16
RMS_EPS = 1e-6

VMEM_LIMIT_BYTES = 58 * 1024 * 1024
CARRY_ROWS = 8


def _cparams(sem):
    return pltpu.CompilerParams(dimension_semantics=sem, vmem_limit_bytes=VMEM_LIMIT_BYTES)


_NT = (((1,), (1,)), ((), ()))


def _rms_bf16(x, w):
    ms = jnp.mean(x * x, axis=-1, keepdims=True)
    return (x * lax.rsqrt(ms + RMS_EPS) * w).astype(BF16)


def _proj_kernel(a_ref, b_ref, *refs):
    n_cast = len(refs) // 2
    src_refs, o_ref, dst_refs = refs[:n_cast], refs[n_cast], refs[n_cast + 1:]
    o_ref[...] = lax.dot_general(a_ref[...], b_ref[...], _NT, preferred_element_type=F32).astype(o_ref.dtype)
    for src, dst in zip(src_refs, dst_refs):
        dst[...] = src[...].astype(dst.dtype)


def projection(a, w_t, *, n, n_start=0, tm, tn, out_dtype, name, casts=()):
    m, kdim = a.shape
    tm = min(tm, m)
    j0 = n_start // tn
    nj = n // tn
    steps = (m // tm) * nj
    cast_specs = []
    for c in casts:
        rows = c.shape[0] // steps
        assert rows * steps == c.shape[0] and rows % 16 == 0, "side-cast rows must split evenly over the grid steps"
        cast_specs.append(pl.BlockSpec((rows, c.shape[1]), lambda i, j: (i * nj + j, 0)))
    out = pl.pallas_call(
        _proj_kernel,
        grid=(m // tm, nj),
        in_specs=[pl.BlockSpec((tm, kdim), lambda i, j: (i, 0)), pl.BlockSpec((tn, kdim), lambda i, j: (j0 + j, 0)),
                  *cast_specs],
        out_specs=[pl.BlockSpec((tm, tn), lambda i, j: (i, j)), *cast_specs],
        out_shape=[jax.ShapeDtypeStruct((m, n), out_dtype)] + [jax.ShapeDtypeStruct(c.shape, BF16) for c in casts],
        compiler_params=_cparams(("parallel", "parallel")),
        name=name,
    )(a, w_t, *casts)
    return out if casts else out[0]


def _rmsnorm_kernel(x_ref, w_ref, o_ref):
    o_ref[...] = _rms_bf16(x_ref[...], w_ref[...])


def rmsnorm_rows(x, w, *, tm=512):
    t, d = x.shape
    return pl.pallas_call(
        _rmsnorm_kernel,
        grid=(t // tm,),
        in_specs=[pl.BlockSpec((tm, d), lambda i: (i, 0)), pl.BlockSpec((1, d), lambda i: (0, 0))],
        out_specs=pl.BlockSpec((tm, d), lambda i: (i, 0)),
        out_shape=jax.ShapeDtypeStruct((t, d), BF16),
        compiler_params=_cparams(("parallel",)),
        name="rmsnorm_rows",
    )(x, w.reshape(1, d))


def _silu(x):
    hx = 0.5 * x
    return hx + hx * jnp.tanh(hx)


def _softplus(x):
    return jnp.maximum(x, 0.0) + jnp.log(1.0 + jnp.exp(-jnp.abs(x)))


_P_CONV, _P_BIAS, _P_DSKIP, _P_NORM = 0, 4, 5, 6


def _ssd_kernel(z_ref, xs_ref, b_ref, c_ref, dt_ref, pg_ref, pdt_ref, o_ref, state_ref, ext_ref, *, chunks):
    g = pl.program_id(1)
    c = pl.program_id(2)
    L = CHUNK
    W = GROUP_W
    N = D_STATE
    XBC = W + 2 * N

    @pl.when(c == 0)
    def _():
        state_ref[...] = jnp.zeros_like(state_ref)
        ext_ref[0:CARRY_ROWS, :] = jnp.zeros((CARRY_ROWS, XBC), F32)

    row = lax.broadcasted_iota(jnp.int32, (L, L), 0)
    col = lax.broadcasted_iota(jnp.int32, (L, L), 1)
    causal = row >= col
    tril = causal.astype(BF16)
    er = lax.broadcasted_iota(jnp.int32, (128, W), 0)
    ec = lax.broadcasted_iota(jnp.int32, (128, W), 1)
    expand = (er == ec // HEAD_DIM).astype(BF16)

    def chunk(ci, carry):
        rs = pl.ds(pl.multiple_of(ci * L, L), L)
        ext_ref[CARRY_ROWS:CARRY_ROWS + L, 0:W] = xs_ref[rs, :].astype(F32)
        ext_ref[CARRY_ROWS:CARRY_ROWS + L, W:W + N] = b_ref[rs, :].astype(F32)
        ext_ref[CARRY_ROWS:CARRY_ROWS + L, W + N:XBC] = c_ref[rs, :].astype(F32)
        acc = ext_ref[CARRY_ROWS:CARRY_ROWS + L, :] * pg_ref[_P_CONV + 3:_P_CONV + 4, :]
        for k in range(3):
            acc = acc + ext_ref[CARRY_ROWS - 3 + k:CARRY_ROWS - 3 + k + L, :] * pg_ref[_P_CONV + k:_P_CONV + k + 1, :]
        xbc = _silu(acc + pg_ref[_P_BIAS:_P_BIAS + 1, :])
        ext_ref[0:CARRY_ROWS, :] = ext_ref[L:L + CARRY_ROWS, :]
        xs = xbc[:, 0:W]
        b_b = xbc[:, W:W + N].astype(BF16)
        c_b = xbc[:, W + N:XBC].astype(BF16)

        dt_all = _softplus(dt_ref[rs, :] + pdt_ref[0:1, :])
        da_all = dt_all * (-jnp.exp(pdt_ref[1:2, :]))
        shift = (128 - g * HEADS_PER_GROUP) % 128
        dt_g = pltpu.roll(dt_all, shift, 1)
        da_g = pltpu.roll(da_all, shift, 1)

        hi = da_g.astype(BF16)
        r1 = da_g - hi.astype(F32)
        mid = r1.astype(BF16)
        lo = (r1 - mid.astype(F32)).astype(BF16)
        acum = (jnp.dot(tril, hi, preferred_element_type=F32)
                + jnp.dot(tril, mid, preferred_element_type=F32)
                + jnp.dot(tril, lo, preferred_element_type=F32))
        acum_t = acum.T
        a_last = acum[L - 1:L, :]

        def expand_heads(v):
            vh = v.astype(BF16)
            vl = (v - vh.astype(F32)).astype(BF16)
            return jnp.dot(vh, expand, preferred_element_type=F32) + jnp.dot(vl, expand, preferred_element_type=F32)

        dt_x = expand_heads(dt_g)
        ea_x = expand_heads(jnp.exp(acum))
        to_end_x = expand_heads(jnp.exp(a_last - acum))

        xdt = xs * dt_x
        xdt_b = xdt.astype(BF16)

        cb = lax.dot_general(c_b, b_b, (((1,), (1,)), ((), ())), preferred_element_type=F32)
        state = state_ref[...]
        y_off = jnp.dot(c_b, state.astype(BF16), preferred_element_type=F32) * ea_x

        y_parts = []
        for r in range(HEADS_PER_GROUP):
            seg = acum[:, r:r + 1] - acum_t[r:r + 1, :]
            decay = jnp.exp(jnp.where(causal, seg, -jnp.inf))
            m = (cb * decay).astype(BF16)
            y_parts.append(jnp.dot(m, xdt_b[:, r * HEAD_DIM:(r + 1) * HEAD_DIM], preferred_element_type=F32))
        y = jnp.concatenate(y_parts, axis=1) + y_off + pg_ref[_P_DSKIP:_P_DSKIP + 1, 0:W] * xs

        upd = lax.dot_general(b_b, (xdt * to_end_x).astype(BF16), (((0,), (0,)), ((), ())),
                              preferred_element_type=F32)
        state_ref[...] = state * ea_x[L - 1:L, :] + upd

        y = y * _silu(z_ref[rs, :].astype(F32))
        ms = jnp.mean(y * y, axis=-1, keepdims=True)
        o_ref[rs, :] = (y * lax.rsqrt(ms + RMS_EPS) * pg_ref[_P_NORM:_P_NORM + 1, 0:W]).astype(o_ref.dtype)
        return carry

    lax.fori_loop(0, chunks, chunk, 0, unroll=8)


def ssd_mixer(proj, dt_raw, conv_w, conv_b, dt_bias, a_log, d_skip, norm_w, *, batch, seq, col_off, chunks=16):
    t = batch * seq
    chunks = min(chunks, seq // CHUNK)
    nchunk = seq // (CHUNK * chunks)
    W, N, L, G = GROUP_W, D_STATE, CHUNK * chunks, N_GROUPS
    zb, xb, bb, cb_ = col_off["z"] // W, col_off["xs"] // W, col_off["B"] // N, col_off["C"] // N

    def rows(b, g, c):
        return b * nchunk + c

    def per_group(v):
        r = v.shape[0]
        x_part = v[:, :D_SSD].reshape(r, G, W)
        b_part = v[:, D_SSD:D_SSD + G * N].reshape(r, G, N)
        c_part = v[:, D_SSD + G * N:].reshape(r, G, N)
        return jnp.concatenate([x_part, b_part, c_part], axis=2).transpose(1, 0, 2)

    def x_only(v):
        return jnp.pad(v.astype(F32).reshape(G, 1, W), ((0, 0), (0, 0), (0, 2 * N)))

    pg = jnp.concatenate([per_group(conv_w.astype(F32)), per_group(conv_b.astype(F32).reshape(1, -1)),
                          x_only(jnp.repeat(d_skip, HEAD_DIM)), x_only(norm_w),
                          jnp.zeros((G, 1, W + 2 * N), F32)], axis=1)
    pad = 128 - N_HEADS
    pdt = jnp.pad(jnp.stack([dt_bias.astype(F32), a_log.astype(F32)]), ((0, 6), (0, pad)))

    in_specs = [
        pl.BlockSpec((L, W), lambda b, g, c: (rows(b, g, c), zb + g)),
        pl.BlockSpec((L, W), lambda b, g, c: (rows(b, g, c), xb + g)),
        pl.BlockSpec((L, N), lambda b, g, c: (rows(b, g, c), bb + g)),
        pl.BlockSpec((L, N), lambda b, g, c: (rows(b, g, c), cb_ + g)),
        pl.BlockSpec((L, 128), lambda b, g, c: (rows(b, g, c), 0)),
        pl.BlockSpec((None, 8, W + 2 * N), lambda b, g, c: (g, 0, 0)),
        pl.BlockSpec((8, 128), lambda b, g, c: (0, 0)),
    ]
    return pl.pallas_call(
        functools.partial(_ssd_kernel, chunks=chunks),
        grid=(batch, N_GROUPS, nchunk),
        in_specs=in_specs,
        out_specs=pl.BlockSpec((L, W), lambda b, g, c: (rows(b, g, c), g)),
        out_shape=jax.ShapeDtypeStruct((t, D_SSD), BF16),
        scratch_shapes=[pltpu.VMEM((N, W), F32), pltpu.VMEM((CHUNK + CARRY_ROWS, W + 2 * N), F32)],
        compiler_params=_cparams(("parallel", "parallel", "arbitrary")),
        name="ssd_mixer",
    )(proj, proj, proj, proj, dt_raw, pg, pdt)


def _shortconv_kernel(b_ref, c_ref, h_ref, w_ref, nw_ref, o_ref, ext_ref, y_ref, *, rows, col_chunk):
    s = pl.program_id(1)
    d = D_CONV

    @pl.when(s == 0)
    def _():
        ext_ref[0:CARRY_ROWS, :] = jnp.zeros((CARRY_ROWS, d), F32)

    ss = jnp.zeros((rows, 1), F32)
    for j in range(0, d, col_chunk):
        sl = slice(j, j + col_chunk)
        ext_ref[CARRY_ROWS:CARRY_ROWS + rows, sl] = c_ref[:, sl].astype(F32) * h_ref[:, sl].astype(F32)
        acc = ext_ref[CARRY_ROWS:CARRY_ROWS + rows, sl] * w_ref[2:3, sl]
        for k in range(2):
            acc = acc + ext_ref[CARRY_ROWS - 2 + k:CARRY_ROWS - 2 + k + rows, sl] * w_ref[k:k + 1, sl]
        y = b_ref[:, sl].astype(F32) * acc
        y_ref[:, sl] = y
        ss = ss + jnp.sum(y * y, axis=-1, keepdims=True)
    ext_ref[0:CARRY_ROWS, :] = ext_ref[rows:rows + CARRY_ROWS, :]
    inv = lax.rsqrt(ss * (1.0 / d) + RMS_EPS)
    for j in range(0, d, col_chunk):
        sl = slice(j, j + col_chunk)
        o_ref[:, sl] = (y_ref[:, sl] * inv * nw_ref[:, sl]).astype(o_ref.dtype)


def shortconv_mixer(proj, conv_w, norm_w, *, batch, seq, col_off, rows=512, col_chunk=512):
    t = batch * seq
    rows = min(rows, seq)
    nblk = seq // rows
    d = D_CONV
    bb, cb_, hb = col_off["scb"] // d, col_off["scc"] // d, col_off["sch"] // d
    return pl.pallas_call(
        functools.partial(_shortconv_kernel, rows=rows, col_chunk=col_chunk),
        grid=(batch, nblk),
        in_specs=[
            pl.BlockSpec((rows, d), lambda b, s: (b * nblk + s, bb)),
            pl.BlockSpec((rows, d), lambda b, s: (b * nblk + s, cb_)),
            pl.BlockSpec((rows, d), lambda b, s: (b * nblk + s, hb)),
            pl.BlockSpec((3, d), lambda b, s: (0, 0)),
            pl.BlockSpec((1, d), lambda b, s: (0, 0)),
        ],
        out_specs=pl.BlockSpec((rows, d), lambda b, s: (b * nblk + s, 0)),
        out_shape=jax.ShapeDtypeStruct((t, d), BF16),
        scratch_shapes=[pltpu.VMEM((rows + CARRY_ROWS, d), F32), pltpu.VMEM((rows, d), F32)],
        compiler_params=_cparams(("parallel", "arbitrary")),
        name="shortconv_mixer",
    )(proj, proj, proj, conv_w.astype(F32), norm_w.astype(F32).reshape(1, d))


def _outproj_kernel(a1_ref, a2_ref, b1_ref, b2_ref, r_ref, u_ref, v_ref, o_ref, ub_ref, vb_ref):
    acc = jnp.dot(a1_ref[...], b1_ref[...], preferred_element_type=F32)
    acc = acc + jnp.dot(a2_ref[...], b2_ref[...], preferred_element_type=F32)
    o_ref[...] = acc + r_ref[...]
    ub_ref[...] = u_ref[...].astype(ub_ref.dtype)
    vb_ref[...] = v_ref[...].astype(vb_ref.dtype)


def out_projection(y_ssd, y_sc, w_out, x, u, v, *, tm=1024, tn=256):
    m, kh = y_ssd.shape
    _, n = w_out.shape
    e, d = u.shape
    tm = min(tm, m)
    nj = n // tn
    steps = (m // tm) * nj
    er = e // steps
    assert er * steps == e and er % 16 == 0, "PEER table rows must split evenly over the grid steps"
    tspec = pl.BlockSpec((er, d), lambda i, j: (i * nj + j, 0))
    return pl.pallas_call(
        _outproj_kernel,
        grid=(m // tm, nj),
        in_specs=[
            pl.BlockSpec((tm, kh), lambda i, j: (i, 0)),
            pl.BlockSpec((tm, kh), lambda i, j: (i, 0)),
            pl.BlockSpec((kh, tn), lambda i, j: (0, j)),
            pl.BlockSpec((kh, tn), lambda i, j: (1, j)),
            pl.BlockSpec((tm, tn), lambda i, j: (i, j)),
            tspec, tspec,
        ],
        out_specs=[pl.BlockSpec((tm, tn), lambda i, j: (i, j)), tspec, tspec],
        out_shape=[jax.ShapeDtypeStruct((m, n), F32), jax.ShapeDtypeStruct((e, d), BF16),
                   jax.ShapeDtypeStruct((e, d), BF16)],
        compiler_params=_cparams(("parallel", "parallel")),
        name="out_projection",
    )(y_ssd, y_sc, w_out, w_out, x, u, v)


def _top_values(x, n):
    outs = []
    for r in range(n):
        m = jnp.max(x, axis=0, keepdims=True)
        outs.append(m)
        if r < n - 1:
            x = jnp.where(x >= m, -jnp.inf, x)
    return outs


def _query_kernel(x_ref, w_ref, o_ref):
    acc = jnp.dot(x_ref[...], w_ref[...], preferred_element_type=F32)
    for c in range(o_ref.shape[0]):
        o_ref[c] = acc[:, c * N_KEYS:(c + 1) * N_KEYS].astype(o_ref.dtype)


def query_projection(xn, wq, *, tm=1024, tn=1024):
    t, d = xn.shape
    n = wq.shape[1]
    tm = min(tm, t)
    per = tn // N_KEYS
    return pl.pallas_call(
        _query_kernel,
        grid=(t // tm, n // tn),
        in_specs=[pl.BlockSpec((tm, d), lambda i, j: (i, 0)), pl.BlockSpec((d, tn), lambda i, j: (0, j))],
        out_specs=pl.BlockSpec((per, tm, N_KEYS), lambda i, j: (j, i, 0)),
        out_shape=jax.ShapeDtypeStruct((n // N_KEYS, t, N_KEYS), BF16),
        compiler_params=_cparams(("parallel", "parallel")),
        name="query_projection",
    )(xn, wq)


_R_PHI, _R_E0, _R_E1 = 0, 1, 2


def _route_kernel(q_ref, keys_ref, r_ref):
    nt = (((1,), (1,)), ((), ()))

    def head(h, carry):
        s0 = lax.dot_general(keys_ref[2 * h], q_ref[2 * h], nt, preferred_element_type=F32)
        s1 = lax.dot_general(keys_ref[2 * h + 1], q_ref[2 * h + 1], nt, preferred_element_type=F32)
        top0 = _top_values(s0, TOPK + 1)
        top1 = _top_values(s1, TOPK + 1)
        t1 = jnp.concatenate(top1, axis=0)
        pieces = [top0[r] + t1[0:(TOPK + 1) // (r + 1)] for r in range(TOPK + 1)]
        n_cand = sum(p.shape[0] for p in pieces)
        pieces.append(jnp.full((-n_cand % 8, s0.shape[1]), -jnp.inf, F32))
        cand = jnp.concatenate(pieces, axis=0)
        best = _top_values(cand, TOPK + 1)
        m = best[0]
        z = jnp.exp(best[0] - m)
        for r in range(1, TOPK):
            z = z + jnp.exp(best[r] - m)
        inv_z = 0.5 / z
        tau = 0.5 * (best[TOPK - 1] + best[TOPK])
        m0, m1 = top0[0], top1[0]
        r_ref[_R_E0, h] = jnp.exp(s0 - m0)
        r_ref[_R_E1, h] = jnp.exp(s1 - m1) * inv_z
        r_ref[_R_PHI, h] = jnp.exp((tau - m1) - s0) * inv_z
        return carry

    lax.fori_loop(0, PEER_HEADS, head, 0, unroll=8)


def peer_routing(q, keys, *, tb=512):
    _, t, _ = q.shape
    return pl.pallas_call(
        _route_kernel,
        grid=(t // tb,),
        in_specs=[pl.BlockSpec((2 * PEER_HEADS, tb, N_KEYS), lambda i: (0, i, 0)),
                  pl.BlockSpec((2 * PEER_HEADS, N_KEYS, N_KEYS), lambda i: (0, 0, 0))],
        out_specs=pl.BlockSpec((3, PEER_HEADS, N_KEYS, tb), lambda i: (0, 0, 0, i)),
        out_shape=jax.ShapeDtypeStruct((3, PEER_HEADS, N_KEYS, t), F32),
        compiler_params=_cparams(("parallel",)),
        name="peer_routing",
    )(q, keys)


def _gelu_times2(x):
    return x * (1.0 + lax.erf(x * (1.0 / math.sqrt(2.0))))


def _peer_dense_kernel(x_ref, u_ref, v_ref, r_ref, o_ref, a_ref, acc_ref, *, te):
    et = pl.program_id(1)
    nblk = te // N_KEYS
    tb = x_ref.shape[0]
    tq = 128
    nt = (((1,), (1,)), ((), ()))

    @pl.when(et == 0)
    def _():
        acc_ref[...] = jnp.zeros_like(acc_ref)

    ht = lax.dot_general(u_ref[...], x_ref[...], nt, preferred_element_type=F32)
    for j in range(nblk):
        i = et * nblk + j
        rows = slice(j * N_KEYS, (j + 1) * N_KEYS)
        phi_rows = [r_ref[_R_PHI, h, pl.ds(i, 1), :] for h in range(PEER_HEADS)]
        e0_rows = [r_ref[_R_E0, h, pl.ds(i, 1), :] for h in range(PEER_HEADS)]
        for q in range(tb // tq):
            ts = slice(q * tq, (q + 1) * tq)
            gate = None
            for h in range(PEER_HEADS):
                e1 = r_ref[_R_E1, h, :, ts]
                sel = jnp.where(e1 >= phi_rows[h][:, ts], e1, 0.0) * e0_rows[h][:, ts]
                gate = sel if gate is None else gate + sel
            act = _gelu_times2(ht[rows, ts]) * gate
            a_ref[ts, rows] = act.T.astype(a_ref.dtype)
    acc_ref[...] += jnp.dot(a_ref[...], v_ref[...], preferred_element_type=F32)

    @pl.when(et == pl.num_programs(1) - 1)
    def _():
        o_ref[...] = acc_ref[...].astype(o_ref.dtype)


def peer_dense(xn, u, v, r, *, tb=512, te=512):
    t, d = xn.shape
    e = u.shape[0]
    tb = min(tb, t)
    return pl.pallas_call(
        functools.partial(_peer_dense_kernel, te=te),
        grid=(t // tb, e // te),
        in_specs=[
            pl.BlockSpec((tb, d), lambda i, j: (i, 0)),
            pl.BlockSpec((te, d), lambda i, j: (j, 0)),
            pl.BlockSpec((te, d), lambda i, j: (j, 0)),
            pl.BlockSpec((3, PEER_HEADS, N_KEYS, tb), lambda i, j: (0, 0, 0, i)),
        ],
        out_specs=pl.BlockSpec((tb, d), lambda i, j: (i, 0)),
        out_shape=jax.ShapeDtypeStruct((t, d), BF16),
        scratch_shapes=[pltpu.VMEM((tb, te), BF16), pltpu.VMEM((tb, d), F32)],
        compiler_params=_cparams(("parallel", "arbitrary")),
        name="peer_dense",
    )(xn, u, v, r)


def _final_kernel(h_ref, p_ref, w_ref, o_ref):
    x = h_ref[...] + p_ref[...]
    ms = jnp.mean(x * x, axis=-1, keepdims=True)
    o_ref[...] = x * lax.rsqrt(ms + RMS_EPS) * w_ref[...]


def final_norm(h1, peer, w, *, tm=512):
    t, d = h1.shape
    return pl.pallas_call(
        _final_kernel,
        grid=(t // tm,),
        in_specs=[pl.BlockSpec((tm, d), lambda i: (i, 0)), pl.BlockSpec((tm, d), lambda i: (i, 0)),
                  pl.BlockSpec((1, d), lambda i: (0, 0))],
        out_specs=pl.BlockSpec((tm, d), lambda i: (i, 0)),
        out_shape=jax.ShapeDtypeStruct((t, d), F32),
        compiler_params=_cparams(("parallel",)),
        name="final_norm",
    )(h1, peer, w.reshape(1, d))


def kernel(x, w_in, ssd_conv_w, ssd_conv_b, ssd_dt_bias, ssd_a_log, ssd_d, ssd_norm_w, sc_conv_w, sc_norm_w, w_out,
           norm_mix_w, norm_ffn_w, peer_w_query, peer_sub_keys, peer_u, peer_v, norm_final_w):
    batch, seq, d = x.shape
    t = batch * seq
    h = x.reshape(t, d)
    assert w_in.shape[0] == 1, "single-layer block"
    o_c = 2 * D_SSD + N_GROUPS * D_STATE
    o2 = o_c + N_GROUPS * D_STATE
    o3 = o2 + N_HEADS
    ssd_off = {"z": 0, "xs": D_SSD, "B": 2 * D_SSD, "C": o_c}
    sc_off = {"scb": 0, "scc": D_CONV, "sch": 2 * D_CONV}
    wt = w_in[0].T.astype(BF16)
    w_sc = wt[o3:]
    keys = peer_sub_keys[0].reshape(2 * PEER_HEADS, N_KEYS, N_KEYS).astype(BF16)

    xn = rmsnorm_rows(h, norm_mix_w[0])
    proj_ssd = projection(xn, wt, n=o2, tm=1024, tn=1024, out_dtype=BF16, name="in_projection_ssd")
    proj_sc, w_out_b, wq_b = projection(xn, w_sc, n=w_sc.shape[0], tm=1024, tn=768, out_dtype=BF16,
                                        name="in_projection_sc", casts=(w_out[0], peer_w_query[0]))
    dt_raw = projection(xn, wt, n=128, n_start=o2, tm=1024, tn=128, out_dtype=F32, name="dt_projection")
    y_ssd = ssd_mixer(proj_ssd, dt_raw, ssd_conv_w[0], ssd_conv_b[0], ssd_dt_bias[0], ssd_a_log[0], ssd_d[0],
                      ssd_norm_w[0], batch=batch, seq=seq, col_off=ssd_off)
    y_sc = shortconv_mixer(proj_sc, sc_conv_w[0], sc_norm_w[0], batch=batch, seq=seq, col_off=sc_off)
    h, u_b, v_b = out_projection(y_ssd, y_sc, w_out_b, h, peer_u[0], peer_v[0])

    xn2 = rmsnorm_rows(h, norm_ffn_w[0])
    q = query_projection(xn2, wq_b)
    r = peer_routing(q, keys)
    peer = peer_dense(xn2, u_b, v_b, r)
    out = final_norm(h, peer, norm_final_w)
    return out.reshape(batch, seq, d)
```

```python
import functools
import math

import jax
import jax.numpy as jnp
from jax import lax
from jax.experimental import pallas as pl
from jax.experimental.pallas import tpu as pltpu

F32 = jnp.float32
BF16 = jnp.bfloat16

D_SSD = 4096
D_CONV = 4096
HEAD_DIM = 64
N_HEADS = 64
N_GROUPS = 8
HEADS_PER_GROUP = 8
D_STATE = 128
CHUNK = 128
GROUP_W = HEADS_PER_GROUP * HEAD_DIM
PEER_HEADS = 8
N_KEYS = 128
TOPK = 16
RMS_EPS = 1e-6

VMEM_LIMIT_BYTES = 58 * 1024 * 1024
CARRY_ROWS = 8


def _cparams(sem):
    return pltpu.CompilerParams(dimension_semantics=sem, vmem_limit_bytes=VMEM_LIMIT_BYTES)


_NT = (((1,), (1,)), ((), ()))


def _rms_bf16(x, w):
    ms = jnp.mean(x * x, axis=-1, keepdims=True)
    return (x * lax.rsqrt(ms + RMS_EPS) * w).astype(BF16)


def _proj_kernel(a_ref, b_ref, *refs):
    n_cast = len(refs) // 2
    src_refs, o_ref, dst_refs = refs[:n_cast], refs[n_cast], refs[n_cast + 1:]
    o_ref[...] = lax.dot_general(a_ref[...], b_ref[...], _NT, preferred_element_type=F32).astype(o_ref.dtype)
    for src, dst in zip(src_refs, dst_refs):
        dst[...] = src[...].astype(dst.dtype)


def projection(a, w_t, *, n, n_start=0, tm, tn, out_dtype, name, casts=()):
    m, kdim = a.shape
    tm = min(tm, m)
    j0 = n_start // tn
    nj = n // tn
    steps = (m // tm) * nj
    cast_specs = []
    for c in casts:
        rows = c.shape[0] // steps
        assert rows * steps == c.shape[0] and rows % 16 == 0, "side-cast rows must split evenly over the grid steps"
        cast_specs.append(pl.BlockSpec((rows, c.shape[1]), lambda i, j: (i * nj + j, 0)))
    out = pl.pallas_call(
        _proj_kernel,
        grid=(m // tm, nj),
        in_specs=[pl.BlockSpec((tm, kdim), lambda i, j: (i, 0)), pl.BlockSpec((tn, kdim), lambda i, j: (j0 + j, 0)),
                  *cast_specs],
        out_specs=[pl.BlockSpec((tm, tn), lambda i, j: (i, j)), *cast_specs],
        out_shape=[jax.ShapeDtypeStruct((m, n), out_dtype)] + [jax.ShapeDtypeStruct(c.shape, BF16) for c in casts],
        compiler_params=_cparams(("parallel", "parallel")),
        name=name,
    )(a, w_t, *casts)
    return out if casts else out[0]


def _rmsnorm_kernel(x_ref, w_ref, o_ref):
    o_ref[...] = _rms_bf16(x_ref[...], w_ref[...])


def rmsnorm_rows(x, w, *, tm=512):
    t, d = x.shape
    return pl.pallas_call(
        _rmsnorm_kernel,
        grid=(t // tm,),
        in_specs=[pl.BlockSpec((tm, d), lambda i: (i, 0)), pl.BlockSpec((1, d), lambda i: (0, 0))],
        out_specs=pl.BlockSpec((tm, d), lambda i: (i, 0)),
        out_shape=jax.ShapeDtypeStruct((t, d), BF16),
        compiler_params=_cparams(("parallel",)),
        name="rmsnorm_rows",
    )(x, w.reshape(1, d))


def _silu(x):
    hx = 0.5 * x
    return hx + hx * jnp.tanh(hx)


def _softplus(x):
    return jnp.maximum(x, 0.0) + jnp.log(1.0 + jnp.exp(-jnp.abs(x)))


_P_CONV, _P_BIAS, _P_DSKIP, _P_NORM = 0, 4, 5, 6


def _ssd_kernel(z_ref, xs_ref, b_ref, c_ref, dt_ref, pg_ref, pdt_ref, o_ref, state_ref, ext_ref, *, chunks):
    g = pl.program_id(1)
    c = pl.program_id(2)
    L = CHUNK
    W = GROUP_W
    N = D_STATE
    XBC = W + 2 * N

    @pl.when(c == 0)
    def _():
        state_ref[...] = jnp.zeros_like(state_ref)
        ext_ref[0:CARRY_ROWS, :] = jnp.zeros((CARRY_ROWS, XBC), F32)

    row = lax.broadcasted_iota(jnp.int32, (L, L), 0)
    col = lax.broadcasted_iota(jnp.int32, (L, L), 1)
    causal = row >= col
    tril = causal.astype(BF16)
    er = lax.broadcasted_iota(jnp.int32, (128, W), 0)
    ec = lax.broadcasted_iota(jnp.int32, (128, W), 1)
    expand = (er == ec // HEAD_DIM).astype(BF16)

    def chunk(ci, carry):
        rs = pl.ds(pl.multiple_of(ci * L, L), L)
        ext_ref[CARRY_ROWS:CARRY_ROWS + L, 0:W] = xs_ref[rs, :].astype(F32)
        ext_ref[CARRY_ROWS:CARRY_ROWS + L, W:W + N] = b_ref[rs, :].astype(F32)
        ext_ref[CARRY_ROWS:CARRY_ROWS + L, W + N:XBC] = c_ref[rs, :].astype(F32)
        acc = ext_ref[CARRY_ROWS:CARRY_ROWS + L, :] * pg_ref[_P_CONV + 3:_P_CONV + 4, :]
        for k in range(3):
            acc = acc + ext_ref[CARRY_ROWS - 3 + k:CARRY_ROWS - 3 + k + L, :] * pg_ref[_P_CONV + k:_P_CONV + k + 1, :]
        xbc = _silu(acc + pg_ref[_P_BIAS:_P_BIAS + 1, :])
        ext_ref[0:CARRY_ROWS, :] = ext_ref[L:L + CARRY_ROWS, :]
        xs = xbc[:, 0:W]
        b_b = xbc[:, W:W + N].astype(BF16)
        c_b = xbc[:, W + N:XBC].astype(BF16)

        dt_all = _softplus(dt_ref[rs, :] + pdt_ref[0:1, :])
        da_all = dt_all * (-jnp.exp(pdt_ref[1:2, :]))
        shift = (128 - g * HEADS_PER_GROUP) % 128
        dt_g = pltpu.roll(dt_all, shift, 1)
        da_g = pltpu.roll(da_all, shift, 1)

        hi = da_g.astype(BF16)
        r1 = da_g - hi.astype(F32)
        mid = r1.astype(BF16)
        lo = (r1 - mid.astype(F32)).astype(BF16)
        acum = (jnp.dot(tril, hi, preferred_element_type=F32)
                + jnp.dot(tril, mid, preferred_element_type=F32)
                + jnp.dot(tril, lo, preferred_element_type=F32))
        acum_t = acum.T
        a_last = acum[L - 1:L, :]

        def expand_heads(v):
            vh = v.astype(BF16)
            vl = (v - vh.astype(F32)).astype(BF16)
            return jnp.dot(vh, expand, preferred_element_type=F32) + jnp.dot(vl, expand, preferred_element_type=F32)

        dt_x = expand_heads(dt_g)
        ea_x = expand_heads(jnp.exp(acum))
        to_end_x = expand_heads(jnp.exp(a_last - acum))

        xdt = xs * dt_x
        xdt_b = xdt.astype(BF16)

        cb = lax.dot_general(c_b, b_b, (((1,), (1,)), ((), ())), preferred_element_type=F32)
        state = state_ref[...]
        y_off = jnp.dot(c_b, state.astype(BF16), preferred_element_type=F32) * ea_x

        y_parts = []
        for r in range(HEADS_PER_GROUP):
            seg = acum[:, r:r + 1] - acum_t[r:r + 1, :]
            decay = jnp.exp(jnp.where(causal, seg, -jnp.inf))
            m = (cb * decay).astype(BF16)
            y_parts.append(jnp.dot(m, xdt_b[:, r * HEAD_DIM:(r + 1) * HEAD_DIM], preferred_element_type=F32))
        y = jnp.concatenate(y_parts, axis=1) + y_off + pg_ref[_P_DSKIP:_P_DSKIP + 1, 0:W] * xs

        upd = lax.dot_general(b_b, (xdt * to_end_x).astype(BF16), (((0,), (0,)), ((), ())),
                              preferred_element_type=F32)
        state_ref[...] = state * ea_x[L - 1:L, :] + upd

        y = y * _silu(z_ref[rs, :].astype(F32))
        ms = jnp.mean(y * y, axis=-1, keepdims=True)
        o_ref[rs, :] = (y * lax.rsqrt(ms + RMS_EPS) * pg_ref[_P_NORM:_P_NORM + 1, 0:W]).astype(o_ref.dtype)
        return carry

    lax.fori_loop(0, chunks, chunk, 0, unroll=8)


def ssd_mixer(proj, dt_raw, conv_w, conv_b, dt_bias, a_log, d_skip, norm_w, *, batch, seq, col_off, chunks=16):
    t = batch * seq
    chunks = min(chunks, seq // CHUNK)
    nchunk = seq // (CHUNK * chunks)
    W, N, L, G = GROUP_W, D_STATE, CHUNK * chunks, N_GROUPS
    zb, xb, bb, cb_ = col_off["z"] // W, col_off["xs"] // W, col_off["B"] // N, col_off["C"] // N

    def rows(b, g, c):
        return b * nchunk + c

    def per_group(v):
        r = v.shape[0]
        x_part = v[:, :D_SSD].reshape(r, G, W)
        b_part = v[:, D_SSD:D_SSD + G * N].reshape(r, G, N)
        c_part = v[:, D_SSD + G * N:].reshape(r, G, N)
        return jnp.concatenate([x_part, b_part, c_part], axis=2).transpose(1, 0, 2)

    def x_only(v):
        return jnp.pad(v.astype(F32).reshape(G, 1, W), ((0, 0), (0, 0), (0, 2 * N)))

    pg = jnp.concatenate([per_group(conv_w.astype(F32)), per_group(conv_b.astype(F32).reshape(1, -1)),
                          x_only(jnp.repeat(d_skip, HEAD_DIM)), x_only(norm_w),
                          jnp.zeros((G, 1, W + 2 * N), F32)], axis=1)
    pad = 128 - N_HEADS
    pdt = jnp.pad(jnp.stack([dt_bias.astype(F32), a_log.astype(F32)]), ((0, 6), (0, pad)))

    in_specs = [
        pl.BlockSpec((L, W), lambda b, g, c: (rows(b, g, c), zb + g)),
        pl.BlockSpec((L, W), lambda b, g, c: (rows(b, g, c), xb + g)),
        pl.BlockSpec((L, N), lambda b, g, c: (rows(b, g, c), bb + g)),
        pl.BlockSpec((L, N), lambda b, g, c: (rows(b, g, c), cb_ + g)),
        pl.BlockSpec((L, 128), lambda b, g, c: (rows(b, g, c), 0)),
        pl.BlockSpec((None, 8, W + 2 * N), lambda b, g, c: (g, 0, 0)),
        pl.BlockSpec((8, 128), lambda b, g, c: (0, 0)),
    ]
    return pl.pallas_call(
        functools.partial(_ssd_kernel, chunks=chunks),
        grid=(batch, N_GROUPS, nchunk),
        in_specs=in_specs,
        out_specs=pl.BlockSpec((L, W), lambda b, g, c: (rows(b, g, c), g)),
        out_shape=jax.ShapeDtypeStruct((t, D_SSD), BF16),
        scratch_shapes=[pltpu.VMEM((N, W), F32), pltpu.VMEM((CHUNK + CARRY_ROWS, W + 2 * N), F32)],
        compiler_params=_cparams(("parallel", "parallel", "arbitrary")),
        name="ssd_mixer",
    )(proj, proj, proj, proj, dt_raw, pg, pdt)


def _shortconv_kernel(b_ref, c_ref, h_ref, w_ref, nw_ref, o_ref, ext_ref, y_ref, *, rows, col_chunk):
    s = pl.program_id(1)
    d = D_CONV

    @pl.when(s == 0)
    def _():
        ext_ref[0:CARRY_ROWS, :] = jnp.zeros((CARRY_ROWS, d), F32)

    ss = jnp.zeros((rows, 1), F32)
    for j in range(0, d, col_chunk):
        sl = slice(j, j + col_chunk)
        ext_ref[CARRY_ROWS:CARRY_ROWS + rows, sl] = c_ref[:, sl].astype(F32) * h_ref[:, sl].astype(F32)
        acc = ext_ref[CARRY_ROWS:CARRY_ROWS + rows, sl] * w_ref[2:3, sl]
        for k in range(2):
            acc = acc + ext_ref[CARRY_ROWS - 2 + k:CARRY_ROWS - 2 + k + rows, sl] * w_ref[k:k + 1, sl]
        y = b_ref[:, sl].astype(F32) * acc
        y_ref[:, sl] = y
        ss = ss + jnp.sum(y * y, axis=-1, keepdims=True)
    ext_ref[0:CARRY_ROWS, :] = ext_ref[rows:rows + CARRY_ROWS, :]
    inv = lax.rsqrt(ss * (1.0 / d) + RMS_EPS)
    for j in range(0, d, col_chunk):
        sl = slice(j, j + col_chunk)
        o_ref[:, sl] = (y_ref[:, sl] * inv * nw_ref[:, sl]).astype(o_ref.dtype)


def shortconv_mixer(proj, conv_w, norm_w, *, batch, seq, col_off, rows=512, col_chunk=512):
    t = batch * seq
    rows = min(rows, seq)
    nblk = seq // rows
    d = D_CONV
    bb, cb_, hb = col_off["scb"] // d, col_off["scc"] // d, col_off["sch"] // d
    return pl.pallas_call(
        functools.partial(_shortconv_kernel, rows=rows, col_chunk=col_chunk),
        grid=(batch, nblk),
        in_specs=[
            pl.BlockSpec((rows, d), lambda b, s: (b * nblk + s, bb)),
            pl.BlockSpec((rows, d), lambda b, s: (b * nblk + s, cb_)),
            pl.BlockSpec((rows, d), lambda b, s: (b * nblk + s, hb)),
            pl.BlockSpec((3, d), lambda b, s: (0, 0)),
            pl.BlockSpec((1, d), lambda b, s: (0, 0)),
        ],
        out_specs=pl.BlockSpec((rows, d), lambda b, s: (b * nblk + s, 0)),
        out_shape=jax.ShapeDtypeStruct((t, d), BF16),
        scratch_shapes=[pltpu.VMEM((rows + CARRY_ROWS, d), F32), pltpu.VMEM((rows, d), F32)],
        compiler_params=_cparams(("parallel", "arbitrary")),
        name="shortconv_mixer",
    )(proj, proj, proj, conv_w.astype(F32), norm_w.astype(F32).reshape(1, d))


def _outproj_kernel(a1_ref, a2_ref, b1_ref, b2_ref, r_ref, u_ref, v_ref, o_ref, uv_ref):
    acc = jnp.dot(a1_ref[...], b1_ref[...], preferred_element_type=F32)
    acc = acc + jnp.dot(a2_ref[...], b2_ref[...], preferred_element_type=F32)
    o_ref[...] = acc + r_ref[...]
    uv_ref[0] = u_ref[...].astype(uv_ref.dtype)
    uv_ref[1] = v_ref[...].astype(uv_ref.dtype)


def out_projection(y_ssd, y_sc, w_out, x, u, v, *, tm=1024, tn=256):
    m, kh = y_ssd.shape
    _, n = w_out.shape
    e, d = u.shape
    tm = min(tm, m)
    nj = n // tn
    steps = (m // tm) * nj
    er = e // steps
    assert er * steps == e and er % 16 == 0, "PEER table rows must split evenly over the grid steps"
    tspec = pl.BlockSpec((er, d), lambda i, j: (i * nj + j, 0))
    return pl.pallas_call(
        _outproj_kernel,
        grid=(m // tm, nj),
        in_specs=[
            pl.BlockSpec((tm, kh), lambda i, j: (i, 0)),
            pl.BlockSpec((tm, kh), lambda i, j: (i, 0)),
            pl.BlockSpec((kh, tn), lambda i, j: (0, j)),
            pl.BlockSpec((kh, tn), lambda i, j: (1, j)),
            pl.BlockSpec((tm, tn), lambda i, j: (i, j)),
            tspec, tspec,
        ],
        out_specs=[pl.BlockSpec((tm, tn), lambda i, j: (i, j)),
                   pl.BlockSpec((2, er, d), lambda i, j: (0, i * nj + j, 0))],
        out_shape=[jax.ShapeDtypeStruct((m, n), F32), jax.ShapeDtypeStruct((2, e, d), BF16)],
        compiler_params=_cparams(("parallel", "parallel")),
        name="out_projection",
    )(y_ssd, y_sc, w_out, w_out, x, u, v)


def _top_values(x, n):
    outs = []
    for r in range(n):
        m = jnp.max(x, axis=0, keepdims=True)
        outs.append(m)
        if r < n - 1:
            x = jnp.where(x >= m, -jnp.inf, x)
    return outs


def _query_kernel(x_ref, w_ref, o_ref):
    acc = jnp.dot(x_ref[...], w_ref[...], preferred_element_type=F32)
    for c in range(o_ref.shape[0]):
        o_ref[c] = acc[:, c * N_KEYS:(c + 1) * N_KEYS].astype(o_ref.dtype)


def query_projection(xn, wq, *, tm=1024, tn=1024):
    t, d = xn.shape
    n = wq.shape[1]
    tm = min(tm, t)
    per = tn // N_KEYS
    return pl.pallas_call(
        _query_kernel,
        grid=(t // tm, n // tn),
        in_specs=[pl.BlockSpec((tm, d), lambda i, j: (i, 0)), pl.BlockSpec((d, tn), lambda i, j: (0, j))],
        out_specs=pl.BlockSpec((per, tm, N_KEYS), lambda i, j: (j, i, 0)),
        out_shape=jax.ShapeDtypeStruct((n // N_KEYS, t, N_KEYS), BF16),
        compiler_params=_cparams(("parallel", "parallel")),
        name="query_projection",
    )(xn, wq)


_R_PHI, _R_E0, _R_E1 = 0, 1, 2


def _route_kernel(q_ref, keys_ref, r_ref):
    nt = (((1,), (1,)), ((), ()))

    def head(h, carry):
        s0 = lax.dot_general(keys_ref[2 * h], q_ref[2 * h], nt, preferred_element_type=F32)
        s1 = lax.dot_general(keys_ref[2 * h + 1], q_ref[2 * h + 1], nt, preferred_element_type=F32)
        top0 = _top_values(s0, TOPK + 1)
        top1 = _top_values(s1, TOPK + 1)
        t1 = jnp.concatenate(top1, axis=0)
        pieces = [top0[r] + t1[0:(TOPK + 1) // (r + 1)] for r in range(TOPK + 1)]
        n_cand = sum(p.shape[0] for p in pieces)
        pieces.append(jnp.full((-n_cand % 8, s0.shape[1]), -jnp.inf, F32))
        cand = jnp.concatenate(pieces, axis=0)
        best = _top_values(cand, TOPK + 1)
        m = best[0]
        z = jnp.exp(best[0] - m)
        for r in range(1, TOPK):
            z = z + jnp.exp(best[r] - m)
        inv_z = 0.5 / z
        tau = 0.5 * (best[TOPK - 1] + best[TOPK])
        m0, m1 = top0[0], top1[0]
        r_ref[_R_E0, h] = jnp.exp(s0 - m0)
        r_ref[_R_E1, h] = jnp.exp(s1 - m1) * inv_z
        r_ref[_R_PHI, h] = jnp.exp((tau - m1) - s0) * inv_z
        return carry

    lax.fori_loop(0, PEER_HEADS, head, 0, unroll=8)


def peer_routing(q, keys, *, tb=512):
    _, t, _ = q.shape
    return pl.pallas_call(
        _route_kernel,
        grid=(t // tb,),
        in_specs=[pl.BlockSpec((2 * PEER_HEADS, tb, N_KEYS), lambda i: (0, i, 0)),
                  pl.BlockSpec((2 * PEER_HEADS, N_KEYS, N_KEYS), lambda i: (0, 0, 0))],
        out_specs=pl.BlockSpec((3, PEER_HEADS, N_KEYS, tb), lambda i: (0, 0, 0, i)),
        out_shape=jax.ShapeDtypeStruct((3, PEER_HEADS, N_KEYS, t), F32),
        compiler_params=_cparams(("parallel",)),
        name="peer_routing",
    )(q, keys)


def _gelu_times2(x):
    return x * (1.0 + lax.erf(x * (1.0 / math.sqrt(2.0))))


def _peer_dense_kernel(x_ref, uv_ref, r_ref, o_ref, a_ref, acc_ref, *, te):
    et = pl.program_id(1)
    nblk = te // N_KEYS
    tb = x_ref.shape[0]
    tq = 128
    nt = (((1,), (1,)), ((), ()))

    @pl.when(et == 0)
    def _():
        acc_ref[...] = jnp.zeros_like(acc_ref)

    ht = lax.dot_general(uv_ref[0], x_ref[...], nt, preferred_element_type=F32)
    for j in range(nblk):
        i = et * nblk + j
        rows = slice(j * N_KEYS, (j + 1) * N_KEYS)
        phi_rows = [r_ref[_R_PHI, h, pl.ds(i, 1), :] for h in range(PEER_HEADS)]
        e0_rows = [r_ref[_R_E0, h, pl.ds(i, 1), :] for h in range(PEER_HEADS)]
        for q in range(tb // tq):
            ts = slice(q * tq, (q + 1) * tq)
            gate = None
            for h in range(PEER_HEADS):
                e1 = r_ref[_R_E1, h, :, ts]
                sel = jnp.where(e1 >= phi_rows[h][:, ts], e1, 0.0) * e0_rows[h][:, ts]
                gate = sel if gate is None else gate + sel
            act = _gelu_times2(ht[rows, ts]) * gate
            a_ref[ts, rows] = act.T.astype(a_ref.dtype)
    acc_ref[...] += jnp.dot(a_ref[...], uv_ref[1], preferred_element_type=F32)

    @pl.when(et == pl.num_programs(1) - 1)
    def _():
        o_ref[...] = acc_ref[...].astype(o_ref.dtype)


def peer_dense(xn, uv, r, *, tb=512, te=512):
    t, d = xn.shape
    e = uv.shape[1]
    tb = min(tb, t)
    return pl.pallas_call(
        functools.partial(_peer_dense_kernel, te=te),
        grid=(t // tb, e // te),
        in_specs=[
            pl.BlockSpec((tb, d), lambda i, j: (i, 0)),
            pl.BlockSpec((2, te, d), lambda i, j: (0, j, 0)),
            pl.BlockSpec((3, PEER_HEADS, N_KEYS, tb), lambda i, j: (0, 0, 0, i)),
        ],
        out_specs=pl.BlockSpec((tb, d), lambda i, j: (i, 0)),
        out_shape=jax.ShapeDtypeStruct((t, d), BF16),
        scratch_shapes=[pltpu.VMEM((tb, te), BF16), pltpu.VMEM((tb, d), F32)],
        compiler_params=_cparams(("parallel", "arbitrary")),
        name="peer_dense",
    )(xn, uv, r)


def _final_kernel(h_ref, p_ref, w_ref, o_ref):
    x = h_ref[...] + p_ref[...]
    ms = jnp.mean(x * x, axis=-1, keepdims=True)
    o_ref[...] = x * lax.rsqrt(ms + RMS_EPS) * w_ref[...]


def final_norm(h1, peer, w, *, tm=512):
    t, d = h1.shape
    return pl.pallas_call(
        _final_kernel,
        grid=(t // tm,),
        in_specs=[pl.BlockSpec((tm, d), lambda i: (i, 0)), pl.BlockSpec((tm, d), lambda i: (i, 0)),
                  pl.BlockSpec((1, d), lambda i: (0, 0))],
        out_specs=pl.BlockSpec((tm, d), lambda i: (i, 0)),
        out_shape=jax.ShapeDtypeStruct((t, d), F32),
        compiler_params=_cparams(("parallel",)),
        name="final_norm",
    )(h1, peer, w.reshape(1, d))


def kernel(x, w_in, ssd_conv_w, ssd_conv_b, ssd_dt_bias, ssd_a_log, ssd_d, ssd_norm_w, sc_conv_w, sc_norm_w, w_out,
           norm_mix_w, norm_ffn_w, peer_w_query, peer_sub_keys, peer_u, peer_v, norm_final_w):
    batch, seq, d = x.shape
    t = batch * seq
    h = x.reshape(t, d)
    assert w_in.shape[0] == 1, "single-layer block"
    o_c = 2 * D_SSD + N_GROUPS * D_STATE
    o2 = o_c + N_GROUPS * D_STATE
    o3 = o2 + N_HEADS
    ssd_off = {"z": 0, "xs": D_SSD, "B": 2 * D_SSD, "C": o_c}
    sc_off = {"scb": 0, "scc": D_CONV, "sch": 2 * D_CONV}
    wt = w_in[0].T.astype(BF16)
    w_sc = wt[o3:]
    keys = peer_sub_keys[0].reshape(2 * PEER_HEADS, N_KEYS, N_KEYS).astype(BF16)

    xn = rmsnorm_rows(h, norm_mix_w[0])
    proj_ssd = projection(xn, wt, n=o2, tm=1024, tn=1024, out_dtype=BF16, name="in_projection_ssd")
    proj_sc, w_out_b, wq_b = projection(xn, w_sc, n=w_sc.shape[0], tm=1024, tn=768, out_dtype=BF16,
                                        name="in_projection_sc", casts=(w_out[0], peer_w_query[0]))
    dt_raw = projection(xn, wt, n=128, n_start=o2, tm=1024, tn=128, out_dtype=F32, name="dt_projection")
    y_ssd = ssd_mixer(proj_ssd, dt_raw, ssd_conv_w[0], ssd_conv_b[0], ssd_dt_bias[0], ssd_a_log[0], ssd_d[0],
                      ssd_norm_w[0], batch=batch, seq=seq, col_off=ssd_off)
    y_sc = shortconv_mixer(proj_sc, sc_conv_w[0], sc_norm_w[0], batch=batch, seq=seq, col_off=sc_off)
    h, uv_b = out_projection(y_ssd, y_sc, w_out_b, h, peer_u[0], peer_v[0])

    xn2 = rmsnorm_rows(h, norm_ffn_w[0])
    q = query_projection(xn2, wq_b)
    r = peer_routing(q, keys)
    peer = peer_dense(xn2, uv_b, r)
    out = final_norm(h, peer, norm_final_w)
    return out.reshape(batch, seq, d)
```

```python
import functools
import math

import jax
import jax.numpy as jnp
from jax import lax
from jax.experimental import pallas as pl
from jax.experimental.pallas import tpu as pltpu

F32 = jnp.float32
BF16 = jnp.bfloat16

D_SSD = 4096
D_CONV = 4096
HEAD_DIM = 64
N_HEADS = 64
N_GROUPS = 8
HEADS_PER_GROUP = 8
D_STATE = 128
CHUNK = 128
GROUP_W = HEADS_PER_GROUP * HEAD_DIM
PEER_HEADS = 8
N_KEYS = 128
TOPK = 16
RMS_EPS = 1e-6

VMEM_LIMIT_BYTES = 58 * 1024 * 1024
CARRY_ROWS = 8


def _cparams(sem):
    return pltpu.CompilerParams(dimension_semantics=sem, vmem_limit_bytes=VMEM_LIMIT_BYTES)


_NT = (((1,), (1,)), ((), ()))


def _rms_bf16(x, w):
    ms = jnp.mean(x * x, axis=-1, keepdims=True)
    return (x * lax.rsqrt(ms + RMS_EPS) * w).astype(BF16)


def _proj_kernel(a_ref, b_ref, *refs):
    n_cast = len(refs) // 2
    src_refs, o_ref, dst_refs = refs[:n_cast], refs[n_cast], refs[n_cast + 1:]
    o_ref[...] = lax.dot_general(a_ref[...], b_ref[...], _NT, preferred_element_type=F32).astype(o_ref.dtype)
    for src, dst in zip(src_refs, dst_refs):
        dst[...] = src[...].astype(dst.dtype)


def projection(a, w_t, *, n, n_start=0, tm, tn, out_dtype, name, casts=()):
    m, kdim = a.shape
    tm = min(tm, m)
    j0 = n_start // tn
    nj = n // tn
    steps = (m // tm) * nj
    cast_specs = []
    for c in casts:
        rows = c.shape[0] // steps
        assert rows * steps == c.shape[0] and rows % 16 == 0, "side-cast rows must split evenly over the grid steps"
        cast_specs.append(pl.BlockSpec((rows, c.shape[1]), lambda i, j: (i * nj + j, 0)))
    out = pl.pallas_call(
        _proj_kernel,
        grid=(m // tm, nj),
        in_specs=[pl.BlockSpec((tm, kdim), lambda i, j: (i, 0)),
                  pl.BlockSpec((pl.Element(tn), pl.Element(kdim)),
                               lambda i, j: (pl.multiple_of(n_start + j * tn, 16), 0)),
                  *cast_specs],
        out_specs=[pl.BlockSpec((tm, tn), lambda i, j: (i, j)), *cast_specs],
        out_shape=[jax.ShapeDtypeStruct((m, n), out_dtype)] + [jax.ShapeDtypeStruct(c.shape, BF16) for c in casts],
        compiler_params=_cparams(("parallel", "parallel")),
        name=name,
    )(a, w_t, *casts)
    return out if casts else out[0]


def _rmsnorm_kernel(x_ref, w_ref, o_ref):
    o_ref[...] = _rms_bf16(x_ref[...], w_ref[...])


def rmsnorm_rows(x, w, *, tm=512):
    t, d = x.shape
    return pl.pallas_call(
        _rmsnorm_kernel,
        grid=(t // tm,),
        in_specs=[pl.BlockSpec((tm, d), lambda i: (i, 0)), pl.BlockSpec((1, d), lambda i: (0, 0))],
        out_specs=pl.BlockSpec((tm, d), lambda i: (i, 0)),
        out_shape=jax.ShapeDtypeStruct((t, d), BF16),
        compiler_params=_cparams(("parallel",)),
        name="rmsnorm_rows",
    )(x, w.reshape(1, d))


def _silu(x):
    hx = 0.5 * x
    return hx + hx * jnp.tanh(hx)


def _softplus(x):
    return jnp.maximum(x, 0.0) + jnp.log(1.0 + jnp.exp(-jnp.abs(x)))


_P_CONV, _P_BIAS, _P_DSKIP, _P_NORM = 0, 4, 5, 6


def _ssd_kernel(z_ref, xs_ref, b_ref, c_ref, dt_ref, pg_ref, pdt_ref, o_ref, state_ref, ext_ref, *, chunks):
    g = pl.program_id(1)
    c = pl.program_id(2)
    L = CHUNK
    W = GROUP_W
    N = D_STATE
    XBC = W + 2 * N

    @pl.when(c == 0)
    def _():
        state_ref[...] = jnp.zeros_like(state_ref)
        ext_ref[0:CARRY_ROWS, :] = jnp.zeros((CARRY_ROWS, XBC), F32)

    row = lax.broadcasted_iota(jnp.int32, (L, L), 0)
    col = lax.broadcasted_iota(jnp.int32, (L, L), 1)
    causal = row >= col
    tril = causal.astype(BF16)
    er = lax.broadcasted_iota(jnp.int32, (128, W), 0)
    ec = lax.broadcasted_iota(jnp.int32, (128, W), 1)
    expand = (er == ec // HEAD_DIM).astype(BF16)

    def chunk(ci, carry):
        rs = pl.ds(pl.multiple_of(ci * L, L), L)
        ext_ref[CARRY_ROWS:CARRY_ROWS + L, 0:W] = xs_ref[rs, :].astype(F32)
        ext_ref[CARRY_ROWS:CARRY_ROWS + L, W:W + N] = b_ref[rs, :].astype(F32)
        ext_ref[CARRY_ROWS:CARRY_ROWS + L, W + N:XBC] = c_ref[rs, :].astype(F32)
        acc = ext_ref[CARRY_ROWS:CARRY_ROWS + L, :] * pg_ref[_P_CONV + 3:_P_CONV + 4, :]
        for k in range(3):
            acc = acc + ext_ref[CARRY_ROWS - 3 + k:CARRY_ROWS - 3 + k + L, :] * pg_ref[_P_CONV + k:_P_CONV + k + 1, :]
        xbc = _silu(acc + pg_ref[_P_BIAS:_P_BIAS + 1, :])
        ext_ref[0:CARRY_ROWS, :] = ext_ref[L:L + CARRY_ROWS, :]
        xs = xbc[:, 0:W]
        b_b = xbc[:, W:W + N].astype(BF16)
        c_b = xbc[:, W + N:XBC].astype(BF16)

        dt_all = _softplus(dt_ref[rs, :] + pdt_ref[0:1, :])
        da_all = dt_all * (-jnp.exp(pdt_ref[1:2, :]))
        shift = (128 - g * HEADS_PER_GROUP) % 128
        dt_g = pltpu.roll(dt_all, shift, 1)
        da_g = pltpu.roll(da_all, shift, 1)

        hi = da_g.astype(BF16)
        r1 = da_g - hi.astype(F32)
        mid = r1.astype(BF16)
        lo = (r1 - mid.astype(F32)).astype(BF16)
        acum = (jnp.dot(tril, hi, preferred_element_type=F32)
                + jnp.dot(tril, mid, preferred_element_type=F32)
                + jnp.dot(tril, lo, preferred_element_type=F32))
        acum_t = acum.T
        a_last = acum[L - 1:L, :]

        def expand_heads(v):
            vh = v.astype(BF16)
            vl = (v - vh.astype(F32)).astype(BF16)
            return jnp.dot(vh, expand, preferred_element_type=F32) + jnp.dot(vl, expand, preferred_element_type=F32)

        dt_x = expand_heads(dt_g)
        ea_x = expand_heads(jnp.exp(acum))
        to_end_x = expand_heads(jnp.exp(a_last - acum))

        xdt = xs * dt_x
        xdt_b = xdt.astype(BF16)

        cb = lax.dot_general(c_b, b_b, (((1,), (1,)), ((), ())), preferred_element_type=F32)
        state = state_ref[...]
        y_off = jnp.dot(c_b, state.astype(BF16), preferred_element_type=F32) * ea_x

        y_parts = []
        for r in range(HEADS_PER_GROUP):
            seg = acum[:, r:r + 1] - acum_t[r:r + 1, :]
            decay = jnp.exp(jnp.where(causal, seg, -jnp.inf))
            m = (cb * decay).astype(BF16)
            y_parts.append(jnp.dot(m, xdt_b[:, r * HEAD_DIM:(r + 1) * HEAD_DIM], preferred_element_type=F32))
        y = jnp.concatenate(y_parts, axis=1) + y_off + pg_ref[_P_DSKIP:_P_DSKIP + 1, 0:W] * xs

        upd = lax.dot_general(b_b, (xdt * to_end_x).astype(BF16), (((0,), (0,)), ((), ())),
                              preferred_element_type=F32)
        state_ref[...] = state * ea_x[L - 1:L, :] + upd

        y = y * _silu(z_ref[rs, :].astype(F32))
        ms = jnp.mean(y * y, axis=-1, keepdims=True)
        o_ref[rs, :] = (y * lax.rsqrt(ms + RMS_EPS) * pg_ref[_P_NORM:_P_NORM + 1, 0:W]).astype(o_ref.dtype)
        return carry

    lax.fori_loop(0, chunks, chunk, 0, unroll=8)


def ssd_mixer(proj, dt_raw, conv_w, conv_b, dt_bias, a_log, d_skip, norm_w, *, batch, seq, col_off, chunks=16):
    t = batch * seq
    chunks = min(chunks, seq // CHUNK)
    nchunk = seq // (CHUNK * chunks)
    W, N, L, G = GROUP_W, D_STATE, CHUNK * chunks, N_GROUPS
    zb, xb, bb, cb_ = col_off["z"] // W, col_off["xs"] // W, col_off["B"] // N, col_off["C"] // N

    def rows(b, g, c):
        return b * nchunk + c

    def per_group(v):
        r = v.shape[0]
        x_part = v[:, :D_SSD].reshape(r, G, W)
        b_part = v[:, D_SSD:D_SSD + G * N].reshape(r, G, N)
        c_part = v[:, D_SSD + G * N:].reshape(r, G, N)
        return jnp.concatenate([x_part, b_part, c_part], axis=2).transpose(1, 0, 2)

    def x_only(v):
        return jnp.pad(v.astype(F32).reshape(G, 1, W), ((0, 0), (0, 0), (0, 2 * N)))

    pg = jnp.concatenate([per_group(conv_w.astype(F32)), per_group(conv_b.astype(F32).reshape(1, -1)),
                          x_only(jnp.repeat(d_skip, HEAD_DIM)), x_only(norm_w),
                          jnp.zeros((G, 1, W + 2 * N), F32)], axis=1)
    pad = 128 - N_HEADS
    pdt = jnp.pad(jnp.stack([dt_bias.astype(F32), a_log.astype(F32)]), ((0, 6), (0, pad)))

    in_specs = [
        pl.BlockSpec((L, W), lambda b, g, c: (rows(b, g, c), zb + g)),
        pl.BlockSpec((L, W), lambda b, g, c: (rows(b, g, c), xb + g)),
        pl.BlockSpec((L, N), lambda b, g, c: (rows(b, g, c), bb + g)),
        pl.BlockSpec((L, N), lambda b, g, c: (rows(b, g, c), cb_ + g)),
        pl.BlockSpec((L, 128), lambda b, g, c: (rows(b, g, c), 0)),
        pl.BlockSpec((None, 8, W + 2 * N), lambda b, g, c: (g, 0, 0)),
        pl.BlockSpec((8, 128), lambda b, g, c: (0, 0)),
    ]
    return pl.pallas_call(
        functools.partial(_ssd_kernel, chunks=chunks),
        grid=(batch, N_GROUPS, nchunk),
        in_specs=in_specs,
        out_specs=pl.BlockSpec((L, W), lambda b, g, c: (rows(b, g, c), g)),
        out_shape=jax.ShapeDtypeStruct((t, D_SSD), BF16),
        scratch_shapes=[pltpu.VMEM((N, W), F32), pltpu.VMEM((CHUNK + CARRY_ROWS, W + 2 * N), F32)],
        compiler_params=_cparams(("parallel", "parallel", "arbitrary")),
        name="ssd_mixer",
    )(proj, proj, proj, proj, dt_raw, pg, pdt)


def _shortconv_kernel(b_ref, c_ref, h_ref, w_ref, nw_ref, o_ref, ext_ref, y_ref, *, rows, col_chunk):
    s = pl.program_id(1)
    d = D_CONV

    @pl.when(s == 0)
    def _():
        ext_ref[0:CARRY_ROWS, :] = jnp.zeros((CARRY_ROWS, d), F32)

    ss = jnp.zeros((rows, 1), F32)
    for j in range(0, d, col_chunk):
        sl = slice(j, j + col_chunk)
        ext_ref[CARRY_ROWS:CARRY_ROWS + rows, sl] = c_ref[:, sl].astype(F32) * h_ref[:, sl].astype(F32)
        acc = ext_ref[CARRY_ROWS:CARRY_ROWS + rows, sl] * w_ref[2:3, sl]
        for k in range(2):
            acc = acc + ext_ref[CARRY_ROWS - 2 + k:CARRY_ROWS - 2 + k + rows, sl] * w_ref[k:k + 1, sl]
        y = b_ref[:, sl].astype(F32) * acc
        y_ref[:, sl] = y
        ss = ss + jnp.sum(y * y, axis=-1, keepdims=True)
    ext_ref[0:CARRY_ROWS, :] = ext_ref[rows:rows + CARRY_ROWS, :]
    inv = lax.rsqrt(ss * (1.0 / d) + RMS_EPS)
    for j in range(0, d, col_chunk):
        sl = slice(j, j + col_chunk)
        o_ref[:, sl] = (y_ref[:, sl] * inv * nw_ref[:, sl]).astype(o_ref.dtype)


def shortconv_mixer(proj, conv_w, norm_w, *, batch, seq, col_off, rows=512, col_chunk=512):
    t = batch * seq
    rows = min(rows, seq)
    nblk = seq // rows
    d = D_CONV
    bb, cb_, hb = col_off["scb"] // d, col_off["scc"] // d, col_off["sch"] // d
    return pl.pallas_call(
        functools.partial(_shortconv_kernel, rows=rows, col_chunk=col_chunk),
        grid=(batch, nblk),
        in_specs=[
            pl.BlockSpec((rows, d), lambda b, s: (b * nblk + s, bb)),
            pl.BlockSpec((rows, d), lambda b, s: (b * nblk + s, cb_)),
            pl.BlockSpec((rows, d), lambda b, s: (b * nblk + s, hb)),
            pl.BlockSpec((3, d), lambda b, s: (0, 0)),
            pl.BlockSpec((1, d), lambda b, s: (0, 0)),
        ],
        out_specs=pl.BlockSpec((rows, d), lambda b, s: (b * nblk + s, 0)),
        out_shape=jax.ShapeDtypeStruct((t, d), BF16),
        scratch_shapes=[pltpu.VMEM((rows + CARRY_ROWS, d), F32), pltpu.VMEM((rows, d), F32)],
        compiler_params=_cparams(("parallel", "arbitrary")),
        name="shortconv_mixer",
    )(proj, proj, proj, conv_w.astype(F32), norm_w.astype(F32).reshape(1, d))


def _outproj_kernel(a1_ref, a2_ref, b1_ref, b2_ref, r_ref, o_ref):
    acc = jnp.dot(a1_ref[...], b1_ref[...], preferred_element_type=F32)
    acc = acc + jnp.dot(a2_ref[...], b2_ref[...], preferred_element_type=F32)
    o_ref[...] = acc + r_ref[...]


def out_projection(y_ssd, y_sc, w_out, x, *, tm=1024, tn=512):
    m, kh = y_ssd.shape
    _, n = w_out.shape
    tm = min(tm, m)
    return pl.pallas_call(
        _outproj_kernel,
        grid=(m // tm, n // tn),
        in_specs=[
            pl.BlockSpec((tm, kh), lambda i, j: (i, 0)),
            pl.BlockSpec((tm, kh), lambda i, j: (i, 0)),
            pl.BlockSpec((kh, tn), lambda i, j: (0, j)),
            pl.BlockSpec((kh, tn), lambda i, j: (1, j)),
            pl.BlockSpec((tm, tn), lambda i, j: (i, j)),
        ],
        out_specs=pl.BlockSpec((tm, tn), lambda i, j: (i, j)),
        out_shape=jax.ShapeDtypeStruct((m, n), F32),
        compiler_params=_cparams(("parallel", "parallel")),
        name="out_projection",
    )(y_ssd, y_sc, w_out, w_out, x)


def _top_values(x, n):
    outs = []
    for r in range(n):
        m = jnp.max(x, axis=0, keepdims=True)
        outs.append(m)
        if r < n - 1:
            x = jnp.where(x >= m, -jnp.inf, x)
    return outs


def _query_kernel(x_ref, w_ref, o_ref):
    acc = jnp.dot(x_ref[...], w_ref[...], preferred_element_type=F32)
    for c in range(o_ref.shape[0]):
        o_ref[c] = acc[:, c * N_KEYS:(c + 1) * N_KEYS].astype(o_ref.dtype)


def query_projection(xn, wq, *, tm=1024, tn=1024):
    t, d = xn.shape
    n = wq.shape[1]
    tm = min(tm, t)
    per = tn // N_KEYS
    return pl.pallas_call(
        _query_kernel,
        grid=(t // tm, n // tn),
        in_specs=[pl.BlockSpec((tm, d), lambda i, j: (i, 0)), pl.BlockSpec((d, tn), lambda i, j: (0, j))],
        out_specs=pl.BlockSpec((per, tm, N_KEYS), lambda i, j: (j, i, 0)),
        out_shape=jax.ShapeDtypeStruct((n // N_KEYS, t, N_KEYS), BF16),
        compiler_params=_cparams(("parallel", "parallel")),
        name="query_projection",
    )(xn, wq)


_R_PHI, _R_E0, _R_E1 = 0, 1, 2


def _route_kernel(q_ref, keys_ref, r_ref):
    nt = (((1,), (1,)), ((), ()))

    def head(h, carry):
        s0 = lax.dot_general(keys_ref[2 * h], q_ref[2 * h], nt, preferred_element_type=F32)
        s1 = lax.dot_general(keys_ref[2 * h + 1], q_ref[2 * h + 1], nt, preferred_element_type=F32)
        top0 = _top_values(s0, TOPK + 1)
        top1 = _top_values(s1, TOPK + 1)
        t1 = jnp.concatenate(top1, axis=0)
        pieces = [top0[r] + t1[0:(TOPK + 1) // (r + 1)] for r in range(TOPK + 1)]
        n_cand = sum(p.shape[0] for p in pieces)
        pieces.append(jnp.full((-n_cand % 8, s0.shape[1]), -jnp.inf, F32))
        cand = jnp.concatenate(pieces, axis=0)
        best = _top_values(cand, TOPK + 1)
        m = best[0]
        z = jnp.exp(best[0] - m)
        for r in range(1, TOPK):
            z = z + jnp.exp(best[r] - m)
        inv_z = 0.5 / z
        tau = 0.5 * (best[TOPK - 1] + best[TOPK])
        m0, m1 = top0[0], top1[0]
        r_ref[_R_E0, h] = jnp.exp(s0 - m0)
        r_ref[_R_E1, h] = jnp.exp(s1 - m1) * inv_z
        r_ref[_R_PHI, h] = jnp.exp((tau - m1) - s0) * inv_z
        return carry

    lax.fori_loop(0, PEER_HEADS, head, 0, unroll=8)


def peer_routing(q, keys, *, tb=512):
    _, t, _ = q.shape
    return pl.pallas_call(
        _route_kernel,
        grid=(t // tb,),
        in_specs=[pl.BlockSpec((2 * PEER_HEADS, tb, N_KEYS), lambda i: (0, i, 0)),
                  pl.BlockSpec((2 * PEER_HEADS, N_KEYS, N_KEYS), lambda i: (0, 0, 0))],
        out_specs=pl.BlockSpec((3, PEER_HEADS, N_KEYS, tb), lambda i: (0, 0, 0, i)),
        out_shape=jax.ShapeDtypeStruct((3, PEER_HEADS, N_KEYS, t), F32),
        compiler_params=_cparams(("parallel",)),
        name="peer_routing",
    )(q, keys)


def _gelu_times2(x):
    return x * (1.0 + lax.erf(x * (1.0 / math.sqrt(2.0))))


def _peer_dense_kernel(x_ref, u_ref, v_ref, r_ref, o_ref, a_ref, acc_ref, *, te):
    et = pl.program_id(1)
    nblk = te // N_KEYS
    tb = x_ref.shape[0]
    tq = 128
    nt = (((1,), (1,)), ((), ()))

    @pl.when(et == 0)
    def _():
        acc_ref[...] = jnp.zeros_like(acc_ref)

    ht = lax.dot_general(u_ref[...], x_ref[...], nt, preferred_element_type=F32)
    for j in range(nblk):
        i = et * nblk + j
        rows = slice(j * N_KEYS, (j + 1) * N_KEYS)
        phi_rows = [r_ref[_R_PHI, h, pl.ds(i, 1), :] for h in range(PEER_HEADS)]
        e0_rows = [r_ref[_R_E0, h, pl.ds(i, 1), :] for h in range(PEER_HEADS)]
        for q in range(tb // tq):
            ts = slice(q * tq, (q + 1) * tq)
            gate = None
            for h in range(PEER_HEADS):
                e1 = r_ref[_R_E1, h, :, ts]
                sel = jnp.where(e1 >= phi_rows[h][:, ts], e1, 0.0) * e0_rows[h][:, ts]
                gate = sel if gate is None else gate + sel
            act = _gelu_times2(ht[rows, ts]) * gate
            a_ref[ts, rows] = act.T.astype(a_ref.dtype)
    acc_ref[...] += jnp.dot(a_ref[...], v_ref[...], preferred_element_type=F32)

    @pl.when(et == pl.num_programs(1) - 1)
    def _():
        o_ref[...] = acc_ref[...].astype(o_ref.dtype)


def peer_dense(xn, u, v, r, *, tb=512, te=512):
    t, d = xn.shape
    e = u.shape[0]
    tb = min(tb, t)
    return pl.pallas_call(
        functools.partial(_peer_dense_kernel, te=te),
        grid=(t // tb, e // te),
        in_specs=[
            pl.BlockSpec((tb, d), lambda i, j: (i, 0)),
            pl.BlockSpec((te, d), lambda i, j: (j, 0)),
            pl.BlockSpec((te, d), lambda i, j: (j, 0)),
            pl.BlockSpec((3, PEER_HEADS, N_KEYS, tb), lambda i, j: (0, 0, 0, i)),
        ],
        out_specs=pl.BlockSpec((tb, d), lambda i, j: (i, 0)),
        out_shape=jax.ShapeDtypeStruct((t, d), BF16),
        scratch_shapes=[pltpu.VMEM((tb, te), BF16), pltpu.VMEM((tb, d), F32)],
        compiler_params=_cparams(("parallel", "arbitrary")),
        name="peer_dense",
    )(xn, u, v, r)


def _final_kernel(h_ref, p_ref, w_ref, o_ref):
    x = h_ref[...] + p_ref[...]
    ms = jnp.mean(x * x, axis=-1, keepdims=True)
    o_ref[...] = x * lax.rsqrt(ms + RMS_EPS) * w_ref[...]


def final_norm(h1, peer, w, *, tm=512):
    t, d = h1.shape
    return pl.pallas_call(
        _final_kernel,
        grid=(t // tm,),
        in_specs=[pl.BlockSpec((tm, d), lambda i: (i, 0)), pl.BlockSpec((tm, d), lambda i: (i, 0)),
                  pl.BlockSpec((1, d), lambda i: (0, 0))],
        out_specs=pl.BlockSpec((tm, d), lambda i: (i, 0)),
        out_shape=jax.ShapeDtypeStruct((t, d), F32),
        compiler_params=_cparams(("parallel",)),
        name="final_norm",
    )(h1, peer, w.reshape(1, d))


def kernel(x, w_in, ssd_conv_w, ssd_conv_b, ssd_dt_bias, ssd_a_log, ssd_d, ssd_norm_w, sc_conv_w, sc_norm_w, w_out,
           norm_mix_w, norm_ffn_w, peer_w_query, peer_sub_keys, peer_u, peer_v, norm_final_w):
    batch, seq, d = x.shape
    t = batch * seq
    h = x.reshape(t, d)
    assert w_in.shape[0] == 1, "single-layer block"
    o_c = 2 * D_SSD + N_GROUPS * D_STATE
    o2 = o_c + N_GROUPS * D_STATE
    o3 = o2 + N_HEADS
    ssd_off = {"z": 0, "xs": D_SSD, "B": 2 * D_SSD, "C": o_c}
    sc_off = {"scb": 0, "scc": D_CONV, "sch": 2 * D_CONV}
    wt = w_in[0].T.astype(BF16)
    keys = peer_sub_keys[0].reshape(2 * PEER_HEADS, N_KEYS, N_KEYS).astype(BF16)

    xn = rmsnorm_rows(h, norm_mix_w[0])
    proj_ssd = projection(xn, wt, n=o2, tm=1024, tn=1024, out_dtype=BF16, name="in_projection_ssd")
    proj_sc, w_out_b, wq_b, u_b, v_b = projection(
        xn, wt, n=3 * D_CONV, n_start=o3, tm=1024, tn=768, out_dtype=BF16, name="in_projection_sc",
        casts=(w_out[0], peer_w_query[0], peer_u[0], peer_v[0]))
    dt_raw = projection(xn, wt, n=128, n_start=o2, tm=1024, tn=128, out_dtype=F32, name="dt_projection")
    y_ssd = ssd_mixer(proj_ssd, dt_raw, ssd_conv_w[0], ssd_conv_b[0], ssd_dt_bias[0], ssd_a_log[0], ssd_d[0],
                      ssd_norm_w[0], batch=batch, seq=seq, col_off=ssd_off)
    y_sc = shortconv_mixer(proj_sc, sc_conv_w[0], sc_norm_w[0], batch=batch, seq=seq, col_off=sc_off)
    h = out_projection(y_ssd, y_sc, w_out_b, h)

    xn2 = rmsnorm_rows(h, norm_ffn_w[0])
    q = query_projection(xn2, wq_b)
    r = peer_routing(q, keys)
    peer = peer_dense(xn2, u_b, v_b, r)
    out = final_norm(h, peer, norm_final_w)
    return out.reshape(batch, seq, d)
```

```python
import functools
import math

import jax
import jax.numpy as jnp
from jax import lax
from jax.experimental import pallas as pl
from jax.experimental.pallas import tpu as pltpu

F32 = jnp.float32
BF16 = jnp.bfloat16

D_SSD = 4096
D_CONV = 4096
HEAD_DIM = 64
N_HEADS = 64
N_GROUPS = 8
HEADS_PER_GROUP = 8
D_STATE = 128
CHUNK = 128
GROUP_W = HEADS_PER_GROUP * HEAD_DIM
PEER_HEADS = 8
N_KEYS = 128
TOPK = 16
RMS_EPS = 1e-6

VMEM_LIMIT_BYTES = 58 * 1024 * 1024
CARRY_ROWS = 8


def _cparams(sem):
    return pltpu.CompilerParams(dimension_semantics=sem, vmem_limit_bytes=VMEM_LIMIT_BYTES)


_NT = (((1,), (1,)), ((), ()))


def _rms_bf16(x, w):
    ms = jnp.mean(x * x, axis=-1, keepdims=True)
    return (x * lax.rsqrt(ms + RMS_EPS) * w).astype(BF16)


def _proj_kernel(a_ref, b_ref, *refs):
    n_cast = len(refs) // 2
    src_refs, o_ref, dst_refs = refs[:n_cast], refs[n_cast], refs[n_cast + 1:]
    o_ref[...] = lax.dot_general(a_ref[...], b_ref[...], _NT, preferred_element_type=F32).astype(o_ref.dtype)
    for src, dst in zip(src_refs, dst_refs):
        dst[...] = src[...].astype(dst.dtype)


def projection(a, w_t, *, n, n_start=0, tm, tn, out_dtype, name, casts=()):
    m, kdim = a.shape
    tm = min(tm, m)
    j0 = n_start // tn
    nj = n // tn
    steps = (m // tm) * nj
    cast_specs = []
    for c in casts:
        rows = c.shape[0] // steps
        assert rows * steps == c.shape[0] and rows % 16 == 0, "side-cast rows must split evenly over the grid steps"
        cast_specs.append(pl.BlockSpec((rows, c.shape[1]), lambda i, j: (i * nj + j, 0)))
    out = pl.pallas_call(
        _proj_kernel,
        grid=(m // tm, nj),
        in_specs=[pl.BlockSpec((tm, kdim), lambda i, j: (i, 0)),
                  pl.BlockSpec((pl.Element(tn), pl.Element(kdim)),
                               lambda i, j: (pl.multiple_of(n_start + j * tn, 16), 0)),
                  *cast_specs],
        out_specs=[pl.BlockSpec((tm, tn), lambda i, j: (i, j)), *cast_specs],
        out_shape=[jax.ShapeDtypeStruct((m, n), out_dtype)] + [jax.ShapeDtypeStruct(c.shape, BF16) for c in casts],
        compiler_params=_cparams(("parallel", "parallel")),
        name=name,
    )(a, w_t, *casts)
    return out if casts else out[0]


def _rmsnorm_kernel(x_ref, w_ref, o_ref):
    o_ref[...] = _rms_bf16(x_ref[...], w_ref[...])


def rmsnorm_rows(x, w, *, tm=512):
    t, d = x.shape
    return pl.pallas_call(
        _rmsnorm_kernel,
        grid=(t // tm,),
        in_specs=[pl.BlockSpec((tm, d), lambda i: (i, 0)), pl.BlockSpec((1, d), lambda i: (0, 0))],
        out_specs=pl.BlockSpec((tm, d), lambda i: (i, 0)),
        out_shape=jax.ShapeDtypeStruct((t, d), BF16),
        compiler_params=_cparams(("parallel",)),
        name="rmsnorm_rows",
    )(x, w.reshape(1, d))


def _silu(x):
    hx = 0.5 * x
    return hx + hx * jnp.tanh(hx)


def _softplus(x):
    return jnp.maximum(x, 0.0) + jnp.log(1.0 + jnp.exp(-jnp.abs(x)))


_P_CONV, _P_BIAS, _P_DSKIP, _P_NORM = 0, 4, 5, 6


def _ssd_kernel(z_ref, xs_ref, b_ref, c_ref, dt_ref, pg_ref, pdt_ref, o_ref, state_ref, ext_ref, *, chunks):
    g = pl.program_id(1)
    c = pl.program_id(2)
    L = CHUNK
    W = GROUP_W
    N = D_STATE
    XBC = W + 2 * N

    @pl.when(c == 0)
    def _():
        state_ref[...] = jnp.zeros_like(state_ref)
        ext_ref[0:CARRY_ROWS, :] = jnp.zeros((CARRY_ROWS, XBC), F32)

    row = lax.broadcasted_iota(jnp.int32, (L, L), 0)
    col = lax.broadcasted_iota(jnp.int32, (L, L), 1)
    causal = row >= col
    tril = causal.astype(BF16)
    er = lax.broadcasted_iota(jnp.int32, (128, W), 0)
    ec = lax.broadcasted_iota(jnp.int32, (128, W), 1)
    expand = (er == ec // HEAD_DIM).astype(BF16)

    def chunk(ci, carry):
        rs = pl.ds(pl.multiple_of(ci * L, L), L)
        ext_ref[CARRY_ROWS:CARRY_ROWS + L, 0:W] = xs_ref[rs, :].astype(F32)
        ext_ref[CARRY_ROWS:CARRY_ROWS + L, W:W + N] = b_ref[rs, :].astype(F32)
        ext_ref[CARRY_ROWS:CARRY_ROWS + L, W + N:XBC] = c_ref[rs, :].astype(F32)
        acc = ext_ref[CARRY_ROWS:CARRY_ROWS + L, :] * pg_ref[_P_CONV + 3:_P_CONV + 4, :]
        for k in range(3):
            acc = acc + ext_ref[CARRY_ROWS - 3 + k:CARRY_ROWS - 3 + k + L, :] * pg_ref[_P_CONV + k:_P_CONV + k + 1, :]
        xbc = _silu(acc + pg_ref[_P_BIAS:_P_BIAS + 1, :])
        ext_ref[0:CARRY_ROWS, :] = ext_ref[L:L + CARRY_ROWS, :]
        xs = xbc[:, 0:W]
        b_b = xbc[:, W:W + N].astype(BF16)
        c_b = xbc[:, W + N:XBC].astype(BF16)

        dt_all = _softplus(dt_ref[rs, :] + pdt_ref[0:1, :])
        da_all = dt_all * (-jnp.exp(pdt_ref[1:2, :]))
        shift = (128 - g * HEADS_PER_GROUP) % 128
        dt_g = pltpu.roll(dt_all, shift, 1)
        da_g = pltpu.roll(da_all, shift, 1)

        hi = da_g.astype(BF16)
        r1 = da_g - hi.astype(F32)
        mid = r1.astype(BF16)
        lo = (r1 - mid.astype(F32)).astype(BF16)
        acum = (jnp.dot(tril, hi, preferred_element_type=F32)
                + jnp.dot(tril, mid, preferred_element_type=F32)
                + jnp.dot(tril, lo, preferred_element_type=F32))
        acum_t = acum.T
        a_last = acum[L - 1:L, :]

        def expand_heads(v):
            vh = v.astype(BF16)
            vl = (v - vh.astype(F32)).astype(BF16)
            return jnp.dot(vh, expand, preferred_element_type=F32) + jnp.dot(vl, expand, preferred_element_type=F32)

        dt_x = expand_heads(dt_g)
        ea_x = expand_heads(jnp.exp(acum))
        to_end_x = expand_heads(jnp.exp(a_last - acum))

        xdt = xs * dt_x
        xdt_b = xdt.astype(BF16)

        cb = lax.dot_general(c_b, b_b, (((1,), (1,)), ((), ())), preferred_element_type=F32)
        state = state_ref[...]
        y_off = jnp.dot(c_b, state.astype(BF16), preferred_element_type=F32) * ea_x

        y_parts = []
        for r in range(HEADS_PER_GROUP):
            seg = acum[:, r:r + 1] - acum_t[r:r + 1, :]
            decay = jnp.exp(jnp.where(causal, seg, -jnp.inf))
            m = (cb * decay).astype(BF16)
            y_parts.append(jnp.dot(m, xdt_b[:, r * HEAD_DIM:(r + 1) * HEAD_DIM], preferred_element_type=F32))
        y = jnp.concatenate(y_parts, axis=1) + y_off + pg_ref[_P_DSKIP:_P_DSKIP + 1, 0:W] * xs

        upd = lax.dot_general(b_b, (xdt * to_end_x).astype(BF16), (((0,), (0,)), ((), ())),
                              preferred_element_type=F32)
        state_ref[...] = state * ea_x[L - 1:L, :] + upd

        y = y * _silu(z_ref[rs, :].astype(F32))
        ms = jnp.mean(y * y, axis=-1, keepdims=True)
        o_ref[rs, :] = (y * lax.rsqrt(ms + RMS_EPS) * pg_ref[_P_NORM:_P_NORM + 1, 0:W]).astype(o_ref.dtype)
        return carry

    lax.fori_loop(0, chunks, chunk, 0, unroll=8)


def ssd_mixer(proj, dt_raw, conv_w, conv_b, dt_bias, a_log, d_skip, norm_w, *, batch, seq, col_off, chunks=16):
    t = batch * seq
    chunks = min(chunks, seq // CHUNK)
    nchunk = seq // (CHUNK * chunks)
    W, N, L, G = GROUP_W, D_STATE, CHUNK * chunks, N_GROUPS
    zb, xb, bb, cb_ = col_off["z"] // W, col_off["xs"] // W, col_off["B"] // N, col_off["C"] // N

    def rows(b, g, c):
        return b * nchunk + c

    def per_group(v):
        r = v.shape[0]
        x_part = v[:, :D_SSD].reshape(r, G, W)
        b_part = v[:, D_SSD:D_SSD + G * N].reshape(r, G, N)
        c_part = v[:, D_SSD + G * N:].reshape(r, G, N)
        return jnp.concatenate([x_part, b_part, c_part], axis=2).transpose(1, 0, 2)

    def x_only(v):
        return jnp.pad(v.astype(F32).reshape(G, 1, W), ((0, 0), (0, 0), (0, 2 * N)))

    pg = jnp.concatenate([per_group(conv_w.astype(F32)), per_group(conv_b.astype(F32).reshape(1, -1)),
                          x_only(jnp.repeat(d_skip, HEAD_DIM)), x_only(norm_w),
                          jnp.zeros((G, 1, W + 2 * N), F32)], axis=1)
    pad = 128 - N_HEADS
    pdt = jnp.pad(jnp.stack([dt_bias.astype(F32), a_log.astype(F32)]), ((0, 6), (0, pad)))

    in_specs = [
        pl.BlockSpec((L, W), lambda b, g, c: (rows(b, g, c), zb + g)),
        pl.BlockSpec((L, W), lambda b, g, c: (rows(b, g, c), xb + g)),
        pl.BlockSpec((L, N), lambda b, g, c: (rows(b, g, c), bb + g)),
        pl.BlockSpec((L, N), lambda b, g, c: (rows(b, g, c), cb_ + g)),
        pl.BlockSpec((L, 128), lambda b, g, c: (rows(b, g, c), 0)),
        pl.BlockSpec((None, 8, W + 2 * N), lambda b, g, c: (g, 0, 0)),
        pl.BlockSpec((8, 128), lambda b, g, c: (0, 0)),
    ]
    return pl.pallas_call(
        functools.partial(_ssd_kernel, chunks=chunks),
        grid=(batch, N_GROUPS, nchunk),
        in_specs=in_specs,
        out_specs=pl.BlockSpec((L, W), lambda b, g, c: (rows(b, g, c), g)),
        out_shape=jax.ShapeDtypeStruct((t, D_SSD), BF16),
        scratch_shapes=[pltpu.VMEM((N, W), F32), pltpu.VMEM((CHUNK + CARRY_ROWS, W + 2 * N), F32)],
        compiler_params=_cparams(("parallel", "parallel", "arbitrary")),
        name="ssd_mixer",
    )(proj, proj, proj, proj, dt_raw, pg, pdt)


def _shortconv_kernel(b_ref, c_ref, h_ref, w_ref, nw_ref, o_ref, ext_ref, y_ref, *, rows, col_chunk):
    s = pl.program_id(1)
    d = D_CONV

    @pl.when(s == 0)
    def _():
        ext_ref[0:CARRY_ROWS, :] = jnp.zeros((CARRY_ROWS, d), F32)

    ss = jnp.zeros((rows, 1), F32)
    for j in range(0, d, col_chunk):
        sl = slice(j, j + col_chunk)
        ext_ref[CARRY_ROWS:CARRY_ROWS + rows, sl] = c_ref[:, sl].astype(F32) * h_ref[:, sl].astype(F32)
        acc = ext_ref[CARRY_ROWS:CARRY_ROWS + rows, sl] * w_ref[2:3, sl]
        for k in range(2):
            acc = acc + ext_ref[CARRY_ROWS - 2 + k:CARRY_ROWS - 2 + k + rows, sl] * w_ref[k:k + 1, sl]
        y = b_ref[:, sl].astype(F32) * acc
        y_ref[:, sl] = y
        ss = ss + jnp.sum(y * y, axis=-1, keepdims=True)
    ext_ref[0:CARRY_ROWS, :] = ext_ref[rows:rows + CARRY_ROWS, :]
    inv = lax.rsqrt(ss * (1.0 / d) + RMS_EPS)
    for j in range(0, d, col_chunk):
        sl = slice(j, j + col_chunk)
        o_ref[:, sl] = (y_ref[:, sl] * inv * nw_ref[:, sl]).astype(o_ref.dtype)


def shortconv_mixer(proj, conv_w, norm_w, *, batch, seq, col_off, rows=512, col_chunk=512):
    t = batch * seq
    rows = min(rows, seq)
    nblk = seq // rows
    d = D_CONV
    bb, cb_, hb = col_off["scb"] // d, col_off["scc"] // d, col_off["sch"] // d
    return pl.pallas_call(
        functools.partial(_shortconv_kernel, rows=rows, col_chunk=col_chunk),
        grid=(batch, nblk),
        in_specs=[
            pl.BlockSpec((rows, d), lambda b, s: (b * nblk + s, bb)),
            pl.BlockSpec((rows, d), lambda b, s: (b * nblk + s, cb_)),
            pl.BlockSpec((rows, d), lambda b, s: (b * nblk + s, hb)),
            pl.BlockSpec((3, d), lambda b, s: (0, 0)),
            pl.BlockSpec((1, d), lambda b, s: (0, 0)),
        ],
        out_specs=pl.BlockSpec((rows, d), lambda b, s: (b * nblk + s, 0)),
        out_shape=jax.ShapeDtypeStruct((t, d), BF16),
        scratch_shapes=[pltpu.VMEM((rows + CARRY_ROWS, d), F32), pltpu.VMEM((rows, d), F32)],
        compiler_params=_cparams(("parallel", "arbitrary")),
        name="shortconv_mixer",
    )(proj, proj, proj, conv_w.astype(F32), norm_w.astype(F32).reshape(1, d))


def _outproj_kernel(a1_ref, a2_ref, b1_ref, b2_ref, r_ref, o_ref):
    acc = jnp.dot(a1_ref[...], b1_ref[...], preferred_element_type=F32)
    acc = acc + jnp.dot(a2_ref[...], b2_ref[...], preferred_element_type=F32)
    o_ref[...] = acc + r_ref[...]


def out_projection(y_ssd, y_sc, w_out, x, *, tm=1024, tn=512):
    m, kh = y_ssd.shape
    _, n = w_out.shape
    tm = min(tm, m)
    return pl.pallas_call(
        _outproj_kernel,
        grid=(m // tm, n // tn),
        in_specs=[
            pl.BlockSpec((tm, kh), lambda i, j: (i, 0)),
            pl.BlockSpec((tm, kh), lambda i, j: (i, 0)),
            pl.BlockSpec((kh, tn), lambda i, j: (0, j)),
            pl.BlockSpec((kh, tn), lambda i, j: (1, j)),
            pl.BlockSpec((tm, tn), lambda i, j: (i, j)),
        ],
        out_specs=pl.BlockSpec((tm, tn), lambda i, j: (i, j)),
        out_shape=jax.ShapeDtypeStruct((m, n), F32),
        compiler_params=_cparams(("parallel", "parallel")),
        name="out_projection",
    )(y_ssd, y_sc, w_out, w_out, x)


def _top_values(x, n):
    outs = []
    for r in range(n):
        m = jnp.max(x, axis=0, keepdims=True)
        outs.append(m)
        if r < n - 1:
            x = jnp.where(x >= m, -jnp.inf, x)
    return outs


_R_PHI, _R_E0, _R_E1 = 0, 1, 2


def _route_kernel(x_ref, wq_ref, keys_ref, r_ref, q_ref):
    nt = (((1,), (1,)), ((), ()))
    q_ref[...] = jnp.dot(x_ref[...], wq_ref[...], preferred_element_type=F32).astype(q_ref.dtype)

    for h in range(PEER_HEADS):
        q0 = q_ref[:, 2 * h * N_KEYS:(2 * h + 1) * N_KEYS]
        q1 = q_ref[:, (2 * h + 1) * N_KEYS:(2 * h + 2) * N_KEYS]
        s0 = lax.dot_general(keys_ref[2 * h], q0, nt, preferred_element_type=F32)
        s1 = lax.dot_general(keys_ref[2 * h + 1], q1, nt, preferred_element_type=F32)
        top0 = _top_values(s0, TOPK + 1)
        top1 = _top_values(s1, TOPK + 1)
        t1 = jnp.concatenate(top1, axis=0)
        pieces = [top0[r] + t1[0:(TOPK + 1) // (r + 1)] for r in range(TOPK + 1)]
        n_cand = sum(p.shape[0] for p in pieces)
        pieces.append(jnp.full((-n_cand % 8, s0.shape[1]), -jnp.inf, F32))
        cand = jnp.concatenate(pieces, axis=0)
        best = _top_values(cand, TOPK + 1)
        m = best[0]
        z = jnp.exp(best[0] - m)
        for r in range(1, TOPK):
            z = z + jnp.exp(best[r] - m)
        inv_z = 0.5 / z
        tau = 0.5 * (best[TOPK - 1] + best[TOPK])
        m0, m1 = top0[0], top1[0]
        r_ref[_R_E0, h] = jnp.exp(s0 - m0)
        r_ref[_R_E1, h] = jnp.exp(s1 - m1) * inv_z
        r_ref[_R_PHI, h] = jnp.exp((tau - m1) - s0) * inv_z


def peer_routing(xn, wq, keys, *, tb=512):
    t, d = xn.shape
    tb = min(tb, t)
    return pl.pallas_call(
        _route_kernel,
        grid=(t // tb,),
        in_specs=[pl.BlockSpec((tb, d), lambda i: (i, 0)),
                  pl.BlockSpec(wq.shape, lambda i: (0, 0), pipeline_mode=pl.Buffered(1)),
                  pl.BlockSpec((2 * PEER_HEADS, N_KEYS, N_KEYS), lambda i: (0, 0, 0))],
        out_specs=pl.BlockSpec((3, PEER_HEADS, N_KEYS, tb), lambda i: (0, 0, 0, i)),
        out_shape=jax.ShapeDtypeStruct((3, PEER_HEADS, N_KEYS, t), F32),
        scratch_shapes=[pltpu.VMEM((tb, wq.shape[1]), xn.dtype)],
        compiler_params=_cparams(("parallel",)),
        name="peer_routing",
    )(xn, wq, keys)


def _gelu_times2(x):
    return x * (1.0 + lax.erf(x * (1.0 / math.sqrt(2.0))))


def _peer_dense_kernel(x_ref, u_ref, v_ref, r_ref, o_ref, a_ref, acc_ref, *, te):
    et = pl.program_id(1)
    nblk = te // N_KEYS
    tb = x_ref.shape[0]
    tq = 128
    nt = (((1,), (1,)), ((), ()))

    @pl.when(et == 0)
    def _():
        acc_ref[...] = jnp.zeros_like(acc_ref)

    ht = lax.dot_general(u_ref[...], x_ref[...], nt, preferred_element_type=F32)
    for j in range(nblk):
        i = et * nblk + j
        rows = slice(j * N_KEYS, (j + 1) * N_KEYS)
        phi_rows = [r_ref[_R_PHI, h, pl.ds(i, 1), :] for h in range(PEER_HEADS)]
        e0_rows = [r_ref[_R_E0, h, pl.ds(i, 1), :] for h in range(PEER_HEADS)]
        for q in range(tb // tq):
            ts = slice(q * tq, (q + 1) * tq)
            gate = None
            for h in range(PEER_HEADS):
                e1 = r_ref[_R_E1, h, :, ts]
                sel = jnp.where(e1 >= phi_rows[h][:, ts], e1, 0.0) * e0_rows[h][:, ts]
                gate = sel if gate is None else gate + sel
            act = _gelu_times2(ht[rows, ts]) * gate
            a_ref[ts, rows] = act.T.astype(a_ref.dtype)
    acc_ref[...] += jnp.dot(a_ref[...], v_ref[...], preferred_element_type=F32)

    @pl.when(et == pl.num_programs(1) - 1)
    def _():
        o_ref[...] = acc_ref[...].astype(o_ref.dtype)


def peer_dense(xn, u, v, r, *, tb=512, te=512):
    t, d = xn.shape
    e = u.shape[0]
    tb = min(tb, t)
    return pl.pallas_call(
        functools.partial(_peer_dense_kernel, te=te),
        grid=(t // tb, e // te),
        in_specs=[
            pl.BlockSpec((tb, d), lambda i, j: (i, 0)),
            pl.BlockSpec((te, d), lambda i, j: (j, 0)),
            pl.BlockSpec((te, d), lambda i, j: (j, 0)),
            pl.BlockSpec((3, PEER_HEADS, N_KEYS, tb), lambda i, j: (0, 0, 0, i)),
        ],
        out_specs=pl.BlockSpec((tb, d), lambda i, j: (i, 0)),
        out_shape=jax.ShapeDtypeStruct((t, d), BF16),
        scratch_shapes=[pltpu.VMEM((tb, te), BF16), pltpu.VMEM((tb, d), F32)],
        compiler_params=_cparams(("parallel", "arbitrary")),
        name="peer_dense",
    )(xn, u, v, r)


def _final_kernel(h_ref, p_ref, w_ref, o_ref):
    x = h_ref[...] + p_ref[...]
    ms = jnp.mean(x * x, axis=-1, keepdims=True)
    o_ref[...] = x * lax.rsqrt(ms + RMS_EPS) * w_ref[...]


def final_norm(h1, peer, w, *, tm=512):
    t, d = h1.shape
    return pl.pallas_call(
        _final_kernel,
        grid=(t // tm,),
        in_specs=[pl.BlockSpec((tm, d), lambda i: (i, 0)), pl.BlockSpec((tm, d), lambda i: (i, 0)),
                  pl.BlockSpec((1, d), lambda i: (0, 0))],
        out_specs=pl.BlockSpec((tm, d), lambda i: (i, 0)),
        out_shape=jax.ShapeDtypeStruct((t, d), F32),
        compiler_params=_cparams(("parallel",)),
        name="final_norm",
    )(h1, peer, w.reshape(1, d))


def kernel(x, w_in, ssd_conv_w, ssd_conv_b, ssd_dt_bias, ssd_a_log, ssd_d, ssd_norm_w, sc_conv_w, sc_norm_w, w_out,
           norm_mix_w, norm_ffn_w, peer_w_query, peer_sub_keys, peer_u, peer_v, norm_final_w):
    batch, seq, d = x.shape
    t = batch * seq
    h = x.reshape(t, d)
    assert w_in.shape[0] == 1, "single-layer block"
    o_c = 2 * D_SSD + N_GROUPS * D_STATE
    o2 = o_c + N_GROUPS * D_STATE
    o3 = o2 + N_HEADS
    ssd_off = {"z": 0, "xs": D_SSD, "B": 2 * D_SSD, "C": o_c}
    sc_off = {"scb": 0, "scc": D_CONV, "sch": 2 * D_CONV}
    wt = w_in[0].T.astype(BF16)
    keys = peer_sub_keys[0].reshape(2 * PEER_HEADS, N_KEYS, N_KEYS).astype(BF16)

    xn = rmsnorm_rows(h, norm_mix_w[0])
    proj_ssd = projection(xn, wt, n=o2, tm=1024, tn=1024, out_dtype=BF16, name="in_projection_ssd")
    proj_sc, w_out_b, wq_b, u_b, v_b = projection(
        xn, wt, n=3 * D_CONV, n_start=o3, tm=1024, tn=768, out_dtype=BF16, name="in_projection_sc",
        casts=(w_out[0], peer_w_query[0], peer_u[0], peer_v[0]))
    dt_raw = projection(xn, wt, n=128, n_start=o2, tm=1024, tn=128, out_dtype=F32, name="dt_projection")
    y_ssd = ssd_mixer(proj_ssd, dt_raw, ssd_conv_w[0], ssd_conv_b[0], ssd_dt_bias[0], ssd_a_log[0], ssd_d[0],
                      ssd_norm_w[0], batch=batch, seq=seq, col_off=ssd_off)
    y_sc = shortconv_mixer(proj_sc, sc_conv_w[0], sc_norm_w[0], batch=batch, seq=seq, col_off=sc_off)
    h = out_projection(y_ssd, y_sc, w_out_b, h)

    xn2 = rmsnorm_rows(h, norm_ffn_w[0])
    r = peer_routing(xn2, wq_b, keys)
    peer = peer_dense(xn2, u_b, v_b, r)
    out = final_norm(h, peer, norm_final_w)
    return out.reshape(batch, seq, d)
```

```python
import functools
import math

import jax
import jax.numpy as jnp
from jax import lax
from jax.experimental import pallas as pl
from jax.experimental.pallas import tpu as pltpu

F32 = jnp.float32
BF16 = jnp.bfloat16

D_SSD = 4096
D_CONV = 4096
HEAD_DIM = 64
N_HEADS = 64
N_GROUPS = 8
HEADS_PER_GROUP = 8
D_STATE = 128
CHUNK = 128
GROUP_W = HEADS_PER_GROUP * HEAD_DIM
PEER_HEADS = 8
N_KEYS = 128
TOPK = 16
RMS_EPS = 1e-6

VMEM_LIMIT_BYTES = 58 * 1024 * 1024
CARRY_ROWS = 8


def _cparams(sem):
    return pltpu.CompilerParams(dimension_semantics=sem, vmem_limit_bytes=VMEM_LIMIT_BYTES)


_NT = (((1,), (1,)), ((), ()))


def _rms_bf16(x, w):
    ms = jnp.mean(x * x, axis=-1, keepdims=True)
    return (x * lax.rsqrt(ms + RMS_EPS) * w).astype(BF16)


def _proj_kernel(a_ref, b_ref, *refs):
    n_cast = len(refs) // 2
    src_refs, o_ref, dst_refs = refs[:n_cast], refs[n_cast], refs[n_cast + 1:]
    o_ref[...] = lax.dot_general(a_ref[...], b_ref[...], _NT, preferred_element_type=F32).astype(o_ref.dtype)
    for src, dst in zip(src_refs, dst_refs):
        dst[...] = src[...].astype(dst.dtype)


def projection(a, w_t, *, n, n_start=0, tm, tn, out_dtype, name, casts=()):
    m, kdim = a.shape
    tm = min(tm, m)
    j0 = n_start // tn
    nj = n // tn
    steps = (m // tm) * nj
    cast_specs = []
    for c in casts:
        rows = c.shape[0] // steps
        assert rows * steps == c.shape[0] and rows % 16 == 0, "side-cast rows must split evenly over the grid steps"
        cast_specs.append(pl.BlockSpec((rows, c.shape[1]), lambda i, j: (i * nj + j, 0)))
    out = pl.pallas_call(
        _proj_kernel,
        grid=(m // tm, nj),
        in_specs=[pl.BlockSpec((tm, kdim), lambda i, j: (i, 0)),
                  pl.BlockSpec((pl.Element(tn), pl.Element(kdim)),
                               lambda i, j: (pl.multiple_of(n_start + j * tn, 16), 0)),
                  *cast_specs],
        out_specs=[pl.BlockSpec((tm, tn), lambda i, j: (i, j)), *cast_specs],
        out_shape=[jax.ShapeDtypeStruct((m, n), out_dtype)] + [jax.ShapeDtypeStruct(c.shape, BF16) for c in casts],
        compiler_params=_cparams(("parallel", "parallel")),
        name=name,
    )(a, w_t, *casts)
    return out if casts else out[0]


def _rmsnorm_kernel(x_ref, w_ref, o_ref):
    o_ref[...] = _rms_bf16(x_ref[...], w_ref[...])


def rmsnorm_rows(x, w, *, tm=512):
    t, d = x.shape
    return pl.pallas_call(
        _rmsnorm_kernel,
        grid=(t // tm,),
        in_specs=[pl.BlockSpec((tm, d), lambda i: (i, 0)), pl.BlockSpec((1, d), lambda i: (0, 0))],
        out_specs=pl.BlockSpec((tm, d), lambda i: (i, 0)),
        out_shape=jax.ShapeDtypeStruct((t, d), BF16),
        compiler_params=_cparams(("parallel",)),
        name="rmsnorm_rows",
    )(x, w.reshape(1, d))


def _silu(x):
    hx = 0.5 * x
    return hx + hx * jnp.tanh(hx)


def _softplus(x):
    return jnp.maximum(x, 0.0) + jnp.log(1.0 + jnp.exp(-jnp.abs(x)))


_P_CONV, _P_BIAS, _P_DSKIP, _P_NORM = 0, 4, 5, 6


def _ssd_kernel(z_ref, xs_ref, b_ref, c_ref, dt_ref, pg_ref, pdt_ref, o_ref, state_ref, ext_ref, *, chunks):
    g = pl.program_id(1)
    c = pl.program_id(2)
    L = CHUNK
    W = GROUP_W
    N = D_STATE
    XBC = W + 2 * N

    @pl.when(c == 0)
    def _():
        state_ref[...] = jnp.zeros_like(state_ref)
        ext_ref[0:CARRY_ROWS, :] = jnp.zeros((CARRY_ROWS, XBC), F32)

    row = lax.broadcasted_iota(jnp.int32, (L, L), 0)
    col = lax.broadcasted_iota(jnp.int32, (L, L), 1)
    causal = row >= col
    tril = causal.astype(BF16)
    er = lax.broadcasted_iota(jnp.int32, (128, W), 0)
    ec = lax.broadcasted_iota(jnp.int32, (128, W), 1)
    expand = (er == ec // HEAD_DIM).astype(BF16)

    def chunk(ci, carry):
        rs = pl.ds(pl.multiple_of(ci * L, L), L)
        ext_ref[CARRY_ROWS:CARRY_ROWS + L, 0:W] = xs_ref[rs, :].astype(F32)
        ext_ref[CARRY_ROWS:CARRY_ROWS + L, W:W + N] = b_ref[rs, :].astype(F32)
        ext_ref[CARRY_ROWS:CARRY_ROWS + L, W + N:XBC] = c_ref[rs, :].astype(F32)
        acc = ext_ref[CARRY_ROWS:CARRY_ROWS + L, :] * pg_ref[_P_CONV + 3:_P_CONV + 4, :]
        for k in range(3):
            acc = acc + ext_ref[CARRY_ROWS - 3 + k:CARRY_ROWS - 3 + k + L, :] * pg_ref[_P_CONV + k:_P_CONV + k + 1, :]
        xbc = _silu(acc + pg_ref[_P_BIAS:_P_BIAS + 1, :])
        ext_ref[0:CARRY_ROWS, :] = ext_ref[L:L + CARRY_ROWS, :]
        xs = xbc[:, 0:W]
        b_b = xbc[:, W:W + N].astype(BF16)
        c_b = xbc[:, W + N:XBC].astype(BF16)

        dt_all = _softplus(dt_ref[rs, :] + pdt_ref[0:1, :])
        da_all = dt_all * (-jnp.exp(pdt_ref[1:2, :]))
        shift = (128 - g * HEADS_PER_GROUP) % 128
        dt_g = pltpu.roll(dt_all, shift, 1)
        da_g = pltpu.roll(da_all, shift, 1)

        hi = da_g.astype(BF16)
        r1 = da_g - hi.astype(F32)
        mid = r1.astype(BF16)
        lo = (r1 - mid.astype(F32)).astype(BF16)
        acum = (jnp.dot(tril, hi, preferred_element_type=F32)
                + jnp.dot(tril, mid, preferred_element_type=F32)
                + jnp.dot(tril, lo, preferred_element_type=F32))
        acum_t = acum.T
        a_last = acum[L - 1:L, :]

        def expand_heads(v):
            vh = v.astype(BF16)
            vl = (v - vh.astype(F32)).astype(BF16)
            return jnp.dot(vh, expand, preferred_element_type=F32) + jnp.dot(vl, expand, preferred_element_type=F32)

        dt_x = expand_heads(dt_g)
        ea_x = expand_heads(jnp.exp(acum))
        to_end_x = expand_heads(jnp.exp(a_last - acum))

        xdt = xs * dt_x
        xdt_b = xdt.astype(BF16)

        cb = lax.dot_general(c_b, b_b, (((1,), (1,)), ((), ())), preferred_element_type=F32)
        state = state_ref[...]
        y_off = jnp.dot(c_b, state.astype(BF16), preferred_element_type=F32) * ea_x

        y_parts = []
        for r in range(HEADS_PER_GROUP):
            seg = acum[:, r:r + 1] - acum_t[r:r + 1, :]
            decay = jnp.exp(jnp.where(causal, seg, -jnp.inf))
            m = (cb * decay).astype(BF16)
            y_parts.append(jnp.dot(m, xdt_b[:, r * HEAD_DIM:(r + 1) * HEAD_DIM], preferred_element_type=F32))
        y = jnp.concatenate(y_parts, axis=1) + y_off + pg_ref[_P_DSKIP:_P_DSKIP + 1, 0:W] * xs

        upd = lax.dot_general(b_b, (xdt * to_end_x).astype(BF16), (((0,), (0,)), ((), ())),
                              preferred_element_type=F32)
        state_ref[...] = state * ea_x[L - 1:L, :] + upd

        y = y * _silu(z_ref[rs, :].astype(F32))
        ms = jnp.mean(y * y, axis=-1, keepdims=True)
        o_ref[rs, :] = (y * lax.rsqrt(ms + RMS_EPS) * pg_ref[_P_NORM:_P_NORM + 1, 0:W]).astype(o_ref.dtype)
        return carry

    lax.fori_loop(0, chunks, chunk, 0, unroll=8)


def ssd_mixer(proj, dt_raw, conv_w, conv_b, dt_bias, a_log, d_skip, norm_w, *, batch, seq, col_off, chunks=16):
    t = batch * seq
    chunks = min(chunks, seq // CHUNK)
    nchunk = seq // (CHUNK * chunks)
    W, N, L, G = GROUP_W, D_STATE, CHUNK * chunks, N_GROUPS
    zb, xb, bb, cb_ = col_off["z"] // W, col_off["xs"] // W, col_off["B"] // N, col_off["C"] // N

    def rows(b, g, c):
        return b * nchunk + c

    def per_group(v):
        r = v.shape[0]
        x_part = v[:, :D_SSD].reshape(r, G, W)
        b_part = v[:, D_SSD:D_SSD + G * N].reshape(r, G, N)
        c_part = v[:, D_SSD + G * N:].reshape(r, G, N)
        return jnp.concatenate([x_part, b_part, c_part], axis=2).transpose(1, 0, 2)

    def x_only(v):
        return jnp.pad(v.astype(F32).reshape(G, 1, W), ((0, 0), (0, 0), (0, 2 * N)))

    pg = jnp.concatenate([per_group(conv_w.astype(F32)), per_group(conv_b.astype(F32).reshape(1, -1)),
                          x_only(jnp.repeat(d_skip, HEAD_DIM)), x_only(norm_w),
                          jnp.zeros((G, 1, W + 2 * N), F32)], axis=1)
    pad = 128 - N_HEADS
    pdt = jnp.pad(jnp.stack([dt_bias.astype(F32), a_log.astype(F32)]), ((0, 6), (0, pad)))

    in_specs = [
        pl.BlockSpec((L, W), lambda b, g, c: (rows(b, g, c), zb + g)),
        pl.BlockSpec((L, W), lambda b, g, c: (rows(b, g, c), xb + g)),
        pl.BlockSpec((L, N), lambda b, g, c: (rows(b, g, c), bb + g)),
        pl.BlockSpec((L, N), lambda b, g, c: (rows(b, g, c), cb_ + g)),
        pl.BlockSpec((L, 128), lambda b, g, c: (rows(b, g, c), 0)),
        pl.BlockSpec((None, 8, W + 2 * N), lambda b, g, c: (g, 0, 0)),
        pl.BlockSpec((8, 128), lambda b, g, c: (0, 0)),
    ]
    return pl.pallas_call(
        functools.partial(_ssd_kernel, chunks=chunks),
        grid=(batch, N_GROUPS, nchunk),
        in_specs=in_specs,
        out_specs=pl.BlockSpec((L, W), lambda b, g, c: (rows(b, g, c), g)),
        out_shape=jax.ShapeDtypeStruct((t, D_SSD), BF16),
        scratch_shapes=[pltpu.VMEM((N, W), F32), pltpu.VMEM((CHUNK + CARRY_ROWS, W + 2 * N), F32)],
        compiler_params=_cparams(("parallel", "parallel", "arbitrary")),
        name="ssd_mixer",
    )(proj, proj, proj, proj, dt_raw, pg, pdt)


def _shortconv_kernel(b_ref, c_ref, h_ref, w_ref, nw_ref, o_ref, ext_ref, y_ref, *, rows, col_chunk):
    s = pl.program_id(1)
    d = D_CONV

    @pl.when(s == 0)
    def _():
        ext_ref[0:CARRY_ROWS, :] = jnp.zeros((CARRY_ROWS, d), F32)

    ss = jnp.zeros((rows, 1), F32)
    for j in range(0, d, col_chunk):
        sl = slice(j, j + col_chunk)
        ext_ref[CARRY_ROWS:CARRY_ROWS + rows, sl] = c_ref[:, sl].astype(F32) * h_ref[:, sl].astype(F32)
        acc = ext_ref[CARRY_ROWS:CARRY_ROWS + rows, sl] * w_ref[2:3, sl]
        for k in range(2):
            acc = acc + ext_ref[CARRY_ROWS - 2 + k:CARRY_ROWS - 2 + k + rows, sl] * w_ref[k:k + 1, sl]
        y = b_ref[:, sl].astype(F32) * acc
        y_ref[:, sl] = y
        ss = ss + jnp.sum(y * y, axis=-1, keepdims=True)
    ext_ref[0:CARRY_ROWS, :] = ext_ref[rows:rows + CARRY_ROWS, :]
    inv = lax.rsqrt(ss * (1.0 / d) + RMS_EPS)
    for j in range(0, d, col_chunk):
        sl = slice(j, j + col_chunk)
        o_ref[:, sl] = (y_ref[:, sl] * inv * nw_ref[:, sl]).astype(o_ref.dtype)


def shortconv_mixer(proj, conv_w, norm_w, *, batch, seq, col_off, rows=512, col_chunk=512):
    t = batch * seq
    rows = min(rows, seq)
    nblk = seq // rows
    d = D_CONV
    bb, cb_, hb = col_off["scb"] // d, col_off["scc"] // d, col_off["sch"] // d
    return pl.pallas_call(
        functools.partial(_shortconv_kernel, rows=rows, col_chunk=col_chunk),
        grid=(batch, nblk),
        in_specs=[
            pl.BlockSpec((rows, d), lambda b, s: (b * nblk + s, bb)),
            pl.BlockSpec((rows, d), lambda b, s: (b * nblk + s, cb_)),
            pl.BlockSpec((rows, d), lambda b, s: (b * nblk + s, hb)),
            pl.BlockSpec((3, d), lambda b, s: (0, 0)),
            pl.BlockSpec((1, d), lambda b, s: (0, 0)),
        ],
        out_specs=pl.BlockSpec((rows, d), lambda b, s: (b * nblk + s, 0)),
        out_shape=jax.ShapeDtypeStruct((t, d), BF16),
        scratch_shapes=[pltpu.VMEM((rows + CARRY_ROWS, d), F32), pltpu.VMEM((rows, d), F32)],
        compiler_params=_cparams(("parallel", "arbitrary")),
        name="shortconv_mixer",
    )(proj, proj, proj, conv_w.astype(F32), norm_w.astype(F32).reshape(1, d))


def _outproj_kernel(a1_ref, a2_ref, b1_ref, b2_ref, r_ref, o_ref):
    acc = jnp.dot(a1_ref[...], b1_ref[...], preferred_element_type=F32)
    acc = acc + jnp.dot(a2_ref[...], b2_ref[...], preferred_element_type=F32)
    o_ref[...] = acc + r_ref[...]


def out_projection(y_ssd, y_sc, w_out, x, *, tm=1024, tn=512):
    m, kh = y_ssd.shape
    _, n = w_out.shape
    tm = min(tm, m)
    return pl.pallas_call(
        _outproj_kernel,
        grid=(m // tm, n // tn),
        in_specs=[
            pl.BlockSpec((tm, kh), lambda i, j: (i, 0)),
            pl.BlockSpec((tm, kh), lambda i, j: (i, 0)),
            pl.BlockSpec((kh, tn), lambda i, j: (0, j)),
            pl.BlockSpec((kh, tn), lambda i, j: (1, j)),
            pl.BlockSpec((tm, tn), lambda i, j: (i, j)),
        ],
        out_specs=pl.BlockSpec((tm, tn), lambda i, j: (i, j)),
        out_shape=jax.ShapeDtypeStruct((m, n), F32),
        compiler_params=_cparams(("parallel", "parallel")),
        name="out_projection",
    )(y_ssd, y_sc, w_out, w_out, x)


def _top_values(x, n):
    outs = []
    for r in range(n):
        m = jnp.max(x, axis=0, keepdims=True)
        outs.append(m)
        if r < n - 1:
            x = jnp.where(x >= m, -jnp.inf, x)
    return outs


_R_PHI, _R_E0, _R_E1 = 0, 1, 2


def _route_kernel(h_ref, nw_ref, wq_ref, keys_ref, xn_ref, r_ref, q_ref):
    nt = (((1,), (1,)), ((), ()))
    rows = min(128, h_ref.shape[0])
    for c in range(h_ref.shape[0] // rows):
        xn_ref[c * rows:(c + 1) * rows] = _rms_bf16(h_ref[c * rows:(c + 1) * rows], nw_ref[...])
    cols = 512
    for c in range(q_ref.shape[1] // cols):
        q_ref[:, c * cols:(c + 1) * cols] = jnp.dot(xn_ref[...], wq_ref[:, c * cols:(c + 1) * cols],
                                                    preferred_element_type=F32).astype(q_ref.dtype)

    for h in range(PEER_HEADS):
        q0 = q_ref[:, 2 * h * N_KEYS:(2 * h + 1) * N_KEYS]
        q1 = q_ref[:, (2 * h + 1) * N_KEYS:(2 * h + 2) * N_KEYS]
        s0 = lax.dot_general(keys_ref[2 * h], q0, nt, preferred_element_type=F32)
        s1 = lax.dot_general(keys_ref[2 * h + 1], q1, nt, preferred_element_type=F32)
        top0 = _top_values(s0, TOPK + 1)
        top1 = _top_values(s1, TOPK + 1)
        t1 = jnp.concatenate(top1, axis=0)
        pieces = [top0[r] + t1[0:(TOPK + 1) // (r + 1)] for r in range(TOPK + 1)]
        n_cand = sum(p.shape[0] for p in pieces)
        pieces.append(jnp.full((-n_cand % 8, s0.shape[1]), -jnp.inf, F32))
        cand = jnp.concatenate(pieces, axis=0)
        best = _top_values(cand, TOPK + 1)
        m = best[0]
        z = jnp.exp(best[0] - m)
        for r in range(1, TOPK):
            z = z + jnp.exp(best[r] - m)
        inv_z = 0.5 / z
        tau = 0.5 * (best[TOPK - 1] + best[TOPK])
        m0, m1 = top0[0], top1[0]
        r_ref[_R_E0, h] = jnp.exp(s0 - m0)
        r_ref[_R_E1, h] = jnp.exp(s1 - m1) * inv_z
        r_ref[_R_PHI, h] = jnp.exp((tau - m1) - s0) * inv_z


def peer_routing(h, norm_w, wq, keys, *, tb=512):
    t, d = h.shape
    tb = min(tb, t)
    return pl.pallas_call(
        _route_kernel,
        grid=(t // tb,),
        in_specs=[pl.BlockSpec((tb, d), lambda i: (i, 0)),
                  pl.BlockSpec((1, d), lambda i: (0, 0)),
                  pl.BlockSpec(wq.shape, lambda i: (0, 0), pipeline_mode=pl.Buffered(1)),
                  pl.BlockSpec((2 * PEER_HEADS, N_KEYS, N_KEYS), lambda i: (0, 0, 0))],
        out_specs=[pl.BlockSpec((tb, d), lambda i: (i, 0)),
                   pl.BlockSpec((3, PEER_HEADS, N_KEYS, tb), lambda i: (0, 0, 0, i))],
        out_shape=[jax.ShapeDtypeStruct((t, d), BF16),
                   jax.ShapeDtypeStruct((3, PEER_HEADS, N_KEYS, t), F32)],
        scratch_shapes=[pltpu.VMEM((tb, wq.shape[1]), BF16)],
        compiler_params=_cparams(("parallel",)),
        name="peer_routing",
    )(h, norm_w.reshape(1, d), wq, keys)


def _gelu_times2(x):
    return x * (1.0 + lax.erf(x * (1.0 / math.sqrt(2.0))))


def _peer_dense_kernel(x_ref, u_ref, v_ref, r_ref, o_ref, a_ref, acc_ref, *, te):
    et = pl.program_id(1)
    nblk = te // N_KEYS
    tb = x_ref.shape[0]
    tq = 128
    nt = (((1,), (1,)), ((), ()))

    @pl.when(et == 0)
    def _():
        acc_ref[...] = jnp.zeros_like(acc_ref)

    ht = lax.dot_general(u_ref[...], x_ref[...], nt, preferred_element_type=F32)
    for j in range(nblk):
        i = et * nblk + j
        rows = slice(j * N_KEYS, (j + 1) * N_KEYS)
        phi_rows = [r_ref[_R_PHI, h, pl.ds(i, 1), :] for h in range(PEER_HEADS)]
        e0_rows = [r_ref[_R_E0, h, pl.ds(i, 1), :] for h in range(PEER_HEADS)]
        for q in range(tb // tq):
            ts = slice(q * tq, (q + 1) * tq)
            gate = None
            for h in range(PEER_HEADS):
                e1 = r_ref[_R_E1, h, :, ts]
                sel = jnp.where(e1 >= phi_rows[h][:, ts], e1, 0.0) * e0_rows[h][:, ts]
                gate = sel if gate is None else gate + sel
            act = _gelu_times2(ht[rows, ts]) * gate
            a_ref[ts, rows] = act.T.astype(a_ref.dtype)
    acc_ref[...] += jnp.dot(a_ref[...], v_ref[...], preferred_element_type=F32)

    @pl.when(et == pl.num_programs(1) - 1)
    def _():
        o_ref[...] = acc_ref[...].astype(o_ref.dtype)


def peer_dense(xn, u, v, r, *, tb=512, te=512):
    t, d = xn.shape
    e = u.shape[0]
    tb = min(tb, t)
    return pl.pallas_call(
        functools.partial(_peer_dense_kernel, te=te),
        grid=(t // tb, e // te),
        in_specs=[
            pl.BlockSpec((tb, d), lambda i, j: (i, 0)),
            pl.BlockSpec((te, d), lambda i, j: (j, 0)),
            pl.BlockSpec((te, d), lambda i, j: (j, 0)),
            pl.BlockSpec((3, PEER_HEADS, N_KEYS, tb), lambda i, j: (0, 0, 0, i)),
        ],
        out_specs=pl.BlockSpec((tb, d), lambda i, j: (i, 0)),
        out_shape=jax.ShapeDtypeStruct((t, d), BF16),
        scratch_shapes=[pltpu.VMEM((tb, te), BF16), pltpu.VMEM((tb, d), F32)],
        compiler_params=_cparams(("parallel", "arbitrary")),
        name="peer_dense",
    )(xn, u, v, r)


def _final_kernel(h_ref, p_ref, w_ref, o_ref):
    x = h_ref[...] + p_ref[...]
    ms = jnp.mean(x * x, axis=-1, keepdims=True)
    o_ref[...] = x * lax.rsqrt(ms + RMS_EPS) * w_ref[...]


def final_norm(h1, peer, w, *, tm=512):
    t, d = h1.shape
    return pl.pallas_call(
        _final_kernel,
        grid=(t // tm,),
        in_specs=[pl.BlockSpec((tm, d), lambda i: (i, 0)), pl.BlockSpec((tm, d), lambda i: (i, 0)),
                  pl.BlockSpec((1, d), lambda i: (0, 0))],
        out_specs=pl.BlockSpec((tm, d), lambda i: (i, 0)),
        out_shape=jax.ShapeDtypeStruct((t, d), F32),
        compiler_params=_cparams(("parallel",)),
        name="final_norm",
    )(h1, peer, w.reshape(1, d))


def kernel(x, w_in, ssd_conv_w, ssd_conv_b, ssd_dt_bias, ssd_a_log, ssd_d, ssd_norm_w, sc_conv_w, sc_norm_w, w_out,
           norm_mix_w, norm_ffn_w, peer_w_query, peer_sub_keys, peer_u, peer_v, norm_final_w):
    batch, seq, d = x.shape
    t = batch * seq
    h = x.reshape(t, d)
    assert w_in.shape[0] == 1, "single-layer block"
    o_c = 2 * D_SSD + N_GROUPS * D_STATE
    o2 = o_c + N_GROUPS * D_STATE
    o3 = o2 + N_HEADS
    ssd_off = {"z": 0, "xs": D_SSD, "B": 2 * D_SSD, "C": o_c}
    sc_off = {"scb": 0, "scc": D_CONV, "sch": 2 * D_CONV}
    wt = w_in[0].T.astype(BF16)
    keys = peer_sub_keys[0].reshape(2 * PEER_HEADS, N_KEYS, N_KEYS).astype(BF16)

    xn = rmsnorm_rows(h, norm_mix_w[0])
    proj_ssd = projection(xn, wt, n=o2, tm=1024, tn=1024, out_dtype=BF16, name="in_projection_ssd")
    proj_sc, w_out_b, wq_b, u_b, v_b = projection(
        xn, wt, n=3 * D_CONV, n_start=o3, tm=1024, tn=768, out_dtype=BF16, name="in_projection_sc",
        casts=(w_out[0], peer_w_query[0], peer_u[0], peer_v[0]))
    dt_raw = projection(xn, wt, n=128, n_start=o2, tm=1024, tn=128, out_dtype=F32, name="dt_projection")
    y_ssd = ssd_mixer(proj_ssd, dt_raw, ssd_conv_w[0], ssd_conv_b[0], ssd_dt_bias[0], ssd_a_log[0], ssd_d[0],
                      ssd_norm_w[0], batch=batch, seq=seq, col_off=ssd_off)
    y_sc = shortconv_mixer(proj_sc, sc_conv_w[0], sc_norm_w[0], batch=batch, seq=seq, col_off=sc_off)
    h = out_projection(y_ssd, y_sc, w_out_b, h)

    xn2, r = peer_routing(h, norm_ffn_w[0], wq_b, keys)
    peer = peer_dense(xn2, u_b, v_b, r)
    out = final_norm(h, peer, norm_final_w)
    return out.reshape(batch, seq, d)
```
